```python
import math
import jax, jax.numpy as jnp
from jax import lax
import numpy as np

D_MODEL = 2048
BATCH = 2
SEQ = 4096
DEPTH = 2
DEC_BATCH = 8
DEC_SEQ = 1
PAST_LEN = 16384
PAGE_SIZE = 128

H_A = 8
HD_A = 128
W_A = H_A * HD_A
H_B = 8
HD_B = 128
W_B = H_B * HD_B
H_IDX = 16
D_IDX = 64
TOPK_MAX = 256
W_C = 1024
GROUP = 16
N_GROUPS = W_C // GROUP
N_STATE = 64
N_BUCKETS = 32
MAX_DIST = 128
N_KEYS = 128
N_EXPERTS = N_KEYS * N_KEYS
P_HEADS = 8
P_DKEY = 256
P_TOPK = 16
Q_BLOCK = 128
EPS = 1e-6
PROJ_SIZES = (W_A, W_A, W_A, H_A, W_B, W_B, W_B, H_IDX * D_IDX, D_IDX, H_IDX, W_C, 3 * D_MODEL)
D_PROJ = sum(PROJ_SIZES)

kernel_name = "hybrid_fox_dsa_s5_peer_step"


def rmsnorm(x, g):
    xf = x.astype(jnp.float32)
    y = xf * lax.rsqrt(jnp.mean(xf * xf, axis=-1, keepdims=True) + EPS)
    return (y * g.astype(jnp.float32)).astype(x.dtype)


def adaln(c, w_ada, b_ada):
    m = jax.nn.silu(c) @ w_ada + b_ada
    return jnp.split(m[:, None, :], 6, axis=-1)


def take_rows(a, idx):
    return jax.vmap(lambda ab, ib: ab[ib])(a, idx)


def gather_pages(pool, page_table):
    g = pool[page_table]
    return g.reshape((page_table.shape[0], -1) + pool.shape[2:])


def t5_bucket(dist):
    n = jnp.maximum(dist, 0)
    max_exact = N_BUCKETS // 2
    nf = jnp.maximum(n, 1).astype(jnp.float32)
    large = max_exact + (jnp.log(nf / max_exact) / math.log(MAX_DIST / max_exact)
                         * (N_BUCKETS - max_exact)).astype(jnp.int32)
    large = jnp.minimum(large, N_BUCKETS - 1)
    return jnp.where(n < max_exact, n, large)


def mixer_projections(h, w_in, b_f):
    Bn, T = h.shape[:2]
    split_pts = [int(s) for s in np.cumsum(PROJ_SIZES)[:-1]]
    fq, fk, fv, ff, bq, bk, bv, iq, ik, iw, su, gl = jnp.split(h @ w_in, split_pts, axis=-1)
    logf = jax.nn.log_sigmoid(ff.astype(jnp.float32) + b_f.astype(jnp.float32))
    fox = (fq.reshape(Bn, T, H_A, HD_A), fk.reshape(Bn, T, H_A, HD_A),
           fv.reshape(Bn, T, H_A, HD_A), logf)
    dsa = (bq.reshape(Bn, T, H_B, HD_B), bk.reshape(Bn, T, H_B, HD_B),
           bv.reshape(Bn, T, H_B, HD_B), iq.reshape(Bn, T, H_IDX, D_IDX), ik,
           iw * (H_IDX ** -0.5))
    return fox, dsa, su, gl


def fox_attend(q, k, v, Fq, Fk, qpos, kpos):
    s = jnp.einsum('bqhd,bkhd->bhqk', q, k).astype(jnp.float32) * (HD_A ** -0.5)
    s = s + jnp.moveaxis(Fq, -1, 1)[..., :, None] - jnp.moveaxis(Fk, -1, 1)[..., None, :]
    s = jnp.where(qpos[:, None] >= kpos[None, :], s, -jnp.inf)
    p = jax.nn.softmax(s, axis=-1).astype(v.dtype)
    return jnp.einsum('bhqk,bkhd->bqhd', p, v)


def fox_prompt(q, k, v, logf):
    Bn, T = q.shape[:2]
    F = jnp.cumsum(logf, axis=1)
    kpos = jnp.arange(T)

    def block(i):
        start = i * Q_BLOCK
        qb = lax.dynamic_slice_in_dim(q, start, Q_BLOCK, axis=1)
        Fq = lax.dynamic_slice_in_dim(F, start, Q_BLOCK, axis=1)
        return fox_attend(qb, k, v, Fq, F, start + jnp.arange(Q_BLOCK), kpos)

    out = lax.map(block, jnp.arange(T // Q_BLOCK))
    return jnp.moveaxis(out, 0, 1).reshape(Bn, T, H_A, HD_A)


def fox_sample(q, k_new, v_new, logf_new, k_past, v_past, logf_past):
    S = q.shape[1]
    k = jnp.concatenate([k_past, k_new], axis=1)
    v = jnp.concatenate([v_past, v_new], axis=1)
    logf = jnp.concatenate([logf_past.astype(jnp.float32), logf_new], axis=1)
    F = jnp.cumsum(logf, axis=1)
    L = k.shape[1]
    return fox_attend(q, k, v, F[:, L - S:], F, (L - S) + jnp.arange(S), jnp.arange(L))


def indexer_scores(qi, wi, ki):
    dots = jnp.einsum('bqhd,bkd->bqhk', qi, ki).astype(jnp.float32) * (D_IDX ** -0.5)
    return jnp.einsum('bqh,bqhk->bqk', wi.astype(jnp.float32), jax.nn.relu(dots))


def sparse_attend(q, qpos, sel, k_sel, v_sel, rel_table):
    s = jnp.einsum('bqhd,bqkhd->bhqk', q, k_sel).astype(jnp.float32) * (HD_B ** -0.5)
    dist = qpos[None, :, None] - sel
    bias = rel_table[t5_bucket(dist)].astype(jnp.float32)
    s = s + jnp.moveaxis(bias, -1, 1)
    s = jnp.where((dist >= 0)[:, None], s, -jnp.inf)
    p = jax.nn.softmax(s, axis=-1).astype(v_sel.dtype)
    return jnp.einsum('bhqk,bqkhd->bqhd', p, v_sel)


def dsa_prompt(q, k, v, qi, ki, wi, rel_table):
    Bn, T = q.shape[:2]
    topk = min(TOPK_MAX, T // 4)
    kpos = jnp.arange(T)

    def block(i):
        start = i * Q_BLOCK
        qb = lax.dynamic_slice_in_dim(q, start, Q_BLOCK, axis=1)
        qib = lax.dynamic_slice_in_dim(qi, start, Q_BLOCK, axis=1)
        wib = lax.dynamic_slice_in_dim(wi, start, Q_BLOCK, axis=1)
        qpos = start + jnp.arange(Q_BLOCK)
        sc = indexer_scores(qib, wib, ki)
        sc = jnp.where(qpos[:, None] >= kpos[None, :], sc, -jnp.inf)
        _, sel = lax.top_k(sc, topk)
        return sparse_attend(qb, qpos, sel, take_rows(k, sel), take_rows(v, sel), rel_table)

    out = lax.map(block, jnp.arange(T // Q_BLOCK))
    return jnp.moveaxis(out, 0, 1).reshape(Bn, T, H_B, HD_B)


def dsa_sample(q, k_new, v_new, qi, ki_new, wi, ki_past, k_pool, v_pool, page_table, rel_table):
    S = q.shape[1]
    P = ki_past.shape[1]
    L = P + S
    topk = min(TOPK_MAX, L // 4)
    ki = jnp.concatenate([ki_past, ki_new], axis=1)
    qpos = P + jnp.arange(S)
    sc = indexer_scores(qi, wi, ki)
    sc = jnp.where(qpos[:, None] >= jnp.arange(L)[None, :], sc, -jnp.inf)
    _, sel = lax.top_k(sc, topk)
    from_past = (sel < P)[..., None, None]
    pidx = jnp.minimum(sel, P - 1)
    phys = take_rows(page_table, pidx // PAGE_SIZE)
    off = pidx % PAGE_SIZE
    nidx = jnp.clip(sel - P, 0, S - 1)
    k_sel = jnp.where(from_past, k_pool[phys, off], take_rows(k_new, nidx))
    v_sel = jnp.where(from_past, v_pool[phys, off], take_rows(v_new, nidx))
    return sparse_attend(q, qpos, sel, k_sel, v_sel, rel_table)


def _cmul_combine(e1, e2):
    a1r, a1i, b1r, b1i = e1
    a2r, a2i, b2r, b2i = e2
    return (a1r * a2r - a1i * a2i, a1r * a2i + a1i * a2r,
            a2r * b1r - a2i * b1i + b2r, a2r * b1i + a2i * b1r + b2i)


def s5_layer(u, x0_re, x0_im, lam_re, lam_im, log_dt, b_re, b_im, c_re, c_im, d_skip):
    f32 = jnp.float32
    Bn, T, _ = u.shape
    uf = u.astype(f32).reshape(Bn, T, N_GROUPS, GROUP)
    lr, li = lam_re.astype(f32), lam_im.astype(f32)
    dt = jnp.exp(log_dt.astype(f32))[:, None]
    mag = jnp.exp(lr * dt)
    ab_re, ab_im = mag * jnp.cos(li * dt), mag * jnp.sin(li * dt)
    den = lr * lr + li * li
    nr = ab_re - 1.0
    k_re = (nr * lr + ab_im * li) / den
    k_im = (ab_im * lr - nr * li) / den
    br, bi = b_re.astype(f32), b_im.astype(f32)
    bb_re = k_re[..., None] * br - k_im[..., None] * bi
    bb_im = k_re[..., None] * bi + k_im[..., None] * br
    bu_re = jnp.einsum('gnp,btgp->btgn', bb_re, uf)
    bu_im = jnp.einsum('gnp,btgp->btgn', bb_im, uf)
    x0r, x0i = x0_re.astype(f32), x0_im.astype(f32)
    bu_re = bu_re.at[:, 0].add(ab_re * x0r - ab_im * x0i)
    bu_im = bu_im.at[:, 0].add(ab_re * x0i + ab_im * x0r)
    a_re = jnp.broadcast_to(ab_re, bu_re.shape)
    a_im = jnp.broadcast_to(ab_im, bu_im.shape)
    _, _, xr, xi = lax.associative_scan(_cmul_combine, (a_re, a_im, bu_re, bu_im), axis=1)
    y = (jnp.einsum('gpn,btgn->btgp', c_re.astype(f32), xr)
         - jnp.einsum('gpn,btgn->btgp', c_im.astype(f32), xi))
    y = y.reshape(Bn, T, W_C) + d_skip.astype(f32) * u.astype(f32)
    return y.astype(u.dtype), xr[:, -1], xi[:, -1]


def peer(h, w_q, sub_keys, u_tab, v_tab):
    T = h.shape[0]
    blk = math.gcd(T, Q_BLOCK)

    def block(hb):
        q = (hb @ w_q).reshape(blk, P_HEADS, 2, P_DKEY // 2)
        s = jnp.einsum('thcd,hcnd->thcn', q, sub_keys).astype(jnp.float32)
        s1, i1 = lax.top_k(s[:, :, 0], P_TOPK)
        s2, i2 = lax.top_k(s[:, :, 1], P_TOPK)
        cand = (s1[..., :, None] + s2[..., None, :]).reshape(blk, P_HEADS, P_TOPK * P_TOPK)
        cs, ci = lax.top_k(cand, P_TOPK)
        e1 = jnp.take_along_axis(i1, ci // P_TOPK, axis=-1)
        e2 = jnp.take_along_axis(i2, ci % P_TOPK, axis=-1)
        eid = e1 * N_KEYS + e2
        g = jax.nn.softmax(cs, axis=-1)
        act = jax.nn.gelu(jnp.einsum('td,thkd->thk', hb, u_tab[eid]).astype(jnp.float32))
        return jnp.einsum('thk,thkd->td', (g * act).astype(hb.dtype), v_tab[eid])

    return lax.map(block, h.reshape(T // blk, blk, D_MODEL)).reshape(T, D_MODEL)


def trunk_layer(x, cond, s5_re0, s5_im0, attend, w_ada, b_ada, norm1_g, norm2_g, w_in, b_f,
                lam_re, lam_im, log_dt, b_re, b_im, c_re, c_im, d_skip, w_glu, w_br, w_out,
                peer_wq, peer_keys, peer_u, peer_v):
    Bn, T, _ = x.shape
    sh1, sc1, gt1, sh2, sc2, gt2 = adaln(cond, w_ada, b_ada)
    h = rmsnorm(x, norm1_g) * (1 + sc1) + sh1
    fox, dsa, su, gl = mixer_projections(h, w_in, b_f)
    o_a, o_b = attend(fox, dsa)
    y_c, s5_re, s5_im = s5_layer(su, s5_re0, s5_im0, lam_re, lam_im, log_dt,
                                 b_re, b_im, c_re, c_im, d_skip)
    glu_a, glu_b = jnp.split(jax.nn.gelu(y_c) @ w_glu, 2, axis=-1)
    o_c = glu_a * jax.nn.sigmoid(glu_b)
    g_a, g_b, g_c = jnp.split(jax.nn.sigmoid(gl), 3, axis=-1)
    wa, wb, wc = jnp.split(w_br, [W_A, W_A + W_B], axis=0)
    merged = (g_a * (o_a.reshape(Bn, T, W_A) @ wa) + g_b * (o_b.reshape(Bn, T, W_B) @ wb)
              + g_c * (o_c @ wc))
    x = x + gt1 * (merged @ w_out)
    h2 = rmsnorm(x, norm2_g) * (1 + sc2) + sh2
    x = x + gt2 * peer(h2.reshape(Bn * T, D_MODEL), peer_wq, peer_keys, peer_u, peer_v).reshape(Bn, T, D_MODEL)
    _, fk, fv, logf = fox
    _, bk, bv, _, ik, _ = dsa
    return x, (fk, fv, logf, bk, bv, ik, s5_re, s5_im)


def setup_inputs(seed: int = 0) -> dict:
    key = jax.random.key(seed)
    ks = iter(jax.random.split(key, 64))
    f32 = jnp.float32

    def nrm(shape, scale):
        return jax.random.normal(next(ks), shape, f32) * scale

    n_pages = PAST_LEN // PAGE_SIZE
    n_used = DEC_BATCH * n_pages
    n_pool = n_used + max(1, n_used // 4)
    page_table = jax.random.permutation(next(ks), n_pool)[:n_used].reshape(DEC_BATCH, n_pages).astype(jnp.int32)
    lam_im = jnp.broadcast_to(jnp.pi * jnp.arange(N_STATE, dtype=f32), (DEPTH, N_GROUPS, N_STATE))
    return {
        "x_prompt": nrm((BATCH, SEQ, D_MODEL), 1.0),
        "x_sample": nrm((DEC_BATCH, DEC_SEQ, D_MODEL), 1.0),
        "cache_fox_k": nrm((DEPTH, n_pool, PAGE_SIZE, H_A, HD_A), 1.0),
        "cache_fox_v": nrm((DEPTH, n_pool, PAGE_SIZE, H_A, HD_A), 1.0),
        "cache_fox_logf": jax.nn.log_sigmoid(2.0 + nrm((DEPTH, n_pool, PAGE_SIZE, H_A), 1.0)),
        "cache_dsa_k": nrm((DEPTH, n_pool, PAGE_SIZE, H_B, HD_B), 1.0),
        "cache_dsa_v": nrm((DEPTH, n_pool, PAGE_SIZE, H_B, HD_B), 1.0),
        "cache_dsa_idx_k": nrm((DEPTH, n_pool, PAGE_SIZE, D_IDX), 1.0),
        "state_s5_re": nrm((DEPTH, DEC_BATCH, N_GROUPS, N_STATE), 0.5),
        "state_s5_im": nrm((DEPTH, DEC_BATCH, N_GROUPS, N_STATE), 0.5),
        "page_table": page_table,
        "c_prompt": nrm((BATCH, D_MODEL), 1.0),
        "c_sample": nrm((DEC_BATCH, D_MODEL), 1.0),
        "w_ada": nrm((DEPTH, D_MODEL, 6 * D_MODEL), 0.5 * D_MODEL ** -0.5),
        "b_ada": nrm((DEPTH, 6 * D_MODEL), 0.02),
        "norm1_g": 1.0 + nrm((DEPTH, D_MODEL), 0.02),
        "norm2_g": 1.0 + nrm((DEPTH, D_MODEL), 0.02),
        "w_in": nrm((DEPTH, D_MODEL, D_PROJ), D_MODEL ** -0.5),
        "b_f": 2.0 + nrm((DEPTH, H_A), 0.5),
        "rel_table": nrm((N_BUCKETS, H_B), 0.5),
        "s5_lam_re": -0.5 + nrm((DEPTH, N_GROUPS, N_STATE), 0.01),
        "s5_lam_im": lam_im + nrm((DEPTH, N_GROUPS, N_STATE), 0.01),
        "s5_log_dt": jax.random.uniform(next(ks), (DEPTH, N_GROUPS), f32, math.log(1e-3), math.log(1e-1)),
        "s5_b_re": nrm((DEPTH, N_GROUPS, N_STATE, GROUP), (2 * GROUP) ** -0.5),
        "s5_b_im": nrm((DEPTH, N_GROUPS, N_STATE, GROUP), (2 * GROUP) ** -0.5),
        "s5_c_re": nrm((DEPTH, N_GROUPS, GROUP, N_STATE), (2 * N_STATE) ** -0.5),
        "s5_c_im": nrm((DEPTH, N_GROUPS, GROUP, N_STATE), (2 * N_STATE) ** -0.5),
        "s5_d": nrm((DEPTH, W_C), 1.0),
        "w_glu": nrm((DEPTH, W_C, 2 * W_C), W_C ** -0.5),
        "w_br": nrm((DEPTH, W_A + W_B + W_C, D_MODEL), W_A ** -0.5),
        "w_out": nrm((DEPTH, D_MODEL, D_MODEL), D_MODEL ** -0.5),
        "peer_wq": nrm((DEPTH, D_MODEL, P_HEADS * P_DKEY), D_MODEL ** -0.5),
        "peer_keys": nrm((DEPTH, P_HEADS, 2, N_KEYS, P_DKEY // 2), (P_DKEY // 2) ** -0.5),
        "peer_u": nrm((DEPTH, N_EXPERTS, D_MODEL), D_MODEL ** -0.5),
        "peer_v": nrm((DEPTH, N_EXPERTS, D_MODEL), P_HEADS ** -0.5),
        "final_norm_g": 1.0 + nrm((D_MODEL,), 0.02),
    }


def reference(x_prompt, x_sample, cache_fox_k, cache_fox_v, cache_fox_logf, cache_dsa_k, cache_dsa_v,
              cache_dsa_idx_k, state_s5_re, state_s5_im, page_table, c_prompt, c_sample,
              w_ada, b_ada, norm1_g, norm2_g, w_in, b_f, rel_table, s5_lam_re, s5_lam_im, s5_log_dt,
              s5_b_re, s5_b_im, s5_c_re, s5_c_im, s5_d, w_glu, w_br, w_out,
              peer_wq, peer_keys, peer_u, peer_v, final_norm_g):
    xp, xs = x_prompt, x_sample
    zero_state = jnp.zeros((x_prompt.shape[0], N_GROUPS, N_STATE), jnp.float32)
    rows_prompt, rows_sample = [], []
    for l in range(DEPTH):
        lw = (w_ada[l], b_ada[l], norm1_g[l], norm2_g[l], w_in[l], b_f[l], s5_lam_re[l], s5_lam_im[l],
              s5_log_dt[l], s5_b_re[l], s5_b_im[l], s5_c_re[l], s5_c_im[l], s5_d[l], w_glu[l], w_br[l],
              w_out[l], peer_wq[l], peer_keys[l], peer_u[l], peer_v[l])

        def attend_prompt(fox, dsa):
            return fox_prompt(*fox), dsa_prompt(*dsa, rel_table)

        def attend_sample(fox, dsa, l=l):
            o_a = fox_sample(*fox, gather_pages(cache_fox_k[l], page_table),
                             gather_pages(cache_fox_v[l], page_table),
                             gather_pages(cache_fox_logf[l], page_table))
            o_b = dsa_sample(*dsa, gather_pages(cache_dsa_idx_k[l], page_table),
                             cache_dsa_k[l], cache_dsa_v[l], page_table, rel_table)
            return o_a, o_b

        xp, rp = trunk_layer(xp, c_prompt, zero_state, zero_state, attend_prompt, *lw)
        xs, rs = trunk_layer(xs, c_sample, state_s5_re[l], state_s5_im[l], attend_sample, *lw)
        rows_prompt.append(rp)
        rows_sample.append(rs)
    y_prompt = rmsnorm(xp, final_norm_g)
    y_sample = rmsnorm(xs, final_norm_g)
    fk_p, fv_p, fl_p, bk_p, bv_p, ik_p, sr_p, si_p = [jnp.stack(a) for a in zip(*rows_prompt)]
    fk_s, fv_s, fl_s, bk_s, bv_s, ik_s, sr_s, si_s = [jnp.stack(a) for a in zip(*rows_sample)]
    return (y_prompt, y_sample, fk_p, fv_p, fl_p, bk_p, bv_p, ik_p, sr_p, si_p,
            fk_s, fv_s, fl_s, bk_s, bv_s, ik_s, sr_s, si_s)
```

```python
import functools
import math

import jax
import jax.numpy as jnp
from jax import lax
from jax.experimental import pallas as pl
from jax.experimental.pallas import tpu as pltpu

F32 = jnp.float32
BF16 = jnp.bfloat16
I32 = jnp.int32

N_HEADS = 8
HEAD_DIM = 128
H_IDX = 16
D_IDX = 64
TOPK_MAX = 256
GROUP = 16
N_STATE = 64
N_BUCKETS = 32
MAX_DIST = 128
N_KEYS = 128
P_HEADS = 8
P_TOPK = 16
PAGE = 128
EPS = 1e-6

V7X_VMEM_LIMIT_BYTES = 56 * 1024 * 1024
LANES = 128
NEG = -1e30
INT_MIN = -2 ** 31

_PAIRS = sorted([(i, j) for i in range(P_TOPK) for j in range(P_TOPK) if (i + 1) * (j + 1) <= P_TOPK],
                key=lambda p: p[0] * P_TOPK + p[1])
_N_CAND = 56


def _params(*sem):
    return pltpu.CompilerParams(dimension_semantics=sem, vmem_limit_bytes=V7X_VMEM_LIMIT_BYTES)


def _gelu(x):
    return 0.5 * x * (1.0 + jnp.tanh(math.sqrt(2.0 / math.pi) * (x + 0.044715 * (x * x * x))))


def _sigmoid(x):
    return 1.0 / (1.0 + jnp.exp(-x))


def _mm_body(*refs, n_extra, epilogue):
    a_ref, b_ref = refs[0], refs[1]
    extra = refs[2:2 + n_extra]
    outs = refs[2 + n_extra:]
    acc = jnp.dot(a_ref[...].astype(BF16), b_ref[...].astype(BF16), preferred_element_type=F32)
    vals = epilogue(acc, *[e[...] for e in extra]) if epilogue is not None else (acc,) * len(outs)
    for o, v in zip(outs, vals):
        o[...] = v.astype(o.dtype)


def _mm(a, b, out_dtypes, *, tm, tn, order="mn", epilogue=None, extras=()):
    M, K = a.shape
    N = b.shape[1]
    tm, tn = min(tm, M), min(tn, N)
    assert M % tm == 0 and N % tn == 0, (M, N, tm, tn)
    if order == "mn":
        grid = (M // tm, N // tn)
        ij = lambda g0, g1: (g0, g1)
    else:
        grid = (N // tn, M // tm)
        ij = lambda g0, g1: (g1, g0)
    in_specs = [pl.BlockSpec((tm, K), lambda g0, g1: (ij(g0, g1)[0], 0)),
                pl.BlockSpec((K, tn), lambda g0, g1: (0, ij(g0, g1)[1]))]
    args = [a, b]
    for arr, bshape, imap in extras:
        in_specs.append(pl.BlockSpec(bshape, lambda g0, g1, imap=imap: imap(*ij(g0, g1))))
        args.append(arr)
    out_specs = [pl.BlockSpec((tm, tn), lambda g0, g1: ij(g0, g1)) for _ in out_dtypes]
    out_shape = [jax.ShapeDtypeStruct((M, N), dt) for dt in out_dtypes]
    return pl.pallas_call(
        functools.partial(_mm_body, n_extra=len(extras), epilogue=epilogue),
        grid=grid, in_specs=in_specs, out_specs=out_specs, out_shape=out_shape,
        compiler_params=_params("parallel", "parallel"),
    )(*args)


def _mod_extra(mod, rows_per_group, tm, tn):
    r = mod.shape[1]
    return (mod, (None, r, tn), lambda i, j: ((i * tm) // rows_per_group, 0, j))


def _adaln_body(c_ref, w_ref, b_ref, o_ref):
    c = c_ref[...]
    s = c * _sigmoid(c)
    o_ref[...] = jnp.dot(s.astype(BF16), w_ref[...].astype(BF16), preferred_element_type=F32) + b_ref[...]


def _adaln(c, w_ada, b_ada, tn=1024):
    R, D = c.shape
    N = w_ada.shape[1]
    return pl.pallas_call(
        _adaln_body, grid=(N // tn,),
        in_specs=[pl.BlockSpec((R, D), lambda j: (0, 0)),
                  pl.BlockSpec((D, tn), lambda j: (0, j)),
                  pl.BlockSpec((1, tn), lambda j: (0, j))],
        out_specs=pl.BlockSpec((R, tn), lambda j: (0, j)),
        out_shape=jax.ShapeDtypeStruct((R, N), F32),
        compiler_params=_params("parallel"),
    )(c, w_ada, b_ada.reshape(1, N))


def _norm_mod_body(x_ref, g_ref, sc_ref, sh_ref, o_ref):
    x = x_ref[...]
    y = x * lax.rsqrt(jnp.mean(x * x, axis=-1, keepdims=True) + EPS) * g_ref[...]
    o_ref[...] = (y * (1.0 + sc_ref[...]) + sh_ref[...]).astype(o_ref.dtype)


def _norm_mod(x, g, sc, sh, rows_per_group, tm=512):
    M, D = x.shape
    tm = min(tm, M)
    r = sc.shape[1]
    mod_spec = pl.BlockSpec((None, r, D), lambda i: ((i * tm) // rows_per_group, 0, 0))
    return pl.pallas_call(
        _norm_mod_body, grid=(M // tm,),
        in_specs=[pl.BlockSpec((tm, D), lambda i: (i, 0)), pl.BlockSpec((1, D), lambda i: (0, 0)),
                  mod_spec, mod_spec],
        out_specs=pl.BlockSpec((tm, D), lambda i: (i, 0)),
        out_shape=jax.ShapeDtypeStruct((M, D), BF16),
        compiler_params=_params("parallel"),
    )(x, g.reshape(1, D), sc, sh)


def _norm_body(x_ref, g_ref, o_ref):
    x = x_ref[...]
    o_ref[...] = x * lax.rsqrt(jnp.mean(x * x, axis=-1, keepdims=True) + EPS) * g_ref[...]


def _norm(x, g, tm=512):
    M, D = x.shape
    tm = min(tm, M)
    return pl.pallas_call(
        _norm_body, grid=(M // tm,),
        in_specs=[pl.BlockSpec((tm, D), lambda i: (i, 0)), pl.BlockSpec((1, D), lambda i: (0, 0))],
        out_specs=pl.BlockSpec((tm, D), lambda i: (i, 0)),
        out_shape=jax.ShapeDtypeStruct((M, D), F32),
        compiler_params=_params("parallel"),
    )(x, g.reshape(1, D))


def _log_sigmoid(z):
    return jnp.minimum(z, 0.0) - jnp.log(1.0 + jnp.exp(-jnp.abs(z)))


def _gate_body(f_ref, b_ref, logf_ref, cum_ref):
    logf = _log_sigmoid(f_ref[...] + b_ref[...])
    logf_ref[...] = logf
    T = logf.shape[1]
    lane = lax.broadcasted_iota(I32, logf.shape, 1)
    x = logf
    sh = 1
    while sh < T:
        x = x + jnp.where(lane >= sh, pltpu.roll(x, sh, axis=1), 0.0)
        sh *= 2
    cum_ref[...] = x


def _gate_cumsum(f_t, b_col):
    R, T = f_t.shape
    return pl.pallas_call(
        _gate_body,
        out_shape=[jax.ShapeDtypeStruct((R, T), F32), jax.ShapeDtypeStruct((R, T), F32)],
        compiler_params=pltpu.CompilerParams(vmem_limit_bytes=V7X_VMEM_LIMIT_BYTES),
    )(f_t, b_col)


def _flash_body(*refs, mode, tile, scale):
    if mode == "fox":
        q_ref, k_ref, v_ref, fk_ref, o_ref, m_sc, l_sc, acc_sc = refs
    else:
        q_ref, k_ref, v_ref, mask_ref, tz_ref, o_ref, m_sc, l_sc, acc_sc = refs
    qi = pl.program_id(1)
    ki = pl.program_id(2)

    @pl.when(ki == 0)
    def _():
        m_sc[...] = jnp.full(m_sc.shape, -jnp.inf, F32)
        l_sc[...] = jnp.zeros(l_sc.shape, F32)
        acc_sc[...] = jnp.zeros(acc_sc.shape, F32)

    def step(diag):
        if mode == "fox":
            if diag:
                row = lax.broadcasted_iota(I32, (tile, tile), 0)
                col = lax.broadcasted_iota(I32, (tile, tile), 1)
                keep = row >= col
        else:
            shared = mask_ref[0]
            dsel = jnp.minimum(qi - ki, 2)
        for h in range(N_HEADS):
            sl = slice(h * HEAD_DIM, (h + 1) * HEAD_DIM)
            s = lax.dot_general(q_ref[0, :, sl], k_ref[0, :, sl], (((1,), (1,)), ((), ())),
                                preferred_element_type=F32) * scale
            if mode == "fox":
                s = s - fk_ref[0, h:h + 1, :]
                if diag:
                    s = jnp.where(keep, s, NEG)
            else:
                s = s + tz_ref[h, dsel] + shared
            m_prev = m_sc[h]
            m_new = jnp.maximum(m_prev, jnp.max(s, axis=1, keepdims=True))
            alpha = jnp.exp(m_prev - m_new)
            p = jnp.exp(s - m_new)
            l_sc[h] = alpha * l_sc[h] + jnp.sum(p, axis=1, keepdims=True)
            acc_sc[:, sl] = alpha * acc_sc[:, sl] + jnp.dot(p.astype(BF16), v_ref[0, :, sl],
                                                            preferred_element_type=F32)
            m_sc[h] = m_new

    if mode == "fox":
        pl.when(ki < qi)(lambda: step(False))
        pl.when(ki == qi)(lambda: step(True))
    else:
        pl.when(ki <= qi)(lambda: step(False))

    @pl.when(ki == qi)
    def _():
        for h in range(N_HEADS):
            sl = slice(h * HEAD_DIM, (h + 1) * HEAD_DIM)
            o_ref[0, :, sl] = (acc_sc[:, sl] / l_sc[h]).astype(o_ref.dtype)


def _flash(mode, q, k, v, *side, tile):
    B, T, W = q.shape
    nt = T // tile
    qspec = pl.BlockSpec((1, tile, W), lambda b, qi, ki: (b, qi, 0))
    kspec = pl.BlockSpec((1, tile, W), lambda b, qi, ki: (b, jnp.minimum(ki, qi), 0))
    if mode == "fox":
        side_specs = [pl.BlockSpec((1, N_HEADS, tile), lambda b, qi, ki: (b, 0, jnp.minimum(ki, qi)))]
    else:
        side_specs = [pl.BlockSpec((1, tile, tile), lambda b, qi, ki: (b, qi, jnp.minimum(ki, qi))),
                      pl.BlockSpec((N_HEADS, 3, tile, tile), lambda b, qi, ki: (0, 0, 0, 0))]
    return pl.pallas_call(
        functools.partial(_flash_body, mode=mode, tile=tile, scale=HEAD_DIM ** -0.5),
        grid=(B, nt, nt),
        in_specs=[qspec, kspec, kspec] + side_specs,
        out_specs=pl.BlockSpec((1, tile, W), lambda b, qi, ki: (b, qi, 0)),
        out_shape=jax.ShapeDtypeStruct((B, T, W), BF16),
        scratch_shapes=[pltpu.VMEM((N_HEADS, tile, 1), F32), pltpu.VMEM((N_HEADS, tile, 1), F32),
                        pltpu.VMEM((tile, W), F32)],
        compiler_params=_params("parallel", "parallel", "arbitrary"),
    )(q, k, v, *side)


def _sortable_key(x):
    bits = pltpu.bitcast(x, I32)
    return jnp.where(bits < 0, bits ^ jnp.int32(0x7FFFFFFF), bits)


def _kth_largest_key(count_ge, shape, k):
    def body(b, cur):
        cand = cur | lax.shift_left(jnp.int32(1), 31 - b)
        cnt = count_ge(cand ^ jnp.int32(INT_MIN))
        return jnp.where(cnt >= k, cand, cur)
    cur = lax.fori_loop(0, 32, body, jnp.zeros(shape, I32))
    return cur ^ jnp.int32(INT_MIN)


def _dsa_select_body(qi_ref, small_ref, kit_ref, mask_ref, key_sc, cut_sc, *, tq, T, topk, chunk):
    q0 = pl.program_id(1) * tq
    w = small_ref[0][:, N_HEADS:N_HEADS + H_IDX] * (H_IDX ** -0.5 * D_IDX ** -0.5)
    row_c = q0 + lax.broadcasted_iota(I32, (tq, chunk), 0)
    lane_c = lax.broadcasted_iota(I32, (tq, chunk), 1)

    def score_chunk(c, carry):
        c0 = pl.multiple_of(c * chunk, chunk)

        @pl.when(c0 < q0 + tq)
        def _():
            kt = kit_ref[0, :, pl.ds(c0, chunk)]
            acc = jnp.zeros((tq, chunk), F32)
            for h in range(H_IDX):
                d = jnp.dot(qi_ref[0, :, h * D_IDX:(h + 1) * D_IDX], kt, preferred_element_type=F32)
                acc = acc + w[:, h:h + 1] * jnp.maximum(d, 0.0)
            key = jnp.where(c0 + lane_c <= row_c, _sortable_key(acc), jnp.int32(INT_MIN))
            key_sc[:, pl.ds(c0, chunk)] = key

        @pl.when(c0 >= q0 + tq)
        def _():
            key_sc[:, pl.ds(c0, chunk)] = jnp.full((tq, chunk), INT_MIN, I32)
        return carry

    lax.fori_loop(0, T // chunk, score_chunk, 0)

    n_lt = T // LANES
    lane = lax.broadcasted_iota(I32, (tq, LANES), 1)

    def count(ind):
        part = jnp.zeros((tq, LANES), I32)
        for c in range(n_lt):
            part = part + ind(key_sc[:, c * LANES:(c + 1) * LANES], c * LANES + lane)
        return jnp.sum(part, axis=1, keepdims=True)

    one = lambda pred: jnp.where(pred, 1, 0)
    thr = _kth_largest_key(lambda t: count(lambda k, col: one(k >= t)), (tq, 1), topk)
    n_gt = count(lambda k, col: one(k > thr))
    n_ge = count(lambda k, col: one(k >= thr))
    excess = jnp.where(thr > jnp.int32(INT_MIN), n_ge - topk, 0)
    cut_sc[...] = jnp.full((tq, 1), T, I32)

    @pl.when(jnp.max(excess) > 0)
    def _():
        need = topk - n_gt
        nbits = max(1, (T - 1).bit_length())

        def body(b, cut):
            cand = cut | lax.shift_left(jnp.int32(1), nbits - 1 - b)
            cnt = count(lambda k, col: jnp.where(k == thr, one(col < cand), 0))
            return jnp.where(cnt < need, cand, cut)
        cut = lax.fori_loop(0, nbits, body, jnp.zeros((tq, 1), I32))
        cut_sc[...] = jnp.where(excess > 0, cut, T)

    cut = cut_sc[...]
    row = q0 + lax.broadcasted_iota(I32, (tq, LANES), 0)
    for c in range(n_lt):
        k = key_sc[:, c * LANES:(c + 1) * LANES]
        col = c * LANES + lane
        val = jnp.where(k > thr, 0.0, jnp.where(k == thr, jnp.where(col <= cut, 0.0, NEG), NEG))
        mask_ref[0, :, c * LANES:(c + 1) * LANES] = jnp.where(col <= row, val, NEG)


def _dsa_select(qi, small, ki_t, topk, tq=256, chunk=512):
    B, T, _ = qi.shape
    tq = min(tq, T)
    chunk = min(chunk, T)
    return pl.pallas_call(
        functools.partial(_dsa_select_body, tq=tq, T=T, topk=topk, chunk=chunk),
        grid=(B, T // tq),
        in_specs=[pl.BlockSpec((1, tq, H_IDX * D_IDX), lambda b, i: (b, i, 0)),
                  pl.BlockSpec((1, tq, LANES), lambda b, i: (b, i, 0)),
                  pl.BlockSpec((1, D_IDX, T), lambda b, i: (b, 0, 0))],
        out_specs=pl.BlockSpec((1, tq, T), lambda b, i: (b, i, 0)),
        out_shape=jax.ShapeDtypeStruct((B, T, T), F32),
        scratch_shapes=[pltpu.VMEM((tq, T), I32), pltpu.VMEM((tq, 1), I32)],
        compiler_params=_params("parallel", "parallel"),
    )(qi, small, ki_t)


def _t5_bucket(dist):
    n = jnp.maximum(dist, 0)
    max_exact = N_BUCKETS // 2
    nf = jnp.maximum(n, 1).astype(F32)
    large = max_exact + (jnp.log(nf / max_exact) / math.log(MAX_DIST / max_exact)
                         * (N_BUCKETS - max_exact)).astype(I32)
    large = jnp.minimum(large, N_BUCKETS - 1)
    return jnp.where(n < max_exact, n, large)


def _t5_tiles(rel_table, tile):
    i = jnp.arange(tile)
    dist = (jnp.arange(3) * tile)[:, None, None] + i[None, :, None] - i[None, None, :]
    return jnp.moveaxis(rel_table[_t5_bucket(dist)].astype(F32), -1, 0)


def _s5_disc_body(lr_ref, li_ref, ldt_ref, br_ref, bi_ref, abr_ref, abi_ref, bbr_ref, bbi_ref):
    lr, li = lr_ref[...], li_ref[...]
    dt = jnp.exp(ldt_ref[...])
    mag = jnp.exp(lr * dt)
    ab_re, ab_im = mag * jnp.cos(li * dt), mag * jnp.sin(li * dt)
    den = lr * lr + li * li
    nr = ab_re - 1.0
    k_re = (nr * lr + ab_im * li) / den
    k_im = (ab_im * lr - nr * li) / den
    br, bi = br_ref[...], bi_ref[...]
    abr_ref[...] = ab_re
    abi_ref[...] = ab_im
    bbr_ref[...] = k_re * br - k_im * bi
    bbi_ref[...] = k_re * bi + k_im * br


def _s5_discretise(lam_re, lam_im, log_dt, b_re, b_im):
    G, N, P = b_re.shape
    rep = lambda a: jnp.broadcast_to(a.reshape(G * N, 1), (G * N, P))
    ldt = jnp.broadcast_to(log_dt.reshape(G, 1, 1), (G, N, P)).reshape(G * N, P)
    shp = jax.ShapeDtypeStruct((G * N, P), F32)
    abr, abi, bbr, bbi = pl.pallas_call(
        _s5_disc_body, out_shape=[shp] * 4,
        compiler_params=pltpu.CompilerParams(vmem_limit_bytes=V7X_VMEM_LIMIT_BYTES),
    )(rep(lam_re), rep(lam_im), ldt, b_re.reshape(G * N, P), b_im.reshape(G * N, P))
    return (abr[:, 0].reshape(G, N), abi[:, 0].reshape(G, N),
            bbr.reshape(G, N, P), bbi.reshape(G, N, P))


def _s5_scan_body(bu_ref, a_ref, x0_ref, xs_ref, xT_ref, carry_sc, *, tc, S, lc):
    c = pl.program_id(1)

    @pl.when(c == 0)
    def _():
        carry_sc[...] = x0_ref[0]

    for j in range(S // lc):
        slr = slice(j * lc, (j + 1) * lc)
        sli = slice(S + j * lc, S + (j + 1) * lc)
        ar, ai = a_ref[:, slr], a_ref[:, sli]

        def step(t, carry, slr=slr, sli=sli, ar=ar, ai=ai):
            xr, xi = carry
            nr = ar * xr - ai * xi + bu_ref[0, pl.ds(t, 1), slr]
            ni = ar * xi + ai * xr + bu_ref[0, pl.ds(t, 1), sli]
            xs_ref[0, pl.ds(t, 1), slr] = nr
            xs_ref[0, pl.ds(t, 1), sli] = ni
            return nr, ni

        xr, xi = lax.fori_loop(0, tc, step, (carry_sc[:, slr], carry_sc[:, sli]), unroll=min(8, tc))
        carry_sc[:, slr] = xr
        carry_sc[:, sli] = xi

    xT_ref[0] = carry_sc[...]


def _s5_scan(bu, a_row, x0, tc=128, lc=1024):
    B, T, S2 = bu.shape
    S = S2 // 2
    tc = min(tc, T)
    return pl.pallas_call(
        functools.partial(_s5_scan_body, tc=tc, S=S, lc=lc),
        grid=(B, T // tc),
        in_specs=[pl.BlockSpec((1, tc, S2), lambda b, c: (b, c, 0)),
                  pl.BlockSpec((1, S2), lambda b, c: (0, 0)),
                  pl.BlockSpec((1, 1, S2), lambda b, c: (b, 0, 0))],
        out_specs=[pl.BlockSpec((1, tc, S2), lambda b, c: (b, c, 0)),
                   pl.BlockSpec((1, 1, S2), lambda b, c: (b, 0, 0))],
        out_shape=[jax.ShapeDtypeStruct((B, T, S2), F32), jax.ShapeDtypeStruct((B, 1, S2), F32)],
        scratch_shapes=[pltpu.VMEM((1, S2), F32)],
        compiler_params=_params("parallel", "arbitrary"),
    )(bu, a_row, x0)


def _glu_body(y_ref, wa_ref, wb_ref, o_ref):
    y = y_ref[...]
    a = jnp.dot(y, wa_ref[...], preferred_element_type=F32)
    b = jnp.dot(y, wb_ref[...], preferred_element_type=F32)
    o_ref[...] = (a * _sigmoid(b)).astype(o_ref.dtype)


def _glu(y, wa, wb, tm=512):
    M, K = y.shape
    N = wa.shape[1]
    tm = min(tm, M)
    return pl.pallas_call(
        _glu_body, grid=(M // tm,),
        in_specs=[pl.BlockSpec((tm, K), lambda i: (i, 0)), pl.BlockSpec((K, N), lambda i: (0, 0)),
                  pl.BlockSpec((K, N), lambda i: (0, 0))],
        out_specs=pl.BlockSpec((tm, N), lambda i: (i, 0)),
        out_shape=jax.ShapeDtypeStruct((M, N), BF16),
        compiler_params=_params("parallel"),
    )(y, wa, wb)


def _merge_body(oa_ref, ob_ref, oc_ref, wa_ref, wb_ref, wc_ref, ga_ref, gb_ref, gc_ref, o_ref):
    dot = lambda x, w: jnp.dot(x[...], w[...], preferred_element_type=F32)
    m = (_sigmoid(ga_ref[...]) * dot(oa_ref, wa_ref) + _sigmoid(gb_ref[...]) * dot(ob_ref, wb_ref)
         + _sigmoid(gc_ref[...]) * dot(oc_ref, wc_ref))
    o_ref[...] = m.astype(o_ref.dtype)


def _merge(o_a, o_b, o_c, wa, wb, wc, gl, tm=512, tn=512):
    M, K = o_a.shape
    D = wa.shape[1]
    tm = min(tm, M)
    nd = D // tn
    ospec = pl.BlockSpec((tm, K), lambda i, j: (i, 0))
    wspec = pl.BlockSpec((K, tn), lambda i, j: (0, j))
    gspec = lambda g: pl.BlockSpec((tm, tn), lambda i, j, g=g: (i, g * nd + j))
    return pl.pallas_call(
        _merge_body, grid=(M // tm, nd),
        in_specs=[ospec, ospec, ospec, wspec, wspec, wspec, gspec(0), gspec(1), gspec(2)],
        out_specs=pl.BlockSpec((tm, tn), lambda i, j: (i, j)),
        out_shape=jax.ShapeDtypeStruct((M, D), BF16),
        compiler_params=_params("parallel", "parallel"),
    )(o_a, o_b, o_c, wa, wb, wc, gl, gl, gl)


def _top16_rows(s, n_rows):
    iota = lax.broadcasted_iota(I32, s.shape, 0)
    rank = jnp.full(s.shape, P_TOPK, I32)
    vals = []
    work = s
    for k in range(P_TOPK):
        m = jnp.max(work, axis=0, keepdims=True)
        idx = jnp.min(jnp.where(work == m, iota, n_rows), axis=0, keepdims=True)
        hit = iota == idx
        rank = jnp.where(hit, k, rank)
        work = jnp.where(hit, -jnp.inf, work)
        vals.append(m)
    return rank, vals


def _peer_select_body(q_ref, keys_ref, a1_ref, w1_ref, a2_ref, b2_ref, cand_sc, sel_sc):
    tt = q_ref.shape[0]
    nt = (((1,), (1,)), ((), ()))
    for h in range(P_HEADS):
        s1 = lax.dot_general(keys_ref[2 * h], q_ref[:, (2 * h) * N_KEYS:(2 * h + 1) * N_KEYS], nt,
                             preferred_element_type=F32)
        s2 = lax.dot_general(keys_ref[2 * h + 1], q_ref[:, (2 * h + 1) * N_KEYS:(2 * h + 2) * N_KEYS], nt,
                             preferred_element_type=F32)
        r1, v1 = _top16_rows(s1, N_KEYS)
        r2, v2 = _top16_rows(s2, N_KEYS)
        for r, (i, j) in enumerate(_PAIRS):
            cand_sc[r:r + 1, :] = v1[i] + v2[j]
        cand_sc[len(_PAIRS):, :] = jnp.full((_N_CAND - len(_PAIRS), tt), -jnp.inf, F32)
        rc, cv = _top16_rows(cand_sc[...], _N_CAND)
        z = jnp.zeros((1, tt), F32)
        for k in range(P_TOPK):
            z = z + jnp.exp(cv[k] - cv[0])
        sel_sc[...] = jnp.where(rc < P_TOPK, 1, 0)
        wm = [jnp.zeros((1, tt), I32) for _ in range(P_TOPK)]
        for r, (i, j) in enumerate(_PAIRS):
            wm[i] = wm[i] + sel_sc[r:r + 1, :] * (1 << j)
        w1 = jnp.zeros((N_KEYS, tt), I32)
        for i in range(P_TOPK):
            w1 = jnp.where(r1 == i, wm[i], w1)
        a1_ref[h] = jnp.exp(s1 - v1[0]) / z
        w1_ref[h] = w1
        a2_ref[h] = jnp.exp(s2 - v2[0])
        b2_ref[h] = jnp.where(r2 < P_TOPK, lax.shift_left(jnp.int32(1), r2), 0)


def _peer_select(q, keys, tt=256):
    M = q.shape[0]
    tt = min(tt, M)
    tab = pl.BlockSpec((P_HEADS, N_KEYS, tt), lambda i: (0, 0, i))
    return pl.pallas_call(
        _peer_select_body, grid=(M // tt,),
        in_specs=[pl.BlockSpec((tt, q.shape[1]), lambda i: (i, 0)),
                  pl.BlockSpec(keys.shape, lambda i: (0, 0, 0))],
        out_specs=[tab, tab, tab, tab],
        out_shape=[jax.ShapeDtypeStruct((P_HEADS, N_KEYS, M), F32), jax.ShapeDtypeStruct((P_HEADS, N_KEYS, M), I32),
                   jax.ShapeDtypeStruct((P_HEADS, N_KEYS, M), F32), jax.ShapeDtypeStruct((P_HEADS, N_KEYS, M), I32)],
        scratch_shapes=[pltpu.VMEM((_N_CAND, tt), F32), pltpu.VMEM((_N_CAND, tt), I32)],
        compiler_params=_params("parallel"),
    )(q, keys)


def _peer_dense_body(h_ref, u_ref, vt_ref, a1_ref, w1_ref, a2_ref, b2_ref, x_ref, gt_ref, o_ref,
                     acc_sc, ga_sc, *, te):
    j = pl.program_id(1)
    tt = h_ref.shape[0]

    @pl.when(j == 0)
    def _():
        acc_sc[...] = jnp.zeros(acc_sc.shape, F32)

    act = lax.dot_general(u_ref[...], h_ref[...], (((1,), (1,)), ((), ())), preferred_element_type=F32)
    g = _gelu(act)
    for r in range(te // N_KEYS):
        e1 = j * (te // N_KEYS) + r
        wt = jnp.zeros((N_KEYS, tt), F32)
        for h in range(P_HEADS):
            a1 = a1_ref[h, pl.ds(e1, 1), :]
            w1 = w1_ref[h, pl.ds(e1, 1), :]
            wt = wt + jnp.where((w1 & b2_ref[h]) != 0, a1 * a2_ref[h], 0.0)
        ga_sc[r * N_KEYS:(r + 1) * N_KEYS, :] = (wt * g[r * N_KEYS:(r + 1) * N_KEYS, :]).astype(BF16)
    acc_sc[...] += jnp.dot(vt_ref[...], ga_sc[...], preferred_element_type=F32)

    @pl.when(j == pl.num_programs(1) - 1)
    def _():
        o_ref[...] = x_ref[...] + gt_ref[...] * acc_sc[...].T


def _peer_dense(h2, u, v_t, tabs, x, gt, rows_per_group, tt=256, te=512):
    M, D = h2.shape
    E = u.shape[0]
    tt = min(tt, M)
    tab = pl.BlockSpec((P_HEADS, N_KEYS, tt), lambda i, j: (0, 0, i))
    r = gt.shape[1]
    return pl.pallas_call(
        functools.partial(_peer_dense_body, te=te),
        grid=(M // tt, E // te),
        in_specs=[pl.BlockSpec((tt, D), lambda i, j: (i, 0)),
                  pl.BlockSpec((te, D), lambda i, j: (j, 0)),
                  pl.BlockSpec((D, te), lambda i, j: (0, j)),
                  tab, tab, tab, tab,
                  pl.BlockSpec((tt, D), lambda i, j: (i, 0)),
                  pl.BlockSpec((None, r, D), lambda i, j: ((i * tt) // rows_per_group, 0, 0))],
        out_specs=pl.BlockSpec((tt, D), lambda i, j: (i, 0)),
        out_shape=jax.ShapeDtypeStruct((M, D), F32),
        scratch_shapes=[pltpu.VMEM((D, tt), F32), pltpu.VMEM((te, tt), BF16)],
        compiler_params=_params("parallel", "arbitrary"),
    )(h2, u, v_t, *tabs, x, gt)


def _peer(h2, x, gt, rows_per_group, wq, keys, u, v_t):
    (q,) = _mm(h2, wq, [BF16], tm=512, tn=1024)
    tabs = _peer_select(q, keys)
    return _peer_dense(h2, u, v_t, tabs, x, gt, rows_per_group)


def _s5_prepare(lam_re, lam_im, log_dt, b_re, b_im, c_re, c_im):
    G, N, P = b_re.shape
    S, W = G * N, G * P
    ab_re, ab_im, bb_re, bb_im = _s5_discretise(lam_re, lam_im, log_dt, b_re, b_im)
    eye = jnp.eye(G, dtype=F32)
    blk = lambda m, rows, cols: (jnp.swapaxes(m, 1, 2)[:, :, None, :] * eye[:, None, :, None]).reshape(rows, cols)
    b_blk = jnp.concatenate([blk(bb_re, W, S), blk(bb_im, W, S)], axis=1).astype(BF16)
    c_blk = jnp.concatenate([blk(c_re.astype(F32), S, W), -blk(c_im.astype(F32), S, W)], axis=0).astype(BF16)
    a_row = jnp.concatenate([ab_re.reshape(1, S), ab_im.reshape(1, S)], axis=1)
    return a_row, b_blk, c_blk


def _s5(u, T, x0_re, x0_im, prep, d_skip):
    M, W = u.shape
    B = M // T
    G, N = x0_re.shape[1:]
    S = G * N
    a_row, b_blk, c_blk = prep
    (bu,) = _mm(u, b_blk, [F32], tm=512, tn=1024)
    x0 = jnp.concatenate([x0_re.reshape(B, 1, S), x0_im.reshape(B, 1, S)], axis=2).astype(F32)
    xs, x_last = _s5_scan(bu.reshape(B, T, 2 * S), a_row, x0)
    tm, tn = min(256, M), 512
    (yg,) = _mm(xs.reshape(M, 2 * S), c_blk, [BF16], tm=tm, tn=tn,
                epilogue=lambda acc, u_, d_: (_gelu(acc + d_ * u_),),
                extras=[(u, (tm, tn), lambda i, j: (i, j)),
                        (d_skip.reshape(1, W).astype(F32), (1, tn), lambda i, j: (0, j))])
    return yg, x_last[:, 0, :S].reshape(B, G, N), x_last[:, 0, S:].reshape(B, G, N)


def _logsig_body(f_ref, b_ref, o_ref):
    o_ref[...] = _log_sigmoid(f_ref[...] + b_ref[...])


def _logsig(f, b_row):
    return pl.pallas_call(_logsig_body, out_shape=jax.ShapeDtypeStruct(f.shape, F32))(f, b_row)


def _head_lane_select(n_lanes):
    lane = lax.broadcasted_iota(I32, (N_HEADS, n_lanes), 1)
    head = lax.broadcasted_iota(I32, (N_HEADS, n_lanes), 0)
    return (lane & (N_HEADS - 1)) == head, lane


def _fox_sample_body(pt_ref, q_ref, kn_ref, vn_ref, lfn_ref, kp_ref, vp_ref, lfp_ref, o_ref,
                     m_sc, l_sc, acc_sc, carry_sc, *, scale):
    del pt_ref
    p = pl.program_id(1)
    n_lanes = PAGE * N_HEADS
    nt = (((1,), (1,)), ((), ()))

    @pl.when(p == 0)
    def _():
        m_sc[...] = jnp.full(m_sc.shape, -jnp.inf, F32)
        l_sc[...] = jnp.zeros(l_sc.shape, F32)
        acc_sc[...] = jnp.zeros(acc_sc.shape, F32)
        carry_sc[...] = jnp.zeros(carry_sc.shape, F32)

    hsel, lane8 = _head_lane_select(n_lanes)
    q = q_ref[0]
    kp = kp_ref[...].reshape(n_lanes, HEAD_DIM).astype(BF16)
    s = lax.dot_general(q, kp, nt, preferred_element_type=F32) * scale
    lf = lfp_ref[...]
    lane = lax.broadcasted_iota(I32, (1, n_lanes), 1)
    pre, tot = lf, lf
    sh = N_HEADS
    while sh < n_lanes:
        pre = pre + jnp.where(lane >= sh, pltpu.roll(pre, sh, axis=1), 0.0)
        tot = tot + pltpu.roll(tot, sh, axis=1)
        sh *= 2
    f_k = carry_sc[...] + pre
    carry_sc[...] = carry_sc[...] + tot
    s = jnp.where(hsel, s - f_k, NEG)
    m_prev = m_sc[...]
    m_new = jnp.maximum(m_prev, jnp.max(s, axis=1, keepdims=True))
    alpha = jnp.exp(m_prev - m_new)
    pr = jnp.exp(s - m_new)
    l_sc[...] = alpha * l_sc[...] + jnp.sum(pr, axis=1, keepdims=True)
    vp = vp_ref[...].reshape(n_lanes, HEAD_DIM).astype(BF16)
    acc_sc[...] = alpha * acc_sc[...] + jnp.dot(pr.astype(BF16), vp, preferred_element_type=F32)
    m_sc[...] = m_new

    @pl.when(p == pl.num_programs(1) - 1)
    def _():
        f_col = jnp.sum(jnp.where(hsel, jnp.where(lane8 < N_HEADS, carry_sc[...], 0.0), 0.0),
                        axis=1, keepdims=True)
        kn = kn_ref[0].astype(BF16).astype(F32)
        vn = vn_ref[0].astype(BF16).astype(F32)
        s_n = jnp.sum(q.astype(F32) * kn, axis=1, keepdims=True) * scale - (f_col + lfn_ref[0])
        m_prev = m_sc[...]
        m_new = jnp.maximum(m_prev, s_n)
        alpha = jnp.exp(m_prev - m_new)
        p_n = jnp.exp(s_n - m_new)
        l = alpha * l_sc[...] + p_n
        acc = alpha * acc_sc[...] + p_n.astype(BF16).astype(F32) * vn
        o_ref[0] = acc / l


def _fox_sample(layer, pt_flat, q, k_new, v_new, logf_new, cache_k, cache_v, cache_logf_rows):
    Bd = q.shape[0]
    n_pages = pt_flat.shape[0] // Bd
    tok = lambda w: pl.BlockSpec((1, N_HEADS, w), lambda b, p, pt: (b, 0, 0))
    page = lambda b, p, pt: (layer, pt[b * n_pages + p], 0, 0, 0)
    grid_spec = pltpu.PrefetchScalarGridSpec(
        num_scalar_prefetch=1, grid=(Bd, n_pages),
        in_specs=[tok(HEAD_DIM), tok(HEAD_DIM), tok(HEAD_DIM), tok(1),
                  pl.BlockSpec((None, None, PAGE, N_HEADS, HEAD_DIM), page),
                  pl.BlockSpec((None, None, PAGE, N_HEADS, HEAD_DIM), page),
                  pl.BlockSpec((None, None, 1, PAGE * N_HEADS),
                               lambda b, p, pt: (layer, pt[b * n_pages + p], 0, 0))],
        out_specs=pl.BlockSpec((1, N_HEADS, HEAD_DIM), lambda b, p, pt: (b, 0, 0)),
        scratch_shapes=[pltpu.VMEM((N_HEADS, 1), F32), pltpu.VMEM((N_HEADS, 1), F32),
                        pltpu.VMEM((N_HEADS, HEAD_DIM), F32), pltpu.VMEM((1, PAGE * N_HEADS), F32)])
    return pl.pallas_call(
        functools.partial(_fox_sample_body, scale=HEAD_DIM ** -0.5),
        grid_spec=grid_spec,
        out_shape=jax.ShapeDtypeStruct((Bd, N_HEADS, HEAD_DIM), F32),
        compiler_params=_params("parallel", "arbitrary"),
    )(pt_flat, q, k_new, v_new, logf_new, cache_k, cache_v, cache_logf_rows)


def _idx_scores_body(*refs):
    qi_ref, w_ref, ki_ref, o_ref = refs[-4:]
    d = lax.dot_general(qi_ref[0], ki_ref[...].astype(BF16), (((1,), (1,)), ((), ())),
                        preferred_element_type=F32)
    w = w_ref[0] * (H_IDX ** -0.5 * D_IDX ** -0.5)
    o_ref[0] = jnp.sum(w * jnp.maximum(d, 0.0), axis=0, keepdims=True)


def _idx_scores_paged(layer, pt_flat, qi, w, cache_ki):
    Bd = qi.shape[0]
    n_pages = pt_flat.shape[0] // Bd
    grid_spec = pltpu.PrefetchScalarGridSpec(
        num_scalar_prefetch=1, grid=(Bd, n_pages),
        in_specs=[pl.BlockSpec((1, H_IDX, D_IDX), lambda b, p, pt: (b, 0, 0)),
                  pl.BlockSpec((1, H_IDX, 1), lambda b, p, pt: (b, 0, 0)),
                  pl.BlockSpec((None, None, PAGE, D_IDX), lambda b, p, pt: (layer, pt[b * n_pages + p], 0, 0))],
        out_specs=pl.BlockSpec((1, 1, PAGE), lambda b, p, pt: (b, 0, p)))
    return pl.pallas_call(
        _idx_scores_body, grid_spec=grid_spec,
        out_shape=jax.ShapeDtypeStruct((Bd, 1, n_pages * PAGE), F32),
        compiler_params=_params("parallel", "arbitrary"),
    )(pt_flat, qi, w, cache_ki)


def _idx_scores_new(qi, w, ki_rows):
    Bd, R, _ = ki_rows.shape
    return pl.pallas_call(
        _idx_scores_body, grid=(Bd,),
        in_specs=[pl.BlockSpec((1, H_IDX, D_IDX), lambda b: (b, 0, 0)),
                  pl.BlockSpec((1, H_IDX, 1), lambda b: (b, 0, 0)),
                  pl.BlockSpec((None, R, D_IDX), lambda b: (b, 0, 0))],
        out_specs=pl.BlockSpec((1, 1, R), lambda b: (b, 0, 0)),
        out_shape=jax.ShapeDtypeStruct((Bd, 1, R), F32),
        compiler_params=_params("parallel"),
    )(qi, w, ki_rows)


def _dsa_sample_select_body(sc_ref, scn_ref, rel_ref, idx_ref, sb_ref, nb_ref, rs_sc, *, topk, chunk):
    Bd, P = sc_ref.shape
    key_p = _sortable_key(sc_ref[...])
    key_n = _sortable_key(scn_ref[:, 0:1])
    col = lax.broadcasted_iota(I32, (Bd, P), 1)

    one = lambda pred: jnp.where(pred, 1, 0)

    def count(ind_p, ind_n):
        return jnp.sum(ind_p, axis=1, keepdims=True) + ind_n

    thr = _kth_largest_key(lambda t: count(one(key_p >= t), one(key_n >= t)), (Bd, 1), topk)
    n_gt = count(one(key_p > thr), one(key_n > thr))
    n_ge = count(one(key_p >= thr), one(key_n >= thr))
    need = topk - n_gt
    nbits = (P + 1).bit_length()

    def body(b, cut):
        cand = cut | lax.shift_left(jnp.int32(1), nbits - 1 - b)
        cnt = count(jnp.where(key_p == thr, one(col < cand), 0), jnp.where(key_n == thr, one(P < cand), 0))
        return jnp.where(cnt < need, cand, cut)
    cut = lax.fori_loop(0, nbits, body, jnp.zeros((Bd, 1), I32))
    cut = jnp.where(n_ge > topk, cut, jnp.int32(2 ** 30))
    sel_p = jnp.where(key_p > thr, 1, jnp.where(key_p == thr, one(col <= cut), 0))
    sel_n = jnp.where(key_n > thr, 1, jnp.where(key_n == thr, one(P <= cut), 0))
    rank = sel_p
    sh = 1
    while sh < P:
        rank = rank + jnp.where(col >= sh, pltpu.roll(rank, sh, axis=1), 0)
        sh *= 2
    rs_sc[...] = sel_p * rank
    n_past = jnp.sum(sel_p, axis=1, keepdims=True)
    nb_ref[...] = jnp.where(sel_n > 0, rel_ref[0:1, :], NEG)

    slot = lax.broadcasted_iota(I32, (topk, 1), 0)
    ccol = lax.broadcasted_iota(I32, (topk, chunk), 1)
    for b in range(Bd):
        idx = jnp.zeros((topk, 1), I32)
        for c in range(P // chunk):
            rs = rs_sc[b:b + 1, c * chunk:(c + 1) * chunk]
            idx = idx + jnp.sum(jnp.where(rs == slot + 1, ccol + c * chunk, 0), axis=1, keepdims=True)
        idx_ref[b] = idx
        bucket = _t5_bucket(P - idx)
        bias = jnp.zeros((topk, N_HEADS), F32)
        for k in range(N_BUCKETS):
            bias = bias + jnp.where(bucket == k, rel_ref[k:k + 1, :], 0.0)
        sb_ref[b] = jnp.where(slot < n_past[b:b + 1, :], bias, NEG)


def _dsa_sample_select(sc, sc_new, rel_table, topk, chunk=2048):
    Bd, P = sc.shape
    return pl.pallas_call(
        functools.partial(_dsa_sample_select_body, topk=topk, chunk=min(chunk, P)),
        out_shape=[jax.ShapeDtypeStruct((Bd, topk, 1), I32), jax.ShapeDtypeStruct((Bd, topk, N_HEADS), F32),
                   jax.ShapeDtypeStruct((Bd, N_HEADS), F32)],
        scratch_shapes=[pltpu.VMEM((Bd, P), I32)],
        compiler_params=pltpu.CompilerParams(vmem_limit_bytes=V7X_VMEM_LIMIT_BYTES),
    )(sc, sc_new, rel_table)


def _dsa_sample_attend_body(idx_ref, pt_ref, q_ref, kn_ref, vn_ref, sb_ref, nb_ref, kc_hbm, vc_hbm, o_ref,
                            kbuf, vbuf, sems, *, layer, topk, n_pages, scale):
    b = pl.program_id(0)

    def row_copies(j):
        i = idx_ref[b * topk + j]
        pg = pt_ref[b * n_pages + lax.shift_right_logical(i, 7)]
        off = i & (PAGE - 1)
        return (pltpu.make_async_copy(kc_hbm.at[layer, pg, off], kbuf.at[j], sems.at[0]),
                pltpu.make_async_copy(vc_hbm.at[layer, pg, off], vbuf.at[j], sems.at[1]))

    def start(j, c):
        ck, cv = row_copies(j)
        ck.start()
        cv.start()
        return c

    def wait(j, c):
        ck, cv = row_copies(j)
        ck.wait()
        cv.wait()
        return c

    lax.fori_loop(0, topk, start, 0)
    lax.fori_loop(0, topk, wait, 0)

    n_lanes = topk * N_HEADS
    hsel, _ = _head_lane_select(n_lanes)
    q = q_ref[0]
    kb = kbuf[...].reshape(n_lanes, HEAD_DIM).astype(BF16)
    s = lax.dot_general(q, kb, (((1,), (1,)), ((), ())), preferred_element_type=F32) * scale
    s = jnp.where(hsel, s + sb_ref[0], NEG)
    kn = kn_ref[0].astype(BF16).astype(F32)
    vn = vn_ref[0].astype(BF16).astype(F32)
    s_n = jnp.sum(q.astype(F32) * kn, axis=1, keepdims=True) * scale + nb_ref[0]
    m = jnp.maximum(jnp.max(s, axis=1, keepdims=True), s_n)
    p = jnp.exp(s - m)
    p_n = jnp.exp(s_n - m)
    l = jnp.sum(p, axis=1, keepdims=True) + p_n
    vb = vbuf[...].reshape(n_lanes, HEAD_DIM).astype(BF16)
    acc = jnp.dot(p.astype(BF16), vb, preferred_element_type=F32) + p_n.astype(BF16).astype(F32) * vn
    o_ref[0] = acc / l


def _dsa_sample_attend(layer, idx_flat, pt_flat, q, k_new, v_new, slot_bias, new_bias, cache_k, cache_v, topk):
    Bd = q.shape[0]
    n_pages = pt_flat.shape[0] // Bd
    tok = lambda w: pl.BlockSpec((1, N_HEADS, w), lambda b, idx, pt: (b, 0, 0))
    grid_spec = pltpu.PrefetchScalarGridSpec(
        num_scalar_prefetch=2, grid=(Bd,),
        in_specs=[tok(HEAD_DIM), tok(HEAD_DIM), tok(HEAD_DIM),
                  pl.BlockSpec((1, 1, topk * N_HEADS), lambda b, idx, pt: (b, 0, 0)),
                  tok(1),
                  pl.BlockSpec(memory_space=pl.ANY), pl.BlockSpec(memory_space=pl.ANY)],
        out_specs=pl.BlockSpec((1, N_HEADS, HEAD_DIM), lambda b, idx, pt: (b, 0, 0)),
        scratch_shapes=[pltpu.VMEM((topk, N_HEADS, HEAD_DIM), F32), pltpu.VMEM((topk, N_HEADS, HEAD_DIM), F32),
                        pltpu.SemaphoreType.DMA((2,))])
    return pl.pallas_call(
        functools.partial(_dsa_sample_attend_body, layer=layer, topk=topk, n_pages=n_pages,
                          scale=HEAD_DIM ** -0.5),
        grid_spec=grid_spec,
        out_shape=jax.ShapeDtypeStruct((Bd, N_HEADS, HEAD_DIM), F32),
        compiler_params=_params("arbitrary"),
    )(idx_flat, pt_flat, q, k_new, v_new, slot_bias, new_bias, cache_k, cache_v)


W_ATT = N_HEADS * HEAD_DIM
_PROJ_NAMES = ("fq", "fk", "fv", "ff", "bq", "bk", "bv", "iq", "ik", "iw", "su", "gl")
_FF_LANES = slice(0, N_HEADS)
_IW_LANES = slice(N_HEADS, N_HEADS + H_IDX)
_IK_LANES = slice(N_HEADS + H_IDX, N_HEADS + H_IDX + D_IDX)


def _layer_weights(l, p):
    D = p["w_in"].shape[1]
    w_c = p["w_glu"].shape[1]
    sizes = (W_ATT, W_ATT, W_ATT, N_HEADS, W_ATT, W_ATT, W_ATT, H_IDX * D_IDX, D_IDX, H_IDX, w_c, 3 * D)
    w_in = p["w_in"][l]
    cols, off = {}, 0
    for name, n in zip(_PROJ_NAMES, sizes):
        cols[name] = w_in[:, off:off + n]
        off += n
    pad = jnp.zeros((D, LANES - (N_HEADS + H_IDX + D_IDX)), w_in.dtype)
    lw = {n: cols[n].astype(BF16) for n in ("fq", "fk", "fv", "bq", "bk", "bv", "iq", "su", "gl")}
    lw["small"] = jnp.concatenate([cols["ff"], cols["iw"], cols["ik"], pad], axis=1).astype(BF16)
    w_glu, w_br = p["w_glu"][l], p["w_br"][l]
    lw["glu_a"], lw["glu_b"] = w_glu[:, :w_c].astype(BF16), w_glu[:, w_c:].astype(BF16)
    lw["wa"] = w_br[:W_ATT].astype(BF16)
    lw["wb"] = w_br[W_ATT:2 * W_ATT].astype(BF16)
    lw["wc"] = w_br[2 * W_ATT:].astype(BF16)
    lw["w_out"] = p["w_out"][l].astype(BF16)
    lw["wq"] = p["peer_wq"][l].astype(BF16)
    lw["keys"] = p["peer_keys"][l].reshape(2 * P_HEADS, N_KEYS, -1).astype(BF16)
    lw["u"] = p["peer_u"][l].astype(BF16)
    lw["v_t"] = p["peer_v"][l].T.astype(BF16)
    lw["s5"] = _s5_prepare(p["s5_lam_re"][l], p["s5_lam_im"][l], p["s5_log_dt"][l], p["s5_b_re"][l],
                           p["s5_b_im"][l], p["s5_c_re"][l], p["s5_c_im"][l])
    lw["s5_d"] = p["s5_d"][l]
    lw["norm1_g"], lw["norm2_g"] = p["norm1_g"][l], p["norm2_g"][l]
    return lw


def _layer(x, mods, rows_per_group, T, lw, s5_re0, s5_im0, attend):
    M, D = x.shape
    sh1, sc1, gt1, sh2, sc2, gt2 = mods
    tm = min(512, M)
    h = _norm_mod(x, lw["norm1_g"], sc1, sh1, rows_per_group)
    proj = lambda name, dts: _mm(h, lw[name], dts, tm=tm, tn=1024, order="nm")
    (fq,) = proj("fq", [BF16])
    fk, fk16 = proj("fk", [F32, BF16])
    fv, fv16 = proj("fv", [F32, BF16])
    (bq,) = proj("bq", [BF16])
    bk, bk16 = proj("bk", [F32, BF16])
    bv, bv16 = proj("bv", [F32, BF16])
    (iq,) = proj("iq", [BF16])
    (su,) = proj("su", [F32])
    (small,) = proj("small", [F32])
    (gl,) = proj("gl", [F32])
    o_a, o_b, logf = attend(fq, fk, fk16, fv, fv16, bq, bk, bk16, bv, bv16, iq, small)
    yg, s5_re, s5_im = _s5(su, T, s5_re0, s5_im0, lw["s5"], lw["s5_d"])
    o_c = _glu(yg, lw["glu_a"], lw["glu_b"])
    merged = _merge(o_a, o_b, o_c, lw["wa"], lw["wb"], lw["wc"], gl)
    tn = 512
    (x1,) = _mm(merged, lw["w_out"], [F32], tm=tm, tn=tn,
                epilogue=lambda acc, x_, g_: (x_ + g_ * acc,),
                extras=[(x, (tm, tn), lambda i, j: (i, j)), _mod_extra(gt1, rows_per_group, tm, tn)])
    h2 = _norm_mod(x1, lw["norm2_g"], sc2, sh2, rows_per_group)
    if M < LANES:
        padr = lambda a: jnp.pad(a, ((0, LANES - M), (0, 0)))
        gt2p = jnp.pad(gt2, ((0, 0), (0, LANES - M), (0, 0)))
        x2 = _peer(padr(h2), padr(x1), gt2p, LANES, lw["wq"], lw["keys"], lw["u"], lw["v_t"])[:M]
    else:
        x2 = _peer(h2, x1, gt2, rows_per_group, lw["wq"], lw["keys"], lw["u"], lw["v_t"])
    return x2, (fk, fv, logf, bk, bv, small[:, _IK_LANES], s5_re, s5_im)


def kernel(x_prompt, x_sample, cache_fox_k, cache_fox_v, cache_fox_logf, cache_dsa_k, cache_dsa_v,
           cache_dsa_idx_k, state_s5_re, state_s5_im, page_table, c_prompt, c_sample,
           w_ada, b_ada, norm1_g, norm2_g, w_in, b_f, rel_table, s5_lam_re, s5_lam_im, s5_log_dt,
           s5_b_re, s5_b_im, s5_c_re, s5_c_im, s5_d, w_glu, w_br, w_out,
           peer_wq, peer_keys, peer_u, peer_v, final_norm_g):
    p = dict(w_in=w_in, w_glu=w_glu, w_br=w_br, w_out=w_out, peer_wq=peer_wq, peer_keys=peer_keys,
             peer_u=peer_u, peer_v=peer_v, s5_lam_re=s5_lam_re, s5_lam_im=s5_lam_im, s5_log_dt=s5_log_dt,
             s5_b_re=s5_b_re, s5_b_im=s5_b_im, s5_c_re=s5_c_re, s5_c_im=s5_c_im, s5_d=s5_d,
             norm1_g=norm1_g, norm2_g=norm2_g)
    depth = w_in.shape[0]
    B, T, D = x_prompt.shape
    Bd = x_sample.shape[0]
    assert x_sample.shape[1] == 1
    n_pool = cache_fox_k.shape[1]
    n_pages = page_table.shape[1]
    past = n_pages * PAGE
    tile = min(256, T)
    topk_p = min(TOPK_MAX, T // 4)
    topk_s = min(TOPK_MAX, (past + 1) // 4)
    pt_flat = page_table.reshape(-1).astype(I32)
    logf_rows = cache_fox_logf.reshape(depth, n_pool, 1, PAGE * N_HEADS)
    rel = rel_table.astype(F32)
    tz = _t5_tiles(rel, tile)

    n_c = B + Bd
    c_all = jnp.pad(jnp.concatenate([c_prompt, c_sample], axis=0), ((0, (-n_c) % 8), (0, 0)))
    xp = x_prompt.reshape(B * T, D)
    xs = x_sample.reshape(Bd, D)
    rows_p, rows_s = [], []
    for l in range(depth):
        lw = _layer_weights(l, p)
        m = _adaln(c_all, w_ada[l], b_ada[l])
        mods_p = [a[:B, None, :] for a in jnp.split(m, 6, axis=1)]
        mods_s = [a[None, B:n_c, :] for a in jnp.split(m, 6, axis=1)]
        b_f_l = b_f[l].astype(F32)

        def attend_prompt(fq, fk, fk16, fv, fv16, bq, bk, bk16, bv, bv16, iq, small):
            r3 = lambda a: a.reshape(B, T, a.shape[-1])
            small3 = r3(small)
            f_t = jnp.moveaxis(small3[:, :, _FF_LANES], -1, 1).reshape(B * N_HEADS, T)
            logf_t, cum = _gate_cumsum(f_t, jnp.tile(b_f_l, B).reshape(B * N_HEADS, 1))
            logf = jnp.moveaxis(logf_t.reshape(B, N_HEADS, T), 1, -1)
            o_a = _flash("fox", r3(fq), r3(fk16), r3(fv16), cum.reshape(B, N_HEADS, T), tile=tile)
            ki_t = jnp.swapaxes(small3[:, :, _IK_LANES], 1, 2).astype(BF16)
            mask = _dsa_select(r3(iq), small3, ki_t, topk_p)
            o_b = _flash("dsa", r3(bq), r3(bk16), r3(bv16), mask, tz, tile=tile)
            return o_a.reshape(B * T, W_ATT), o_b.reshape(B * T, W_ATT), logf

        def attend_sample(fq, fk, fk16, fv, fv16, bq, bk, bk16, bv, bv16, iq, small, l=l):
            h3 = lambda a: a.reshape(Bd, N_HEADS, HEAD_DIM)
            logf = _logsig(small[:, _FF_LANES], b_f_l.reshape(1, N_HEADS))
            o_a = _fox_sample(l, pt_flat, h3(fq), h3(fk), h3(fv), logf.reshape(Bd, N_HEADS, 1),
                              cache_fox_k, cache_fox_v, logf_rows)
            qi3 = iq.reshape(Bd, H_IDX, D_IDX)
            w3 = small[:, _IW_LANES].reshape(Bd, H_IDX, 1)
            sc = _idx_scores_paged(l, pt_flat, qi3, w3, cache_dsa_idx_k)
            ki_new = jnp.pad(small[:, _IK_LANES].reshape(Bd, 1, D_IDX), ((0, 0), (0, LANES - 1), (0, 0)))
            sc_new = _idx_scores_new(qi3, w3, ki_new)
            idx, sb, nb = _dsa_sample_select(sc.reshape(Bd, past), sc_new.reshape(Bd, LANES), rel, topk_s)
            o_b = _dsa_sample_attend(l, idx.reshape(-1), pt_flat, h3(bq), h3(bk), h3(bv),
                                     sb.reshape(Bd, 1, topk_s * N_HEADS), nb.reshape(Bd, N_HEADS, 1),
                                     cache_dsa_k, cache_dsa_v, topk_s)
            return (o_a.reshape(Bd, W_ATT).astype(BF16), o_b.reshape(Bd, W_ATT).astype(BF16), logf)

        zero_state = jnp.zeros((B,) + state_s5_re.shape[2:], F32)
        xp, rp = _layer(xp, mods_p, T, T, lw, zero_state, zero_state, attend_prompt)
        xs, rs = _layer(xs, mods_s, Bd, 1, lw, state_s5_re[l], state_s5_im[l], attend_sample)
        rows_p.append(rp)
        rows_s.append(rs)

    y_prompt = _norm(xp, final_norm_g).reshape(B, T, D)
    y_sample = _norm(xs, final_norm_g).reshape(Bd, 1, D)

    def leaves(rows, nb, nt):
        fk, fv, fl, bk, bv, ik, sr, si = [jnp.stack(a) for a in zip(*rows)]
        hd = (depth, nb, nt, N_HEADS, HEAD_DIM)
        return (fk.reshape(hd), fv.reshape(hd), fl.reshape(depth, nb, nt, N_HEADS), bk.reshape(hd),
                bv.reshape(hd), ik.reshape(depth, nb, nt, D_IDX), sr, si)

    return (y_prompt, y_sample) + leaves(rows_p, B, T) + leaves(rows_s, Bd, 1)
```

```python
import functools
import math

import jax
import jax.numpy as jnp
from jax import lax
from jax.experimental import pallas as pl
from jax.experimental.pallas import tpu as pltpu

F32 = jnp.float32
BF16 = jnp.bfloat16
I32 = jnp.int32

N_HEADS = 8
HEAD_DIM = 128
H_IDX = 16
D_IDX = 64
TOPK_MAX = 256
GROUP = 16
N_STATE = 64
N_BUCKETS = 32
MAX_DIST = 128
N_KEYS = 128
P_HEADS = 8
P_TOPK = 16
PAGE = 128
EPS = 1e-6

V7X_VMEM_LIMIT_BYTES = 56 * 1024 * 1024
LANES = 128
NEG = -1e30
INT_MIN = -2 ** 31

_PAIRS = sorted([(i, j) for i in range(P_TOPK) for j in range(P_TOPK) if (i + 1) * (j + 1) <= P_TOPK],
                key=lambda p: p[0] * P_TOPK + p[1])
_N_CAND = 56


def _params(*sem):
    return pltpu.CompilerParams(dimension_semantics=sem, vmem_limit_bytes=V7X_VMEM_LIMIT_BYTES)


def _gelu(x):
    return 0.5 * x * (1.0 + jnp.tanh(math.sqrt(2.0 / math.pi) * (x + 0.044715 * (x * x * x))))


def _sigmoid(x):
    return 1.0 / (1.0 + jnp.exp(-x))


def _mm_body(*refs, n_extra, epilogue):
    a_ref, b_ref = refs[0], refs[1]
    extra = refs[2:2 + n_extra]
    outs = refs[2 + n_extra:]
    acc = jnp.dot(a_ref[...].astype(BF16), b_ref[...].astype(BF16), preferred_element_type=F32)
    vals = epilogue(acc, *[e[...] for e in extra]) if epilogue is not None else (acc,) * len(outs)
    for o, v in zip(outs, vals):
        o[...] = v.astype(o.dtype)


def _mm(a, b, out_dtypes, *, tm, tn, order="mn", epilogue=None, extras=(), name="mm"):
    M, K = a.shape
    N = b.shape[1]
    tm, tn = min(tm, M), min(tn, N)
    assert M % tm == 0 and N % tn == 0, (M, N, tm, tn)
    if order == "mn":
        grid = (M // tm, N // tn)
        ij = lambda g0, g1: (g0, g1)
    else:
        grid = (N // tn, M // tm)
        ij = lambda g0, g1: (g1, g0)
    in_specs = [pl.BlockSpec((tm, K), lambda g0, g1: (ij(g0, g1)[0], 0)),
                pl.BlockSpec((K, tn), lambda g0, g1: (0, ij(g0, g1)[1]))]
    args = [a, b]
    for arr, bshape, imap in extras:
        in_specs.append(pl.BlockSpec(bshape, lambda g0, g1, imap=imap: imap(*ij(g0, g1))))
        args.append(arr)
    out_specs = [pl.BlockSpec((tm, tn), lambda g0, g1: ij(g0, g1)) for _ in out_dtypes]
    out_shape = [jax.ShapeDtypeStruct((M, N), dt) for dt in out_dtypes]
    return pl.pallas_call(
        functools.partial(_mm_body, n_extra=len(extras), epilogue=epilogue),
        name=name, grid=grid, in_specs=in_specs, out_specs=out_specs, out_shape=out_shape,
        compiler_params=_params("parallel", "parallel"),
    )(*args)


def _mod_extra(mod, rows_per_group, tm, tn):
    r = mod.shape[1]
    return (mod, (None, r, tn), lambda i, j: ((i * tm) // rows_per_group, 0, j))


def _adaln_body(c_ref, w_ref, b_ref, o_ref):
    c = c_ref[...]
    s = c * _sigmoid(c)
    o_ref[...] = jnp.dot(s.astype(BF16), w_ref[...].astype(BF16), preferred_element_type=F32) + b_ref[...]


def _adaln(c, w_ada, b_ada, tn=1024):
    R, D = c.shape
    N = w_ada.shape[1]
    return pl.pallas_call(
        _adaln_body, name="adaln", grid=(N // tn,),
        in_specs=[pl.BlockSpec((R, D), lambda j: (0, 0)),
                  pl.BlockSpec((D, tn), lambda j: (0, j)),
                  pl.BlockSpec((1, tn), lambda j: (0, j))],
        out_specs=pl.BlockSpec((R, tn), lambda j: (0, j)),
        out_shape=jax.ShapeDtypeStruct((R, N), F32),
        compiler_params=_params("parallel"),
    )(c, w_ada, b_ada.reshape(1, N))


def _norm_mod_body(x_ref, g_ref, sc_ref, sh_ref, o_ref):
    x = x_ref[...]
    y = x * lax.rsqrt(jnp.mean(x * x, axis=-1, keepdims=True) + EPS) * g_ref[...]
    o_ref[...] = (y * (1.0 + sc_ref[...]) + sh_ref[...]).astype(o_ref.dtype)


def _norm_mod(x, g, sc, sh, rows_per_group, tm=512):
    M, D = x.shape
    tm = min(tm, M)
    r = sc.shape[1]
    mod_spec = pl.BlockSpec((None, r, D), lambda i: ((i * tm) // rows_per_group, 0, 0))
    return pl.pallas_call(
        _norm_mod_body, name="norm_mod", grid=(M // tm,),
        in_specs=[pl.BlockSpec((tm, D), lambda i: (i, 0)), pl.BlockSpec((1, D), lambda i: (0, 0)),
                  mod_spec, mod_spec],
        out_specs=pl.BlockSpec((tm, D), lambda i: (i, 0)),
        out_shape=jax.ShapeDtypeStruct((M, D), BF16),
        compiler_params=_params("parallel"),
    )(x, g.reshape(1, D), sc, sh)


def _norm_body(x_ref, g_ref, o_ref):
    x = x_ref[...]
    o_ref[...] = x * lax.rsqrt(jnp.mean(x * x, axis=-1, keepdims=True) + EPS) * g_ref[...]


def _norm(x, g, tm=512):
    M, D = x.shape
    tm = min(tm, M)
    return pl.pallas_call(
        _norm_body, name="final_norm", grid=(M // tm,),
        in_specs=[pl.BlockSpec((tm, D), lambda i: (i, 0)), pl.BlockSpec((1, D), lambda i: (0, 0))],
        out_specs=pl.BlockSpec((tm, D), lambda i: (i, 0)),
        out_shape=jax.ShapeDtypeStruct((M, D), F32),
        compiler_params=_params("parallel"),
    )(x, g.reshape(1, D))


def _log_sigmoid(z):
    return jnp.minimum(z, 0.0) - jnp.log(1.0 + jnp.exp(-jnp.abs(z)))


def _gate_body(f_ref, b_ref, logf_ref, cum_ref):
    logf = _log_sigmoid(f_ref[...] + b_ref[...])
    logf_ref[...] = logf
    T = logf.shape[1]
    lane = lax.broadcasted_iota(I32, logf.shape, 1)
    x = logf
    sh = 1
    while sh < T:
        x = x + jnp.where(lane >= sh, pltpu.roll(x, sh, axis=1), 0.0)
        sh *= 2
    cum_ref[...] = x


def _gate_cumsum(f_t, b_col):
    R, T = f_t.shape
    return pl.pallas_call(
        _gate_body, name="fox_gate",
        out_shape=[jax.ShapeDtypeStruct((R, T), F32), jax.ShapeDtypeStruct((R, T), F32)],
        compiler_params=pltpu.CompilerParams(vmem_limit_bytes=V7X_VMEM_LIMIT_BYTES),
    )(f_t, b_col)


def _flash_body(*refs, mode, tile, scale):
    if mode == "fox":
        q_ref, k_ref, v_ref, fk_ref, o_ref, m_sc, l_sc, acc_sc = refs
    else:
        q_ref, k_ref, v_ref, mask_ref, tz_ref, o_ref, m_sc, l_sc, acc_sc = refs
    qi = pl.program_id(1)
    ki = pl.program_id(2)

    @pl.when(ki == 0)
    def _():
        m_sc[...] = jnp.full(m_sc.shape, -jnp.inf, F32)
        l_sc[...] = jnp.zeros(l_sc.shape, F32)
        acc_sc[...] = jnp.zeros(acc_sc.shape, F32)

    nsub = tile // LANES

    def t5_bias(h, near):
        zero = jnp.zeros((LANES, LANES), F32)
        if near == "diag":
            pick = lambda a, b: tz_ref[h, 0] if a == b else (tz_ref[h, 1] if a == b + 1 else zero)
        else:
            pick = lambda a, b: tz_ref[h, 1] if (a == 0 and b == nsub - 1) else zero
        rows = [jnp.concatenate([pick(a, b) for b in range(nsub)], axis=1) if nsub > 1 else pick(a, 0)
                for a in range(nsub)]
        return jnp.concatenate(rows, axis=0) if nsub > 1 else rows[0]

    def step(near):
        if mode == "fox":
            if near == "diag":
                row = lax.broadcasted_iota(I32, (tile, tile), 0)
                col = lax.broadcasted_iota(I32, (tile, tile), 1)
                keep = row >= col
        else:
            shared = mask_ref[0]
        for h in range(N_HEADS):
            sl = slice(h * HEAD_DIM, (h + 1) * HEAD_DIM)
            s = lax.dot_general(q_ref[0, :, sl], k_ref[0, :, sl], (((1,), (1,)), ((), ())),
                                preferred_element_type=F32) * scale
            if mode == "fox":
                s = s - fk_ref[0, h:h + 1, :]
                if near == "diag":
                    s = jnp.where(keep, s, NEG)
            else:
                s = s + shared
                if near != "far":
                    s = s + t5_bias(h, near)
            m_prev = m_sc[h]
            m_new = jnp.maximum(m_prev, jnp.max(s, axis=1, keepdims=True))
            alpha = jnp.exp(m_prev - m_new)
            p = jnp.exp(s - m_new)
            l_sc[h] = alpha * l_sc[h] + jnp.sum(p, axis=1, keepdims=True)
            acc_sc[:, sl] = alpha * acc_sc[:, sl] + jnp.dot(p.astype(BF16), v_ref[0, :, sl],
                                                            preferred_element_type=F32)
            m_sc[h] = m_new

    if mode == "fox":
        pl.when(ki < qi)(lambda: step("far"))
        pl.when(ki == qi)(lambda: step("diag"))
    else:
        pl.when(ki < qi - 1)(lambda: step("far"))
        pl.when(ki == qi - 1)(lambda: step("next"))
        pl.when(ki == qi)(lambda: step("diag"))

    @pl.when(ki == qi)
    def _():
        for h in range(N_HEADS):
            sl = slice(h * HEAD_DIM, (h + 1) * HEAD_DIM)
            o_ref[0, :, sl] = (acc_sc[:, sl] / l_sc[h]).astype(o_ref.dtype)


def _flash(mode, q, k, v, *side, tile):
    B, T, W = q.shape
    nt = T // tile
    qspec = pl.BlockSpec((1, tile, W), lambda b, qi, ki: (b, qi, 0))
    kspec = pl.BlockSpec((1, tile, W), lambda b, qi, ki: (b, jnp.minimum(ki, qi), 0))
    if mode == "fox":
        side_specs = [pl.BlockSpec((1, N_HEADS, tile), lambda b, qi, ki: (b, 0, jnp.minimum(ki, qi)))]
    else:
        side_specs = [pl.BlockSpec((1, tile, tile), lambda b, qi, ki: (b, qi, jnp.minimum(ki, qi))),
                      pl.BlockSpec((N_HEADS, 2, LANES, LANES), lambda b, qi, ki: (0, 0, 0, 0))]
    return pl.pallas_call(
        functools.partial(_flash_body, mode=mode, tile=tile, scale=HEAD_DIM ** -0.5),
        name="flash_" + mode, grid=(B, nt, nt),
        in_specs=[qspec, kspec, kspec] + side_specs,
        out_specs=pl.BlockSpec((1, tile, W), lambda b, qi, ki: (b, qi, 0)),
        out_shape=jax.ShapeDtypeStruct((B, T, W), BF16),
        scratch_shapes=[pltpu.VMEM((N_HEADS, tile, 1), F32), pltpu.VMEM((N_HEADS, tile, 1), F32),
                        pltpu.VMEM((tile, W), F32)],
        compiler_params=_params("parallel", "parallel", "arbitrary"),
    )(q, k, v, *side)


def _sortable_key(x):
    bits = pltpu.bitcast(x, I32)
    return jnp.where(bits < 0, bits ^ jnp.int32(0x7FFFFFFF), bits)


def _kth_largest_key(count_ge, shape, k):
    def body(b, cur):
        cand = cur | lax.shift_left(jnp.int32(1), 31 - b)
        cnt = count_ge(cand ^ jnp.int32(INT_MIN))
        return jnp.where(cnt >= k, cand, cur)
    cur = lax.fori_loop(0, 32, body, jnp.zeros(shape, I32))
    return cur ^ jnp.int32(INT_MIN)


def _dsa_select_body(qi_ref, small_ref, kit_ref, mask_ref, key_sc, cut_sc, *, tq, T, topk):
    q0 = pl.program_id(1) * tq
    w = small_ref[0][:, N_HEADS:N_HEADS + H_IDX] * (H_IDX ** -0.5 * D_IDX ** -0.5)
    n_valid = pl.program_id(1) + 1
    row_c = q0 + lax.broadcasted_iota(I32, (tq, tq), 0)
    lane_c = lax.broadcasted_iota(I32, (tq, tq), 1)

    def score_chunk(c, carry):
        c0 = pl.multiple_of(c * tq, tq)
        kt = kit_ref[0, :, pl.ds(c0, tq)]
        acc = jnp.zeros((tq, tq), F32)
        for h in range(H_IDX):
            d = jnp.dot(qi_ref[0, :, h * D_IDX:(h + 1) * D_IDX], kt, preferred_element_type=F32)
            acc = acc + w[:, h:h + 1] * jnp.maximum(d, 0.0)
        key_sc[:, pl.ds(c0, tq)] = jnp.where(c0 + lane_c <= row_c, _sortable_key(acc), jnp.int32(INT_MIN))
        return carry

    lax.fori_loop(0, n_valid, score_chunk, 0)

    lane = lax.broadcasted_iota(I32, (tq, LANES), 1)

    def count(ind):
        def body(c, part):
            for s in range(tq // LANES):
                c0 = pl.multiple_of(c * tq + s * LANES, LANES)
                part = part + ind(key_sc[:, pl.ds(c0, LANES)], c0 + lane)
            return part
        part = lax.fori_loop(0, n_valid, body, jnp.zeros((tq, LANES), I32))
        return jnp.sum(part, axis=1, keepdims=True)

    one = lambda pred: jnp.where(pred, 1, 0)
    thr = _kth_largest_key(lambda t: count(lambda k, col: one(k >= t)), (tq, 1), topk)
    n_gt = count(lambda k, col: one(k > thr))
    n_ge = count(lambda k, col: one(k >= thr))
    excess = jnp.where(thr > jnp.int32(INT_MIN), n_ge - topk, 0)
    cut_sc[...] = jnp.full((tq, 1), T, I32)

    @pl.when(jnp.max(excess) > 0)
    def _():
        need = topk - n_gt
        nbits = max(1, (T - 1).bit_length())

        def body(b, cut):
            cand = cut | lax.shift_left(jnp.int32(1), nbits - 1 - b)
            cnt = count(lambda k, col: jnp.where(k == thr, one(col < cand), 0))
            return jnp.where(cnt < need, cand, cut)
        cut = lax.fori_loop(0, nbits, body, jnp.zeros((tq, 1), I32))
        cut_sc[...] = jnp.where(excess > 0, cut, T)

    cut = cut_sc[...]
    row = q0 + lax.broadcasted_iota(I32, (tq, LANES), 0)

    def write_chunk(c, carry):
        for s in range(tq // LANES):
            c0 = pl.multiple_of(c * tq + s * LANES, LANES)
            k = key_sc[:, pl.ds(c0, LANES)]
            col = c0 + lane
            val = jnp.where(k > thr, 0.0, jnp.where(k == thr, jnp.where(col <= cut, 0.0, NEG), NEG))
            mask_ref[0, :, pl.ds(c0, LANES)] = jnp.where(col <= row, val, NEG)
        return carry

    def fill_chunk(c, carry):
        mask_ref[0, :, pl.ds(pl.multiple_of(c * tq, tq), tq)] = jnp.full((tq, tq), NEG, F32)
        return carry

    lax.fori_loop(0, n_valid, write_chunk, 0)
    lax.fori_loop(n_valid, T // tq, fill_chunk, 0)


def _dsa_select(qi, small, ki_t, topk, tq=256):
    B, T, _ = qi.shape
    tq = min(tq, T)
    return pl.pallas_call(
        functools.partial(_dsa_select_body, tq=tq, T=T, topk=topk),
        name="dsa_select", grid=(B, T // tq),
        in_specs=[pl.BlockSpec((1, tq, H_IDX * D_IDX), lambda b, i: (b, i, 0)),
                  pl.BlockSpec((1, tq, LANES), lambda b, i: (b, i, 0)),
                  pl.BlockSpec((1, D_IDX, T), lambda b, i: (b, 0, 0))],
        out_specs=pl.BlockSpec((1, tq, T), lambda b, i: (b, i, 0)),
        out_shape=jax.ShapeDtypeStruct((B, T, T), F32),
        scratch_shapes=[pltpu.VMEM((tq, T), I32), pltpu.VMEM((tq, 1), I32)],
        compiler_params=_params("parallel", "parallel"),
    )(qi, small, ki_t)


def _t5_bucket(dist):
    n = jnp.maximum(dist, 0)
    max_exact = N_BUCKETS // 2
    nf = jnp.maximum(n, 1).astype(F32)
    large = max_exact + (jnp.log(nf / max_exact) / math.log(MAX_DIST / max_exact)
                         * (N_BUCKETS - max_exact)).astype(I32)
    large = jnp.minimum(large, N_BUCKETS - 1)
    return jnp.where(n < max_exact, n, large)


def _t5_tiles(rel_table):
    assert LANES >= MAX_DIST
    i = jnp.arange(LANES)
    bucket = _t5_bucket((jnp.arange(2) * LANES)[:, None, None] + i[None, :, None] - i[None, None, :])
    rel = rel_table.astype(F32)
    tz = jnp.zeros((rel.shape[1],) + bucket.shape, F32)
    for k in range(N_BUCKETS - 1):
        tz = tz + jnp.where(bucket[None] == k, (rel[k] - rel[N_BUCKETS - 1])[:, None, None, None], 0.0)
    return tz


def _s5_disc_body(lr_ref, li_ref, ldt_ref, br_ref, bi_ref, abr_ref, abi_ref, bbr_ref, bbi_ref):
    lr, li = lr_ref[...], li_ref[...]
    dt = jnp.exp(ldt_ref[...])
    mag = jnp.exp(lr * dt)
    ab_re, ab_im = mag * jnp.cos(li * dt), mag * jnp.sin(li * dt)
    den = lr * lr + li * li
    nr = ab_re - 1.0
    k_re = (nr * lr + ab_im * li) / den
    k_im = (ab_im * lr - nr * li) / den
    br, bi = br_ref[...], bi_ref[...]
    abr_ref[...] = ab_re
    abi_ref[...] = ab_im
    bbr_ref[...] = k_re * br - k_im * bi
    bbi_ref[...] = k_re * bi + k_im * br


def _s5_discretise(lam_re, lam_im, log_dt, b_re, b_im):
    G, N, P = b_re.shape
    rep = lambda a: jnp.broadcast_to(a.reshape(G * N, 1), (G * N, P))
    ldt = jnp.broadcast_to(log_dt.reshape(G, 1, 1), (G, N, P)).reshape(G * N, P)
    shp = jax.ShapeDtypeStruct((G * N, P), F32)
    abr, abi, bbr, bbi = pl.pallas_call(
        _s5_disc_body, name="s5_discretise", out_shape=[shp] * 4,
        compiler_params=pltpu.CompilerParams(vmem_limit_bytes=V7X_VMEM_LIMIT_BYTES),
    )(rep(lam_re), rep(lam_im), ldt, b_re.reshape(G * N, P), b_im.reshape(G * N, P))
    return (abr[:, 0].reshape(G, N), abi[:, 0].reshape(G, N),
            bbr.reshape(G, N, P), bbi.reshape(G, N, P))


def _s5_scan_body(bu_ref, a_ref, x0_ref, xs_ref, xT_ref, carry_sc, *, tc, S, lc):
    c = pl.program_id(1)

    @pl.when(c == 0)
    def _():
        carry_sc[...] = x0_ref[0]

    for j in range(S // lc):
        slr = slice(j * lc, (j + 1) * lc)
        sli = slice(S + j * lc, S + (j + 1) * lc)
        ar, ai = a_ref[:, slr], a_ref[:, sli]

        def step(t, carry, slr=slr, sli=sli, ar=ar, ai=ai):
            xr, xi = carry
            nr = ar * xr - ai * xi + bu_ref[0, pl.ds(t, 1), slr]
            ni = ar * xi + ai * xr + bu_ref[0, pl.ds(t, 1), sli]
            xs_ref[0, pl.ds(t, 1), slr] = nr
            xs_ref[0, pl.ds(t, 1), sli] = ni
            return nr, ni

        xr, xi = lax.fori_loop(0, tc, step, (carry_sc[:, slr], carry_sc[:, sli]), unroll=min(8, tc))
        carry_sc[:, slr] = xr
        carry_sc[:, sli] = xi

    xT_ref[0] = carry_sc[...]


def _s5_scan(bu, a_row, x0, tc=128, lc=1024):
    B, T, S2 = bu.shape
    S = S2 // 2
    tc = min(tc, T)
    return pl.pallas_call(
        functools.partial(_s5_scan_body, tc=tc, S=S, lc=lc),
        name="s5_scan", grid=(B, T // tc),
        in_specs=[pl.BlockSpec((1, tc, S2), lambda b, c: (b, c, 0)),
                  pl.BlockSpec((1, S2), lambda b, c: (0, 0)),
                  pl.BlockSpec((1, 1, S2), lambda b, c: (b, 0, 0))],
        out_specs=[pl.BlockSpec((1, tc, S2), lambda b, c: (b, c, 0)),
                   pl.BlockSpec((1, 1, S2), lambda b, c: (b, 0, 0))],
        out_shape=[jax.ShapeDtypeStruct((B, T, S2), F32), jax.ShapeDtypeStruct((B, 1, S2), F32)],
        scratch_shapes=[pltpu.VMEM((1, S2), F32)],
        compiler_params=_params("parallel", "arbitrary"),
    )(bu, a_row, x0)


def _glu_body(y_ref, wa_ref, wb_ref, o_ref):
    y = y_ref[...]
    a = jnp.dot(y, wa_ref[...], preferred_element_type=F32)
    b = jnp.dot(y, wb_ref[...], preferred_element_type=F32)
    o_ref[...] = (a * _sigmoid(b)).astype(o_ref.dtype)


def _glu(y, wa, wb, tm=512):
    M, K = y.shape
    N = wa.shape[1]
    tm = min(tm, M)
    return pl.pallas_call(
        _glu_body, name="s5_glu", grid=(M // tm,),
        in_specs=[pl.BlockSpec((tm, K), lambda i: (i, 0)), pl.BlockSpec((K, N), lambda i: (0, 0)),
                  pl.BlockSpec((K, N), lambda i: (0, 0))],
        out_specs=pl.BlockSpec((tm, N), lambda i: (i, 0)),
        out_shape=jax.ShapeDtypeStruct((M, N), BF16),
        compiler_params=_params("parallel"),
    )(y, wa, wb)


def _merge_body(oa_ref, ob_ref, oc_ref, wa_ref, wb_ref, wc_ref, ga_ref, gb_ref, gc_ref, o_ref):
    dot = lambda x, w: jnp.dot(x[...], w[...], preferred_element_type=F32)
    m = (_sigmoid(ga_ref[...]) * dot(oa_ref, wa_ref) + _sigmoid(gb_ref[...]) * dot(ob_ref, wb_ref)
         + _sigmoid(gc_ref[...]) * dot(oc_ref, wc_ref))
    o_ref[...] = m.astype(o_ref.dtype)


def _merge(o_a, o_b, o_c, wa, wb, wc, gl, tm=512, tn=512):
    M, K = o_a.shape
    D = wa.shape[1]
    tm = min(tm, M)
    nd = D // tn
    ospec = pl.BlockSpec((tm, K), lambda i, j: (i, 0))
    wspec = pl.BlockSpec((K, tn), lambda i, j: (0, j))
    gspec = lambda g: pl.BlockSpec((tm, tn), lambda i, j, g=g: (i, g * nd + j))
    return pl.pallas_call(
        _merge_body, name="branch_merge", grid=(M // tm, nd),
        in_specs=[ospec, ospec, ospec, wspec, wspec, wspec, gspec(0), gspec(1), gspec(2)],
        out_specs=pl.BlockSpec((tm, tn), lambda i, j: (i, j)),
        out_shape=jax.ShapeDtypeStruct((M, D), BF16),
        compiler_params=_params("parallel", "parallel"),
    )(o_a, o_b, o_c, wa, wb, wc, gl, gl, gl)


def _top16_rows(s, n_rows):
    iota = lax.broadcasted_iota(I32, s.shape, 0)
    rank = jnp.full(s.shape, P_TOPK, I32)
    vals = []
    work = s
    for k in range(P_TOPK):
        m = jnp.max(work, axis=0, keepdims=True)
        idx = jnp.min(jnp.where(work == m, iota, n_rows), axis=0, keepdims=True)
        hit = iota == idx
        rank = jnp.where(hit, k, rank)
        work = jnp.where(hit, -jnp.inf, work)
        vals.append(m)
    return rank, vals


def _peer_select_body(q_ref, keys_ref, a1_ref, n1_ref, a2_ref, r2_ref, cand_sc, sel_sc):
    tt = q_ref.shape[0]
    nt = (((1,), (1,)), ((), ()))
    for h in range(P_HEADS):
        s1 = lax.dot_general(keys_ref[2 * h], q_ref[:, (2 * h) * N_KEYS:(2 * h + 1) * N_KEYS], nt,
                             preferred_element_type=F32)
        s2 = lax.dot_general(keys_ref[2 * h + 1], q_ref[:, (2 * h + 1) * N_KEYS:(2 * h + 2) * N_KEYS], nt,
                             preferred_element_type=F32)
        r1, v1 = _top16_rows(s1, N_KEYS)
        r2, v2 = _top16_rows(s2, N_KEYS)
        for r, (i, j) in enumerate(_PAIRS):
            cand_sc[r:r + 1, :] = v1[i] + v2[j]
        cand_sc[len(_PAIRS):, :] = jnp.full((_N_CAND - len(_PAIRS), tt), -jnp.inf, F32)
        rc, cv = _top16_rows(cand_sc[...], _N_CAND)
        z = jnp.zeros((1, tt), F32)
        for k in range(P_TOPK):
            z = z + jnp.exp(cv[k] - cv[0])
        sel_sc[...] = jnp.where(rc < P_TOPK, 1, 0)
        cnt = [jnp.zeros((1, tt), I32) for _ in range(P_TOPK)]
        for r, (i, j) in enumerate(_PAIRS):
            cnt[i] = cnt[i] + sel_sc[r:r + 1, :]
        n1 = jnp.zeros((N_KEYS, tt), I32)
        for i in range(P_TOPK):
            n1 = jnp.where(r1 == i, cnt[i], n1)
        a1_ref[h] = jnp.exp(s1 - v1[0]) / z
        n1_ref[h] = n1.astype(F32)
        a2_ref[h] = jnp.exp(s2 - v2[0]).astype(a2_ref.dtype)
        r2_ref[h] = r2.astype(F32).astype(r2_ref.dtype)


def _peer_select(q, keys, tt=256):
    M = q.shape[0]
    tt = min(tt, M)
    tab = pl.BlockSpec((P_HEADS, N_KEYS, tt), lambda i: (0, 0, i))
    shp = lambda dt: jax.ShapeDtypeStruct((P_HEADS, N_KEYS, M), dt)
    return pl.pallas_call(
        _peer_select_body, name="peer_select", grid=(M // tt,),
        in_specs=[pl.BlockSpec((tt, q.shape[1]), lambda i: (i, 0)),
                  pl.BlockSpec(keys.shape, lambda i: (0, 0, 0))],
        out_specs=[tab, tab, tab, tab],
        out_shape=[shp(F32), shp(F32), shp(BF16), shp(BF16)],
        scratch_shapes=[pltpu.VMEM((_N_CAND, tt), F32), pltpu.VMEM((_N_CAND, tt), I32)],
        compiler_params=_params("parallel"),
    )(q, keys)


def _peer_dense_body(h_ref, u_ref, vt_ref, a1_ref, n1_ref, a2_ref, r2_ref, x_ref, gt_ref, o_ref,
                     acc_sc, act_sc, ga_sc, *, te):
    j = pl.program_id(1)

    @pl.when(j == 0)
    def _():
        acc_sc[...] = jnp.zeros(acc_sc.shape, F32)

    act_sc[...] = lax.dot_general(u_ref[...], h_ref[...], (((1,), (1,)), ((), ())),
                                  preferred_element_type=F32)
    for r in range(te // N_KEYS):
        rows = slice(r * N_KEYS, (r + 1) * N_KEYS)
        wt = None
        for h in range(P_HEADS):
            a1 = a1_ref[h, r:r + 1, :].astype(BF16)
            n1 = n1_ref[h, r:r + 1, :].astype(BF16)
            term = jnp.where(r2_ref[h] < n1, a1 * a2_ref[h], jnp.zeros((), BF16))
            wt = term if wt is None else wt + term
        ga_sc[rows, :] = wt * _gelu(act_sc[rows, :]).astype(BF16)
    acc_sc[...] += jnp.dot(vt_ref[...], ga_sc[...], preferred_element_type=F32)

    @pl.when(j == pl.num_programs(1) - 1)
    def _():
        o_ref[...] = x_ref[...] + gt_ref[...] * acc_sc[...].T


def _peer_dense(h2, u, v_t, tabs, x, gt, rows_per_group, tt=512, te=1024):
    M, D = h2.shape
    E = u.shape[0]
    tt = min(tt, M)
    n_e1 = te // N_KEYS
    tab1 = pl.BlockSpec((P_HEADS, n_e1, tt), lambda i, j: (0, j, i))
    tab2 = pl.BlockSpec((P_HEADS, N_KEYS, tt), lambda i, j: (0, 0, i))
    r = gt.shape[1]
    return pl.pallas_call(
        functools.partial(_peer_dense_body, te=te),
        name="peer_dense", grid=(M // tt, E // te),
        in_specs=[pl.BlockSpec((tt, D), lambda i, j: (i, 0)),
                  pl.BlockSpec((te, D), lambda i, j: (j, 0)),
                  pl.BlockSpec((D, te), lambda i, j: (0, j)),
                  tab1, tab1, tab2, tab2,
                  pl.BlockSpec((tt, D), lambda i, j: (i, 0)),
                  pl.BlockSpec((None, r, D), lambda i, j: ((i * tt) // rows_per_group, 0, 0))],
        out_specs=pl.BlockSpec((tt, D), lambda i, j: (i, 0)),
        out_shape=jax.ShapeDtypeStruct((M, D), F32),
        scratch_shapes=[pltpu.VMEM((D, tt), F32), pltpu.VMEM((te, tt), F32), pltpu.VMEM((te, tt), BF16)],
        compiler_params=_params("parallel", "arbitrary"),
    )(h2, u, v_t, *tabs, x, gt)


def _peer(h2, x, gt, rows_per_group, wq, keys, u, v_t):
    (q,) = _mm(h2, wq, [BF16], tm=512, tn=1024, name="peer_query")
    tabs = _peer_select(q, keys)
    return _peer_dense(h2, u, v_t, tabs, x, gt, rows_per_group)


def _s5_prepare(lam_re, lam_im, log_dt, b_re, b_im, c_re, c_im):
    G, N, P = b_re.shape
    S, W = G * N, G * P
    ab_re, ab_im, bb_re, bb_im = _s5_discretise(lam_re, lam_im, log_dt, b_re, b_im)
    eye = jnp.eye(G, dtype=F32)
    blk = lambda m, rows, cols: (jnp.swapaxes(m, 1, 2)[:, :, None, :] * eye[:, None, :, None]).reshape(rows, cols)
    b_blk = jnp.concatenate([blk(bb_re, W, S), blk(bb_im, W, S)], axis=1).astype(BF16)
    c_blk = jnp.concatenate([blk(c_re.astype(F32), S, W), -blk(c_im.astype(F32), S, W)], axis=0).astype(BF16)
    a_row = jnp.concatenate([ab_re.reshape(1, S), ab_im.reshape(1, S)], axis=1)
    return a_row, b_blk, c_blk


def _s5(u, T, x0_re, x0_im, prep, d_skip):
    M, W = u.shape
    B = M // T
    G, N = x0_re.shape[1:]
    S = G * N
    a_row, b_blk, c_blk = prep
    (bu,) = _mm(u, b_blk, [F32], tm=512, tn=1024, name="s5_input")
    x0 = jnp.concatenate([x0_re.reshape(B, 1, S), x0_im.reshape(B, 1, S)], axis=2).astype(F32)
    xs, x_last = _s5_scan(bu.reshape(B, T, 2 * S), a_row, x0)
    tm, tn = min(256, M), 512
    (yg,) = _mm(xs.reshape(M, 2 * S), c_blk, [BF16], tm=tm, tn=tn,
                epilogue=lambda acc, u_, d_: (_gelu(acc + d_ * u_),),
                extras=[(u, (tm, tn), lambda i, j: (i, j)),
                        (d_skip.reshape(1, W).astype(F32), (1, tn), lambda i, j: (0, j))], name="s5_output")
    return yg, x_last[:, 0, :S].reshape(B, G, N), x_last[:, 0, S:].reshape(B, G, N)


def _logsig_body(f_ref, b_ref, o_ref):
    o_ref[...] = _log_sigmoid(f_ref[...] + b_ref[...])


def _logsig(f, b_row):
    return pl.pallas_call(_logsig_body, out_shape=jax.ShapeDtypeStruct(f.shape, F32))(f, b_row)


def _head_lane_select(n_lanes):
    lane = lax.broadcasted_iota(I32, (N_HEADS, n_lanes), 1)
    head = lax.broadcasted_iota(I32, (N_HEADS, n_lanes), 0)
    return (lane & (N_HEADS - 1)) == head, lane


def _page_prefix_body(lf_ref, pre_ref, tot_ref):
    lf = lf_ref[...]
    n_lanes = lf.shape[1]
    lane = lax.broadcasted_iota(I32, lf.shape, 1)
    pre, tot = lf, lf
    sh = N_HEADS
    while sh < n_lanes:
        pre = pre + jnp.where(lane >= sh, pltpu.roll(pre, sh, axis=1), 0.0)
        tot = tot + pltpu.roll(tot, sh, axis=1)
        sh *= 2
    pre_ref[...] = pre
    tot_ref[...] = tot


def _page_prefix(logf_rows):
    R, n_lanes = logf_rows.shape
    tr = math.gcd(R, 256)
    spec = pl.BlockSpec((tr, n_lanes), lambda i: (i, 0))
    return pl.pallas_call(
        _page_prefix_body, name="fox_page_prefix", grid=(R // tr,), in_specs=[spec], out_specs=[spec, spec],
        out_shape=[jax.ShapeDtypeStruct((R, n_lanes), F32)] * 2,
        compiler_params=_params("parallel"),
    )(logf_rows)


def _fox_sample_body(pt_ref, q_ref, kn_ref, vn_ref, lfn_ref, k_hbm, v_hbm, pre_hbm, tot_hbm, o_ref,
                     kbuf, vbuf, pbuf, tbuf, sems, m_sc, l_sc, acc_sc, carry_sc,
                     *, layer, n_pool, n_pages, pp, scale):
    g = pl.program_id(1)
    ng = pl.num_programs(1)
    step = pl.program_id(0) * ng + g
    n_steps = pl.num_programs(0) * ng
    slot = lax.rem(step, 2)
    page_lanes = PAGE * N_HEADS
    n_lanes = pp * page_lanes
    nt = (((1,), (1,)), ((), ()))

    def copies(st, sl):
        first = (st // ng) * n_pages + lax.rem(st, ng) * pp
        out = []
        for k in range(pp):
            pg = pt_ref[first + k]
            row = layer * n_pool + pg
            out += [pltpu.make_async_copy(k_hbm.at[layer, pg], kbuf.at[sl, pl.ds(k * PAGE, PAGE)], sems.at[0, sl]),
                    pltpu.make_async_copy(v_hbm.at[layer, pg], vbuf.at[sl, pl.ds(k * PAGE, PAGE)], sems.at[1, sl]),
                    pltpu.make_async_copy(pre_hbm.at[pl.ds(row, 1)], pbuf.at[sl, pl.ds(k, 1)], sems.at[2, sl]),
                    pltpu.make_async_copy(tot_hbm.at[pl.ds(row, 1)], tbuf.at[sl, pl.ds(k, 1)], sems.at[3, sl])]
        return out

    @pl.when(step == 0)
    def _():
        for c in copies(step, slot):
            c.start()

    @pl.when(step + 1 < n_steps)
    def _():
        for c in copies(step + 1, 1 - slot):
            c.start()

    @pl.when(g == 0)
    def _():
        m_sc[...] = jnp.full(m_sc.shape, -jnp.inf, F32)
        l_sc[...] = jnp.zeros(l_sc.shape, F32)
        acc_sc[...] = jnp.zeros(acc_sc.shape, F32)
        carry_sc[...] = jnp.zeros(carry_sc.shape, F32)

    for c in copies(step, slot):
        c.wait()

    hsel, lane8 = _head_lane_select(n_lanes)
    q = q_ref[0]
    kp = kbuf[slot].reshape(n_lanes, HEAD_DIM).astype(BF16)
    s = lax.dot_general(q, kp, nt, preferred_element_type=F32) * scale
    carry = carry_sc[...]
    f_pages = []
    for k in range(pp):
        f_pages.append(carry + pbuf[slot, k:k + 1, :])
        carry = carry + tbuf[slot, k:k + 1, :]
    carry_sc[...] = carry
    f_k = jnp.concatenate(f_pages, axis=1) if pp > 1 else f_pages[0]
    s = jnp.where(hsel, s - f_k, NEG)
    m_prev = m_sc[...]
    m_new = jnp.maximum(m_prev, jnp.max(s, axis=1, keepdims=True))
    alpha = jnp.exp(m_prev - m_new)
    pr = jnp.exp(s - m_new)
    l_sc[...] = alpha * l_sc[...] + jnp.sum(pr, axis=1, keepdims=True)
    vp = vbuf[slot].reshape(n_lanes, HEAD_DIM).astype(BF16)
    acc_sc[...] = alpha * acc_sc[...] + jnp.dot(pr.astype(BF16), vp, preferred_element_type=F32)
    m_sc[...] = m_new

    @pl.when(g == ng - 1)
    def _():
        hsel_p, lane_p = _head_lane_select(page_lanes)
        f_col = jnp.sum(jnp.where(hsel_p, jnp.where(lane_p < N_HEADS, carry_sc[...], 0.0), 0.0),
                        axis=1, keepdims=True)
        kn = kn_ref[0].astype(BF16).astype(F32)
        vn = vn_ref[0].astype(BF16).astype(F32)
        s_n = jnp.sum(q.astype(F32) * kn, axis=1, keepdims=True) * scale - (f_col + lfn_ref[0])
        m_prev = m_sc[...]
        m_new = jnp.maximum(m_prev, s_n)
        alpha = jnp.exp(m_prev - m_new)
        p_n = jnp.exp(s_n - m_new)
        l = alpha * l_sc[...] + p_n
        acc = alpha * acc_sc[...] + p_n.astype(BF16).astype(F32) * vn
        o_ref[0] = acc / l


def _fox_sample(layer, pt_flat, q, k_new, v_new, logf_new, cache_k, cache_v, page_pre, page_tot, pp=8):
    Bd = q.shape[0]
    n_pool = cache_k.shape[1]
    n_pages = pt_flat.shape[0] // Bd
    pp = math.gcd(pp, n_pages)
    tok = lambda w: pl.BlockSpec((1, N_HEADS, w), lambda b, g, pt: (b, 0, 0))
    hbm = pl.BlockSpec(memory_space=pl.ANY)
    grid_spec = pltpu.PrefetchScalarGridSpec(
        num_scalar_prefetch=1, grid=(Bd, n_pages // pp),
        in_specs=[tok(HEAD_DIM), tok(HEAD_DIM), tok(HEAD_DIM), tok(1), hbm, hbm, hbm, hbm],
        out_specs=pl.BlockSpec((1, N_HEADS, HEAD_DIM), lambda b, g, pt: (b, 0, 0)),
        scratch_shapes=[pltpu.VMEM((2, pp * PAGE, N_HEADS, HEAD_DIM), F32),
                        pltpu.VMEM((2, pp * PAGE, N_HEADS, HEAD_DIM), F32),
                        pltpu.VMEM((2, pp, PAGE * N_HEADS), F32), pltpu.VMEM((2, pp, PAGE * N_HEADS), F32),
                        pltpu.SemaphoreType.DMA((4, 2)),
                        pltpu.VMEM((N_HEADS, 1), F32), pltpu.VMEM((N_HEADS, 1), F32),
                        pltpu.VMEM((N_HEADS, HEAD_DIM), F32), pltpu.VMEM((1, PAGE * N_HEADS), F32)])
    return pl.pallas_call(
        functools.partial(_fox_sample_body, layer=layer, n_pool=n_pool, n_pages=n_pages, pp=pp,
                          scale=HEAD_DIM ** -0.5),
        name="fox_sample", grid_spec=grid_spec,
        out_shape=jax.ShapeDtypeStruct((Bd, N_HEADS, HEAD_DIM), F32),
        compiler_params=_params("arbitrary", "arbitrary"),
    )(pt_flat, q, k_new, v_new, logf_new, cache_k, cache_v, page_pre, page_tot)


def _idx_scores(qi, w, ki_rows):
    d = lax.dot_general(qi, ki_rows.astype(BF16), (((1,), (1,)), ((), ())), preferred_element_type=F32)
    return jnp.sum(w * (H_IDX ** -0.5 * D_IDX ** -0.5) * jnp.maximum(d, 0.0), axis=0, keepdims=True)


def _idx_scores_body(qi_ref, w_ref, ki_ref, o_ref):
    o_ref[0] = _idx_scores(qi_ref[0], w_ref[0], ki_ref[...])


def _idx_scores_paged_body(pt_ref, qi_ref, w_ref, ki_hbm, o_ref, buf, sem, *, layer, n_pages, chunk):
    b = pl.program_id(0)

    def page_copy(p):
        return pltpu.make_async_copy(ki_hbm.at[layer, pt_ref[b * n_pages + p]],
                                     buf.at[pl.ds(pl.multiple_of(p * PAGE, PAGE), PAGE)], sem.at[0])

    def start(p, c):
        page_copy(p).start()
        return c

    def wait(p, c):
        page_copy(p).wait()
        return c

    lax.fori_loop(0, n_pages, start, 0)
    lax.fori_loop(0, n_pages, wait, 0)
    for c in range(n_pages * PAGE // chunk):
        o_ref[0, :, c * chunk:(c + 1) * chunk] = _idx_scores(qi_ref[0], w_ref[0], buf[c * chunk:(c + 1) * chunk, :])


def _idx_scores_paged(layer, pt_flat, qi, w, cache_ki, chunk=2048):
    Bd = qi.shape[0]
    n_pages = pt_flat.shape[0] // Bd
    chunk = math.gcd(chunk, n_pages * PAGE)
    grid_spec = pltpu.PrefetchScalarGridSpec(
        num_scalar_prefetch=1, grid=(Bd,),
        in_specs=[pl.BlockSpec((1, H_IDX, D_IDX), lambda b, pt: (b, 0, 0)),
                  pl.BlockSpec((1, H_IDX, 1), lambda b, pt: (b, 0, 0)),
                  pl.BlockSpec(memory_space=pl.ANY)],
        out_specs=pl.BlockSpec((1, 1, n_pages * PAGE), lambda b, pt: (b, 0, 0)),
        scratch_shapes=[pltpu.VMEM((n_pages * PAGE, D_IDX), F32), pltpu.SemaphoreType.DMA((1,))])
    return pl.pallas_call(
        functools.partial(_idx_scores_paged_body, layer=layer, n_pages=n_pages, chunk=chunk),
        name="dsa_sample_scores", grid_spec=grid_spec,
        out_shape=jax.ShapeDtypeStruct((Bd, 1, n_pages * PAGE), F32),
        compiler_params=_params("arbitrary"),
    )(pt_flat, qi, w, cache_ki)


def _idx_scores_new(qi, w, ki_rows):
    Bd, R, _ = ki_rows.shape
    return pl.pallas_call(
        _idx_scores_body, grid=(Bd,),
        in_specs=[pl.BlockSpec((1, H_IDX, D_IDX), lambda b: (b, 0, 0)),
                  pl.BlockSpec((1, H_IDX, 1), lambda b: (b, 0, 0)),
                  pl.BlockSpec((None, R, D_IDX), lambda b: (b, 0, 0))],
        out_specs=pl.BlockSpec((1, 1, R), lambda b: (b, 0, 0)),
        out_shape=jax.ShapeDtypeStruct((Bd, 1, R), F32),
        compiler_params=_params("parallel"),
    )(qi, w, ki_rows)


def _dsa_sample_select_body(sc_ref, scn_ref, rel_ref, idx_ref, sb_ref, nb_ref, rs_sc, *, topk, chunk):
    Bd, P = sc_ref.shape
    key_p = _sortable_key(sc_ref[...])
    key_n = _sortable_key(scn_ref[:, 0:1])
    col = lax.broadcasted_iota(I32, (Bd, P), 1)

    one = lambda pred: jnp.where(pred, 1, 0)

    def count(ind_p, ind_n):
        return jnp.sum(ind_p, axis=1, keepdims=True) + ind_n

    thr = _kth_largest_key(lambda t: count(one(key_p >= t), one(key_n >= t)), (Bd, 1), topk)
    n_gt = count(one(key_p > thr), one(key_n > thr))
    n_ge = count(one(key_p >= thr), one(key_n >= thr))
    need = topk - n_gt
    nbits = (P + 1).bit_length()

    def body(b, cut):
        cand = cut | lax.shift_left(jnp.int32(1), nbits - 1 - b)
        cnt = count(jnp.where(key_p == thr, one(col < cand), 0), jnp.where(key_n == thr, one(P < cand), 0))
        return jnp.where(cnt < need, cand, cut)
    cut = lax.fori_loop(0, nbits, body, jnp.zeros((Bd, 1), I32))
    cut = jnp.where(n_ge > topk, cut, jnp.int32(2 ** 30))
    sel_p = jnp.where(key_p > thr, 1, jnp.where(key_p == thr, one(col <= cut), 0))
    sel_n = jnp.where(key_n > thr, 1, jnp.where(key_n == thr, one(P <= cut), 0))
    rank = sel_p
    sh = 1
    while sh < P:
        rank = rank + jnp.where(col >= sh, pltpu.roll(rank, sh, axis=1), 0)
        sh *= 2
    rs_sc[...] = sel_p * rank
    n_past = jnp.sum(sel_p, axis=1, keepdims=True)
    nb_ref[...] = jnp.where(sel_n > 0, rel_ref[0:1, :], NEG)

    slot = lax.broadcasted_iota(I32, (topk, 1), 0)
    ccol = lax.broadcasted_iota(I32, (topk, chunk), 1)
    for b in range(Bd):
        idx = jnp.zeros((topk, 1), I32)
        for c in range(P // chunk):
            rs = rs_sc[b:b + 1, c * chunk:(c + 1) * chunk]
            idx = idx + jnp.sum(jnp.where(rs == slot + 1, ccol + c * chunk, 0), axis=1, keepdims=True)
        idx_ref[b] = idx
        bucket = _t5_bucket(P - idx)
        bias = jnp.zeros((topk, N_HEADS), F32)
        for k in range(N_BUCKETS):
            bias = bias + jnp.where(bucket == k, rel_ref[k:k + 1, :], 0.0)
        sb_ref[b] = jnp.where(slot < n_past[b:b + 1, :], bias, NEG)


def _dsa_sample_select(sc, sc_new, rel_table, topk, chunk=2048):
    Bd, P = sc.shape
    return pl.pallas_call(
        functools.partial(_dsa_sample_select_body, topk=topk, chunk=min(chunk, P)),
        out_shape=[jax.ShapeDtypeStruct((Bd, topk, 1), I32), jax.ShapeDtypeStruct((Bd, topk, N_HEADS), F32),
                   jax.ShapeDtypeStruct((Bd, N_HEADS), F32)],
        scratch_shapes=[pltpu.VMEM((Bd, P), I32)],
        compiler_params=pltpu.CompilerParams(vmem_limit_bytes=V7X_VMEM_LIMIT_BYTES),
    )(sc, sc_new, rel_table)


def _dsa_sample_attend_body(idx_ref, pt_ref, q_ref, kn_ref, vn_ref, sb_ref, nb_ref, kc_hbm, vc_hbm, o_ref,
                            kbuf, vbuf, sems, *, layer, topk, n_pages, scale):
    b = pl.program_id(0)

    def row_copies(j):
        i = idx_ref[b * topk + j]
        pg = pt_ref[b * n_pages + lax.shift_right_logical(i, 7)]
        off = i & (PAGE - 1)
        return (pltpu.make_async_copy(kc_hbm.at[layer, pg, off], kbuf.at[j], sems.at[0]),
                pltpu.make_async_copy(vc_hbm.at[layer, pg, off], vbuf.at[j], sems.at[1]))

    def start(j, c):
        ck, cv = row_copies(j)
        ck.start()
        cv.start()
        return c

    def wait(j, c):
        ck, cv = row_copies(j)
        ck.wait()
        cv.wait()
        return c

    lax.fori_loop(0, topk, start, 0)
    lax.fori_loop(0, topk, wait, 0)

    n_lanes = topk * N_HEADS
    hsel, _ = _head_lane_select(n_lanes)
    q = q_ref[0]
    kb = kbuf[...].reshape(n_lanes, HEAD_DIM).astype(BF16)
    s = lax.dot_general(q, kb, (((1,), (1,)), ((), ())), preferred_element_type=F32) * scale
    s = jnp.where(hsel, s + sb_ref[0], NEG)
    kn = kn_ref[0].astype(BF16).astype(F32)
    vn = vn_ref[0].astype(BF16).astype(F32)
    s_n = jnp.sum(q.astype(F32) * kn, axis=1, keepdims=True) * scale + nb_ref[0]
    m = jnp.maximum(jnp.max(s, axis=1, keepdims=True), s_n)
    p = jnp.exp(s - m)
    p_n = jnp.exp(s_n - m)
    l = jnp.sum(p, axis=1, keepdims=True) + p_n
    vb = vbuf[...].reshape(n_lanes, HEAD_DIM).astype(BF16)
    acc = jnp.dot(p.astype(BF16), vb, preferred_element_type=F32) + p_n.astype(BF16).astype(F32) * vn
    o_ref[0] = acc / l


def _dsa_sample_attend(layer, idx_flat, pt_flat, q, k_new, v_new, slot_bias, new_bias, cache_k, cache_v, topk):
    Bd = q.shape[0]
    n_pages = pt_flat.shape[0] // Bd
    tok = lambda w: pl.BlockSpec((1, N_HEADS, w), lambda b, idx, pt: (b, 0, 0))
    grid_spec = pltpu.PrefetchScalarGridSpec(
        num_scalar_prefetch=2, grid=(Bd,),
        in_specs=[tok(HEAD_DIM), tok(HEAD_DIM), tok(HEAD_DIM),
                  pl.BlockSpec((1, 1, topk * N_HEADS), lambda b, idx, pt: (b, 0, 0)),
                  tok(1),
                  pl.BlockSpec(memory_space=pl.ANY), pl.BlockSpec(memory_space=pl.ANY)],
        out_specs=pl.BlockSpec((1, N_HEADS, HEAD_DIM), lambda b, idx, pt: (b, 0, 0)),
        scratch_shapes=[pltpu.VMEM((topk, N_HEADS, HEAD_DIM), F32), pltpu.VMEM((topk, N_HEADS, HEAD_DIM), F32),
                        pltpu.SemaphoreType.DMA((2,))])
    return pl.pallas_call(
        functools.partial(_dsa_sample_attend_body, layer=layer, topk=topk, n_pages=n_pages,
                          scale=HEAD_DIM ** -0.5),
        grid_spec=grid_spec,
        out_shape=jax.ShapeDtypeStruct((Bd, N_HEADS, HEAD_DIM), F32),
        compiler_params=_params("arbitrary"),
    )(idx_flat, pt_flat, q, k_new, v_new, slot_bias, new_bias, cache_k, cache_v)


W_ATT = N_HEADS * HEAD_DIM
_PROJ_NAMES = ("fq", "fk", "fv", "ff", "bq", "bk", "bv", "iq", "ik", "iw", "su", "gl")
_FF_LANES = slice(0, N_HEADS)
_IW_LANES = slice(N_HEADS, N_HEADS + H_IDX)
_IK_LANES = slice(N_HEADS + H_IDX, N_HEADS + H_IDX + D_IDX)


def _layer_weights(l, p):
    D = p["w_in"].shape[1]
    w_c = p["w_glu"].shape[1]
    sizes = (W_ATT, W_ATT, W_ATT, N_HEADS, W_ATT, W_ATT, W_ATT, H_IDX * D_IDX, D_IDX, H_IDX, w_c, 3 * D)
    w_in = p["w_in"][l]
    cols, off = {}, 0
    for name, n in zip(_PROJ_NAMES, sizes):
        cols[name] = w_in[:, off:off + n]
        off += n
    pad = jnp.zeros((D, LANES - (N_HEADS + H_IDX + D_IDX)), w_in.dtype)
    lw = {n: cols[n].astype(BF16) for n in ("fq", "fk", "fv", "bq", "bk", "bv", "iq", "su", "gl")}
    lw["small"] = jnp.concatenate([cols["ff"], cols["iw"], cols["ik"], pad], axis=1).astype(BF16)
    w_glu, w_br = p["w_glu"][l], p["w_br"][l]
    lw["glu_a"], lw["glu_b"] = w_glu[:, :w_c].astype(BF16), w_glu[:, w_c:].astype(BF16)
    lw["wa"] = w_br[:W_ATT].astype(BF16)
    lw["wb"] = w_br[W_ATT:2 * W_ATT].astype(BF16)
    lw["wc"] = w_br[2 * W_ATT:].astype(BF16)
    lw["w_out"] = p["w_out"][l].astype(BF16)
    lw["wq"] = p["peer_wq"][l].astype(BF16)
    lw["keys"] = p["peer_keys"][l].reshape(2 * P_HEADS, N_KEYS, -1).astype(BF16)
    lw["u"] = p["peer_u"][l].astype(BF16)
    lw["v_t"] = p["peer_v"][l].T.astype(BF16)
    lw["s5"] = _s5_prepare(p["s5_lam_re"][l], p["s5_lam_im"][l], p["s5_log_dt"][l], p["s5_b_re"][l],
                           p["s5_b_im"][l], p["s5_c_re"][l], p["s5_c_im"][l])
    lw["s5_d"] = p["s5_d"][l]
    lw["norm1_g"], lw["norm2_g"] = p["norm1_g"][l], p["norm2_g"][l]
    return lw


def _layer(x, mods, rows_per_group, T, lw, s5_re0, s5_im0, attend):
    M, D = x.shape
    sh1, sc1, gt1, sh2, sc2, gt2 = mods
    tm = min(512, M)
    h = _norm_mod(x, lw["norm1_g"], sc1, sh1, rows_per_group)
    proj = lambda name, dts: _mm(h, lw[name], dts, tm=tm, tn=1024, order="nm", name="proj_" + name)
    (fq,) = proj("fq", [BF16])
    fk, fk16 = proj("fk", [F32, BF16])
    fv, fv16 = proj("fv", [F32, BF16])
    (bq,) = proj("bq", [BF16])
    bk, bk16 = proj("bk", [F32, BF16])
    bv, bv16 = proj("bv", [F32, BF16])
    (iq,) = proj("iq", [BF16])
    (su,) = proj("su", [F32])
    (small,) = proj("small", [F32])
    (gl,) = proj("gl", [F32])
    o_a, o_b, logf = attend(fq, fk, fk16, fv, fv16, bq, bk, bk16, bv, bv16, iq, small)
    yg, s5_re, s5_im = _s5(su, T, s5_re0, s5_im0, lw["s5"], lw["s5_d"])
    o_c = _glu(yg, lw["glu_a"], lw["glu_b"])
    merged = _merge(o_a, o_b, o_c, lw["wa"], lw["wb"], lw["wc"], gl)
    tn = 512
    (x1,) = _mm(merged, lw["w_out"], [F32], tm=tm, tn=tn,
                epilogue=lambda acc, x_, g_: (x_ + g_ * acc,),
                extras=[(x, (tm, tn), lambda i, j: (i, j)), _mod_extra(gt1, rows_per_group, tm, tn)],
                name="out_proj")
    h2 = _norm_mod(x1, lw["norm2_g"], sc2, sh2, rows_per_group)
    if M < LANES:
        padr = lambda a: jnp.pad(a, ((0, LANES - M), (0, 0)))
        gt2p = jnp.pad(gt2, ((0, 0), (0, LANES - M), (0, 0)))
        x2 = _peer(padr(h2), padr(x1), gt2p, LANES, lw["wq"], lw["keys"], lw["u"], lw["v_t"])[:M]
    else:
        x2 = _peer(h2, x1, gt2, rows_per_group, lw["wq"], lw["keys"], lw["u"], lw["v_t"])
    return x2, (fk, fv, logf, bk, bv, small[:, _IK_LANES], s5_re, s5_im)


def kernel(x_prompt, x_sample, cache_fox_k, cache_fox_v, cache_fox_logf, cache_dsa_k, cache_dsa_v,
           cache_dsa_idx_k, state_s5_re, state_s5_im, page_table, c_prompt, c_sample,
           w_ada, b_ada, norm1_g, norm2_g, w_in, b_f, rel_table, s5_lam_re, s5_lam_im, s5_log_dt,
           s5_b_re, s5_b_im, s5_c_re, s5_c_im, s5_d, w_glu, w_br, w_out,
           peer_wq, peer_keys, peer_u, peer_v, final_norm_g):
    p = dict(w_in=w_in, w_glu=w_glu, w_br=w_br, w_out=w_out, peer_wq=peer_wq, peer_keys=peer_keys,
             peer_u=peer_u, peer_v=peer_v, s5_lam_re=s5_lam_re, s5_lam_im=s5_lam_im, s5_log_dt=s5_log_dt,
             s5_b_re=s5_b_re, s5_b_im=s5_b_im, s5_c_re=s5_c_re, s5_c_im=s5_c_im, s5_d=s5_d,
             norm1_g=norm1_g, norm2_g=norm2_g)
    depth = w_in.shape[0]
    B, T, D = x_prompt.shape
    Bd = x_sample.shape[0]
    assert x_sample.shape[1] == 1
    n_pool = cache_fox_k.shape[1]
    n_pages = page_table.shape[1]
    past = n_pages * PAGE
    tile = min(512, T)
    topk_p = min(TOPK_MAX, T // 4)
    topk_s = min(TOPK_MAX, (past + 1) // 4)
    pt_flat = page_table.reshape(-1).astype(I32)
    page_pre, page_tot = _page_prefix(cache_fox_logf.reshape(depth * n_pool, PAGE * N_HEADS).astype(F32))
    rel = rel_table.astype(F32)
    tz = _t5_tiles(rel)

    n_c = B + Bd
    c_all = jnp.pad(jnp.concatenate([c_prompt, c_sample], axis=0), ((0, (-n_c) % 8), (0, 0)))
    xp = x_prompt.reshape(B * T, D)
    xs = x_sample.reshape(Bd, D)
    rows_p, rows_s = [], []
    for l in range(depth):
        lw = _layer_weights(l, p)
        m = _adaln(c_all, w_ada[l], b_ada[l])
        mods_p = [a[:B, None, :] for a in jnp.split(m, 6, axis=1)]
        mods_s = [a[None, B:n_c, :] for a in jnp.split(m, 6, axis=1)]
        b_f_l = b_f[l].astype(F32)

        def attend_prompt(fq, fk, fk16, fv, fv16, bq, bk, bk16, bv, bv16, iq, small):
            r3 = lambda a: a.reshape(B, T, a.shape[-1])
            small3 = r3(small)
            f_t = jnp.moveaxis(small3[:, :, _FF_LANES], -1, 1).reshape(B * N_HEADS, T)
            logf_t, cum = _gate_cumsum(f_t, jnp.tile(b_f_l, B).reshape(B * N_HEADS, 1))
            logf = jnp.moveaxis(logf_t.reshape(B, N_HEADS, T), 1, -1)
            o_a = _flash("fox", r3(fq), r3(fk16), r3(fv16), cum.reshape(B, N_HEADS, T), tile=tile)
            ki_t = jnp.swapaxes(small3[:, :, _IK_LANES], 1, 2).astype(BF16)
            mask = _dsa_select(r3(iq), small3, ki_t, topk_p)
            o_b = _flash("dsa", r3(bq), r3(bk16), r3(bv16), mask, tz, tile=tile)
            return o_a.reshape(B * T, W_ATT), o_b.reshape(B * T, W_ATT), logf

        def attend_sample(fq, fk, fk16, fv, fv16, bq, bk, bk16, bv, bv16, iq, small, l=l):
            h3 = lambda a: a.reshape(Bd, N_HEADS, HEAD_DIM)
            logf = _logsig(small[:, _FF_LANES], b_f_l.reshape(1, N_HEADS))
            o_a = _fox_sample(l, pt_flat, h3(fq), h3(fk), h3(fv), logf.reshape(Bd, N_HEADS, 1),
                              cache_fox_k, cache_fox_v, page_pre, page_tot)
            qi3 = iq.reshape(Bd, H_IDX, D_IDX)
            w3 = small[:, _IW_LANES].reshape(Bd, H_IDX, 1)
            sc = _idx_scores_paged(l, pt_flat, qi3, w3, cache_dsa_idx_k)
            ki_new = jnp.pad(small[:, _IK_LANES].reshape(Bd, 1, D_IDX), ((0, 0), (0, LANES - 1), (0, 0)))
            sc_new = _idx_scores_new(qi3, w3, ki_new)
            idx, sb, nb = _dsa_sample_select(sc.reshape(Bd, past), sc_new.reshape(Bd, LANES), rel, topk_s)
            o_b = _dsa_sample_attend(l, idx.reshape(-1), pt_flat, h3(bq), h3(bk), h3(bv),
                                     sb.reshape(Bd, 1, topk_s * N_HEADS), nb.reshape(Bd, N_HEADS, 1),
                                     cache_dsa_k, cache_dsa_v, topk_s)
            return (o_a.reshape(Bd, W_ATT).astype(BF16), o_b.reshape(Bd, W_ATT).astype(BF16), logf)

        zero_state = jnp.zeros((B,) + state_s5_re.shape[2:], F32)
        xp, rp = _layer(xp, mods_p, T, T, lw, zero_state, zero_state, attend_prompt)
        xs, rs = _layer(xs, mods_s, Bd, 1, lw, state_s5_re[l], state_s5_im[l], attend_sample)
        rows_p.append(rp)
        rows_s.append(rs)

    y_prompt = _norm(xp, final_norm_g).reshape(B, T, D)
    y_sample = _norm(xs, final_norm_g).reshape(Bd, 1, D)

    def leaves(rows, nb, nt):
        fk, fv, fl, bk, bv, ik, sr, si = [jnp.stack(a) for a in zip(*rows)]
        hd = (depth, nb, nt, N_HEADS, HEAD_DIM)
        return (fk.reshape(hd), fv.reshape(hd), fl.reshape(depth, nb, nt, N_HEADS), bk.reshape(hd),
                bv.reshape(hd), ik.reshape(depth, nb, nt, D_IDX), sr, si)

    return (y_prompt, y_sample) + leaves(rows_p, B, T) + leaves(rows_s, Bd, 1)
```

```python
import functools
import math

import jax
import jax.numpy as jnp
from jax import lax
from jax.experimental import pallas as pl
from jax.experimental.pallas import tpu as pltpu

F32 = jnp.float32
BF16 = jnp.bfloat16
I32 = jnp.int32

N_HEADS = 8
HEAD_DIM = 128
H_IDX = 16
D_IDX = 64
TOPK_MAX = 256
GROUP = 16
N_STATE = 64
N_BUCKETS = 32
MAX_DIST = 128
N_KEYS = 128
P_HEADS = 8
P_TOPK = 16
PAGE = 128
EPS = 1e-6

V7X_VMEM_LIMIT_BYTES = 56 * 1024 * 1024
LANES = 128
NEG = -1e30
LOG2E = math.log2(math.e)
INT_MIN = -2 ** 31

_PAIRS = sorted([(i, j) for i in range(P_TOPK) for j in range(P_TOPK) if (i + 1) * (j + 1) <= P_TOPK],
                key=lambda p: p[0] * P_TOPK + p[1])
_N_CAND = 56


def _params(*sem):
    return pltpu.CompilerParams(dimension_semantics=sem, vmem_limit_bytes=V7X_VMEM_LIMIT_BYTES)


def _gelu(x):
    return 0.5 * x * (1.0 + jnp.tanh(math.sqrt(2.0 / math.pi) * (x + 0.044715 * (x * x * x))))


def _sigmoid(x):
    return 1.0 / (1.0 + jnp.exp(-x))


def _mm_body(*refs, n_extra, epilogue):
    a_ref, b_ref = refs[0], refs[1]
    extra = refs[2:2 + n_extra]
    outs = refs[2 + n_extra:]
    acc = jnp.dot(a_ref[...].astype(BF16), b_ref[...].astype(BF16), preferred_element_type=F32)
    vals = epilogue(acc, *[e[...] for e in extra]) if epilogue is not None else (acc,) * len(outs)
    for o, v in zip(outs, vals):
        o[...] = v.astype(o.dtype)


def _mm(a, b, out_dtypes, *, tm, tn, order="mn", epilogue=None, extras=(), name="mm"):
    M, K = a.shape
    N = b.shape[1]
    tm, tn = min(tm, M), min(tn, N)
    assert M % tm == 0 and N % tn == 0, (M, N, tm, tn)
    if order == "mn":
        grid = (M // tm, N // tn)
        ij = lambda g0, g1: (g0, g1)
    else:
        grid = (N // tn, M // tm)
        ij = lambda g0, g1: (g1, g0)
    in_specs = [pl.BlockSpec((tm, K), lambda g0, g1: (ij(g0, g1)[0], 0)),
                pl.BlockSpec((K, tn), lambda g0, g1: (0, ij(g0, g1)[1]))]
    args = [a, b]
    for arr, bshape, imap in extras:
        in_specs.append(pl.BlockSpec(bshape, lambda g0, g1, imap=imap: imap(*ij(g0, g1))))
        args.append(arr)
    out_specs = [pl.BlockSpec((tm, tn), lambda g0, g1: ij(g0, g1)) for _ in out_dtypes]
    out_shape = [jax.ShapeDtypeStruct((M, N), dt) for dt in out_dtypes]
    return pl.pallas_call(
        functools.partial(_mm_body, n_extra=len(extras), epilogue=epilogue),
        name=name, grid=grid, in_specs=in_specs, out_specs=out_specs, out_shape=out_shape,
        compiler_params=_params("parallel", "parallel"),
    )(*args)


def _mod_extra(mod, rows_per_group, tm, tn):
    r = mod.shape[1]
    return (mod, (None, r, tn), lambda i, j: ((i * tm) // rows_per_group, 0, j))


def _adaln_body(c_ref, w_ref, b_ref, o_ref):
    c = c_ref[...]
    s = c * _sigmoid(c)
    o_ref[...] = jnp.dot(s.astype(BF16), w_ref[...].astype(BF16), preferred_element_type=F32) + b_ref[...]


def _adaln(c, w_ada, b_ada, tn=1024):
    R, D = c.shape
    N = w_ada.shape[1]
    return pl.pallas_call(
        _adaln_body, name="adaln", grid=(N // tn,),
        in_specs=[pl.BlockSpec((R, D), lambda j: (0, 0)),
                  pl.BlockSpec((D, tn), lambda j: (0, j)),
                  pl.BlockSpec((1, tn), lambda j: (0, j))],
        out_specs=pl.BlockSpec((R, tn), lambda j: (0, j)),
        out_shape=jax.ShapeDtypeStruct((R, N), F32),
        compiler_params=_params("parallel"),
    )(c, w_ada, b_ada.reshape(1, N))


def _norm_mod_body(x_ref, g_ref, sc_ref, sh_ref, o_ref):
    x = x_ref[...]
    y = x * lax.rsqrt(jnp.mean(x * x, axis=-1, keepdims=True) + EPS) * g_ref[...]
    o_ref[...] = (y * (1.0 + sc_ref[...]) + sh_ref[...]).astype(o_ref.dtype)


def _norm_mod(x, g, sc, sh, rows_per_group, tm=512):
    M, D = x.shape
    tm = min(tm, M)
    r = sc.shape[1]
    mod_spec = pl.BlockSpec((None, r, D), lambda i: ((i * tm) // rows_per_group, 0, 0))
    return pl.pallas_call(
        _norm_mod_body, name="norm_mod", grid=(M // tm,),
        in_specs=[pl.BlockSpec((tm, D), lambda i: (i, 0)), pl.BlockSpec((1, D), lambda i: (0, 0)),
                  mod_spec, mod_spec],
        out_specs=pl.BlockSpec((tm, D), lambda i: (i, 0)),
        out_shape=jax.ShapeDtypeStruct((M, D), BF16),
        compiler_params=_params("parallel"),
    )(x, g.reshape(1, D), sc, sh)


def _norm_body(x_ref, g_ref, o_ref):
    x = x_ref[...]
    o_ref[...] = x * lax.rsqrt(jnp.mean(x * x, axis=-1, keepdims=True) + EPS) * g_ref[...]


def _norm(x, g, tm=512):
    M, D = x.shape
    tm = min(tm, M)
    return pl.pallas_call(
        _norm_body, name="final_norm", grid=(M // tm,),
        in_specs=[pl.BlockSpec((tm, D), lambda i: (i, 0)), pl.BlockSpec((1, D), lambda i: (0, 0))],
        out_specs=pl.BlockSpec((tm, D), lambda i: (i, 0)),
        out_shape=jax.ShapeDtypeStruct((M, D), F32),
        compiler_params=_params("parallel"),
    )(x, g.reshape(1, D))


def _log_sigmoid(z):
    return jnp.minimum(z, 0.0) - jnp.log(1.0 + jnp.exp(-jnp.abs(z)))


def _gate_body(f_ref, b_ref, logf_ref, cum_ref):
    logf = _log_sigmoid(f_ref[...] + b_ref[...])
    logf_ref[...] = logf
    T = logf.shape[1]
    lane = lax.broadcasted_iota(I32, logf.shape, 1)
    x = logf
    sh = 1
    while sh < T:
        x = x + jnp.where(lane >= sh, pltpu.roll(x, sh, axis=1), 0.0)
        sh *= 2
    cum_ref[...] = x


def _gate_cumsum(f_t, b_col):
    R, T = f_t.shape
    return pl.pallas_call(
        _gate_body, name="fox_gate",
        out_shape=[jax.ShapeDtypeStruct((R, T), F32), jax.ShapeDtypeStruct((R, T), F32)],
        compiler_params=pltpu.CompilerParams(vmem_limit_bytes=V7X_VMEM_LIMIT_BYTES),
    )(f_t, b_col)


def _flash_body(*refs, mode, tile, scale):
    if mode == "fox":
        q_ref, k_ref, v_ref, fk_ref, o_ref, m_sc, acc_sc = refs
    else:
        q_ref, k_ref, v_ref, mask_ref, tz_ref, o_ref, m_sc, acc_sc = refs
    qi = pl.program_id(1)
    ki = pl.program_id(2)

    @pl.when(ki == 0)
    def _():
        m_sc[...] = jnp.full(m_sc.shape, -jnp.inf, F32)
        acc_sc[...] = jnp.zeros(acc_sc.shape, F32)

    nsub = tile // LANES

    def t5_bias(h, near):
        zero = jnp.zeros((LANES, LANES), F32)
        if near == "diag":
            pick = lambda a, b: tz_ref[h, 0] if a == b else (tz_ref[h, 1] if a == b + 1 else zero)
        else:
            pick = lambda a, b: tz_ref[h, 1] if (a == 0 and b == nsub - 1) else zero
        rows = [jnp.concatenate([pick(a, b) for b in range(nsub)], axis=1) if nsub > 1 else pick(a, 0)
                for a in range(nsub)]
        return jnp.concatenate(rows, axis=0) if nsub > 1 else rows[0]

    def step(near):
        if mode == "fox":
            if near == "diag":
                row = lax.broadcasted_iota(I32, (tile, tile), 0)
                col = lax.broadcasted_iota(I32, (tile, tile), 1)
                keep = row >= col
        else:
            shared = mask_ref[0]
        ones = jnp.ones((tile, HEAD_DIM), BF16)
        for h in range(N_HEADS):
            sl = slice(h * HEAD_DIM, (h + 1) * HEAD_DIM)
            s = lax.dot_general(q_ref[0, :, sl], k_ref[0, :, sl], (((1,), (1,)), ((), ())),
                                preferred_element_type=F32) * (scale * LOG2E)
            if mode == "fox":
                s = s - fk_ref[0, h:h + 1, :]
                if near == "diag":
                    s = jnp.where(keep, s, NEG)
            else:
                s = s + shared
                if near != "far":
                    s = s + t5_bias(h, near)
            m_prev = m_sc[h]
            m_new = jnp.maximum(m_prev, jnp.max(s, axis=1, keepdims=True))
            alpha = jnp.exp2(m_prev - m_new)
            p = jnp.exp2(s - jnp.concatenate([m_new] * nsub, axis=1))
            pv = jnp.dot(p.astype(BF16), jnp.concatenate([v_ref[0, :, sl], ones], axis=1),
                         preferred_element_type=F32)
            acc_sc[h] = jnp.concatenate([alpha] * (2 * HEAD_DIM // LANES), axis=1) * acc_sc[h] + pv
            m_sc[h] = m_new

    if mode == "fox":
        pl.when(ki < qi)(lambda: step("far"))
        pl.when(ki == qi)(lambda: step("diag"))
    else:
        pl.when(ki < qi - 1)(lambda: step("far"))
        pl.when(ki == qi - 1)(lambda: step("next"))
        pl.when(ki == qi)(lambda: step("diag"))

    @pl.when(ki == qi)
    def _():
        for h in range(N_HEADS):
            sl = slice(h * HEAD_DIM, (h + 1) * HEAD_DIM)
            o_ref[0, :, sl] = (acc_sc[h, :, :HEAD_DIM] / acc_sc[h, :, HEAD_DIM:]).astype(o_ref.dtype)


def _flash(mode, q, k, v, *side, tile):
    B, T, W = q.shape
    nt = T // tile
    qspec = pl.BlockSpec((1, tile, W), lambda b, qi, ki: (b, qi, 0))
    kspec = pl.BlockSpec((1, tile, W), lambda b, qi, ki: (b, jnp.minimum(ki, qi), 0))
    if mode == "fox":
        side_specs = [pl.BlockSpec((1, N_HEADS, tile), lambda b, qi, ki: (b, 0, jnp.minimum(ki, qi)))]
    else:
        side_specs = [pl.BlockSpec((1, tile, tile), lambda b, qi, ki: (b, qi, jnp.minimum(ki, qi))),
                      pl.BlockSpec((N_HEADS, 2, LANES, LANES), lambda b, qi, ki: (0, 0, 0, 0))]
    return pl.pallas_call(
        functools.partial(_flash_body, mode=mode, tile=tile, scale=HEAD_DIM ** -0.5),
        name="flash_" + mode, grid=(B, nt, nt),
        in_specs=[qspec, kspec, kspec] + side_specs,
        out_specs=pl.BlockSpec((1, tile, W), lambda b, qi, ki: (b, qi, 0)),
        out_shape=jax.ShapeDtypeStruct((B, T, W), BF16),
        scratch_shapes=[pltpu.VMEM((N_HEADS, tile, LANES), F32), pltpu.VMEM((N_HEADS, tile, 2 * HEAD_DIM), F32)],
        compiler_params=_params("parallel", "parallel", "arbitrary"),
    )(q, k, v, *side)


def _sortable_key(x):
    bits = pltpu.bitcast(x, I32)
    return jnp.where(bits < 0, bits ^ jnp.int32(0x7FFFFFFF), bits)


def _kth_largest_key(count_ge, shape, k):
    def body(b, cur):
        cand = cur | lax.shift_left(jnp.int32(1), 31 - b)
        cnt = count_ge(cand ^ jnp.int32(INT_MIN))
        return jnp.where(cnt >= k, cand, cur)
    cur = lax.fori_loop(0, 32, body, jnp.zeros(shape, I32))
    return cur ^ jnp.int32(INT_MIN)


def _dsa_select_body(qi_ref, small_ref, kit_ref, mask_ref, key_sc, cut_sc, *, tq, cw, bw, T, topk):
    q0 = pl.program_id(1) * tq
    w = small_ref[0][:, N_HEADS:N_HEADS + H_IDX] * (H_IDX ** -0.5 * D_IDX ** -0.5)
    bucket = (q0 + tq + bw - 1) // bw
    n_valid = bucket * (bw // cw)
    row_c = q0 + lax.broadcasted_iota(I32, (tq, cw), 0)
    lane_c = lax.broadcasted_iota(I32, (tq, cw), 1)

    def score_chunk(c, carry):
        c0 = pl.multiple_of(c * cw, cw)
        kt = kit_ref[0, :, pl.ds(c0, cw)]
        acc = jnp.zeros((tq, cw), F32)
        for h in range(H_IDX):
            d = jnp.dot(qi_ref[0, :, h * D_IDX:(h + 1) * D_IDX], kt, preferred_element_type=F32)
            acc = acc + w[:, h:h + 1] * jnp.maximum(d, 0.0)
        key_sc[:, pl.ds(c0, cw)] = jnp.where(c0 + lane_c <= row_c, _sortable_key(acc), jnp.int32(INT_MIN))
        return carry

    lax.fori_loop(0, n_valid, score_chunk, 0)

    lane = lax.broadcasted_iota(I32, (tq, LANES), 1)
    row = q0 + lax.broadcasted_iota(I32, (tq, LANES), 0)
    one = lambda pred: jnp.where(pred, 1, 0)

    def select(n_lt):
        def count(ind):
            part = jnp.zeros((tq, LANES), I32)
            for c in range(n_lt):
                part = part + ind(key_sc[:, c * LANES:(c + 1) * LANES], c * LANES + lane)
            return jnp.sum(part, axis=1, keepdims=True)

        thr = _kth_largest_key(lambda t: count(lambda k, col: one(k >= t)), (tq, 1), topk)
        n_gt = count(lambda k, col: one(k > thr))
        n_ge = count(lambda k, col: one(k >= thr))
        excess = jnp.where(thr > jnp.int32(INT_MIN), n_ge - topk, 0)
        cut_sc[...] = jnp.full((tq, 1), T, I32)

        @pl.when(jnp.max(excess) > 0)
        def _():
            need = topk - n_gt
            nbits = max(1, (T - 1).bit_length())

            def body(b, cut):
                cand = cut | lax.shift_left(jnp.int32(1), nbits - 1 - b)
                cnt = count(lambda k, col: jnp.where(k == thr, one(col < cand), 0))
                return jnp.where(cnt < need, cand, cut)
            cut = lax.fori_loop(0, nbits, body, jnp.zeros((tq, 1), I32))
            cut_sc[...] = jnp.where(excess > 0, cut, T)

        cut = cut_sc[...]
        for c in range(n_lt):
            k = key_sc[:, c * LANES:(c + 1) * LANES]
            col = c * LANES + lane
            val = jnp.where(k > thr, 0.0, jnp.where(k == thr, jnp.where(col <= cut, 0.0, NEG), NEG))
            mask_ref[0, :, c * LANES:(c + 1) * LANES] = jnp.where(col <= row, val, NEG)
        if n_lt * LANES < T:
            mask_ref[0, :, n_lt * LANES:] = jnp.full((tq, T - n_lt * LANES), NEG, F32)

    for k in range(1, T // bw + 1):
        pl.when(bucket == k)(functools.partial(select, k * bw // LANES))


def _dsa_select(qi, small, ki_t, topk, tq=256, cw=256, bw=512):
    B, T, _ = qi.shape
    tq, cw, bw = min(tq, T), min(cw, T), min(bw, T)
    assert bw % cw == 0 and T % bw == 0
    return pl.pallas_call(
        functools.partial(_dsa_select_body, tq=tq, cw=cw, bw=bw, T=T, topk=topk),
        name="dsa_select", grid=(B, T // tq),
        in_specs=[pl.BlockSpec((1, tq, H_IDX * D_IDX), lambda b, i: (b, i, 0)),
                  pl.BlockSpec((1, tq, LANES), lambda b, i: (b, i, 0)),
                  pl.BlockSpec((1, D_IDX, T), lambda b, i: (b, 0, 0))],
        out_specs=pl.BlockSpec((1, tq, T), lambda b, i: (b, i, 0)),
        out_shape=jax.ShapeDtypeStruct((B, T, T), F32),
        scratch_shapes=[pltpu.VMEM((tq, T), I32), pltpu.VMEM((tq, 1), I32)],
        compiler_params=_params("parallel", "parallel"),
    )(qi, small, ki_t)


def _t5_bucket(dist):
    n = jnp.maximum(dist, 0)
    max_exact = N_BUCKETS // 2
    nf = jnp.maximum(n, 1).astype(F32)
    large = max_exact + (jnp.log(nf / max_exact) / math.log(MAX_DIST / max_exact)
                         * (N_BUCKETS - max_exact)).astype(I32)
    large = jnp.minimum(large, N_BUCKETS - 1)
    return jnp.where(n < max_exact, n, large)


def _t5_tiles(rel_table):
    assert LANES >= MAX_DIST
    i = jnp.arange(LANES)
    bucket = _t5_bucket((jnp.arange(2) * LANES)[:, None, None] + i[None, :, None] - i[None, None, :])
    rel = rel_table.astype(F32)
    tz = jnp.zeros((rel.shape[1],) + bucket.shape, F32)
    for k in range(N_BUCKETS - 1):
        tz = tz + jnp.where(bucket[None] == k, (rel[k] - rel[N_BUCKETS - 1])[:, None, None, None], 0.0)
    return tz * LOG2E


def _s5_disc_body(lr_ref, li_ref, ldt_ref, br_ref, bi_ref, abr_ref, abi_ref, bbr_ref, bbi_ref):
    lr, li = lr_ref[...], li_ref[...]
    dt = jnp.exp(ldt_ref[...])
    mag = jnp.exp(lr * dt)
    ab_re, ab_im = mag * jnp.cos(li * dt), mag * jnp.sin(li * dt)
    den = lr * lr + li * li
    nr = ab_re - 1.0
    k_re = (nr * lr + ab_im * li) / den
    k_im = (ab_im * lr - nr * li) / den
    br, bi = br_ref[...], bi_ref[...]
    abr_ref[...] = ab_re
    abi_ref[...] = ab_im
    bbr_ref[...] = k_re * br - k_im * bi
    bbi_ref[...] = k_re * bi + k_im * br


def _s5_discretise(lam_re, lam_im, log_dt, b_re, b_im):
    G, N, P = b_re.shape
    rep = lambda a: jnp.broadcast_to(a.reshape(G * N, 1), (G * N, P))
    ldt = jnp.broadcast_to(log_dt.reshape(G, 1, 1), (G, N, P)).reshape(G * N, P)
    shp = jax.ShapeDtypeStruct((G * N, P), F32)
    abr, abi, bbr, bbi = pl.pallas_call(
        _s5_disc_body, name="s5_discretise", out_shape=[shp] * 4,
        compiler_params=pltpu.CompilerParams(vmem_limit_bytes=V7X_VMEM_LIMIT_BYTES),
    )(rep(lam_re), rep(lam_im), ldt, b_re.reshape(G * N, P), b_im.reshape(G * N, P))
    return (abr[:, 0].reshape(G, N), abi[:, 0].reshape(G, N),
            bbr.reshape(G, N, P), bbi.reshape(G, N, P))


def _s5_scan_body(bu_ref, a_ref, x0_ref, xs_ref, xT_ref, carry_sc, *, tc, S, lc):
    c = pl.program_id(1)

    @pl.when(c == 0)
    def _():
        carry_sc[...] = x0_ref[0]

    for j in range(S // lc):
        slr = slice(j * lc, (j + 1) * lc)
        sli = slice(S + j * lc, S + (j + 1) * lc)
        ar, ai = a_ref[:, slr], a_ref[:, sli]

        def step(t, carry, slr=slr, sli=sli, ar=ar, ai=ai):
            xr, xi = carry
            nr = ar * xr - ai * xi + bu_ref[0, pl.ds(t, 1), slr]
            ni = ar * xi + ai * xr + bu_ref[0, pl.ds(t, 1), sli]
            xs_ref[0, pl.ds(t, 1), slr] = nr
            xs_ref[0, pl.ds(t, 1), sli] = ni
            return nr, ni

        xr, xi = lax.fori_loop(0, tc, step, (carry_sc[:, slr], carry_sc[:, sli]), unroll=min(8, tc))
        carry_sc[:, slr] = xr
        carry_sc[:, sli] = xi

    xT_ref[0] = carry_sc[...]


def _s5_scan(bu, a_row, x0, tc=128, lc=1024):
    B, T, S2 = bu.shape
    S = S2 // 2
    tc = min(tc, T)
    return pl.pallas_call(
        functools.partial(_s5_scan_body, tc=tc, S=S, lc=lc),
        name="s5_scan", grid=(B, T // tc),
        in_specs=[pl.BlockSpec((1, tc, S2), lambda b, c: (b, c, 0)),
                  pl.BlockSpec((1, S2), lambda b, c: (0, 0)),
                  pl.BlockSpec((1, 1, S2), lambda b, c: (b, 0, 0))],
        out_specs=[pl.BlockSpec((1, tc, S2), lambda b, c: (b, c, 0)),
                   pl.BlockSpec((1, 1, S2), lambda b, c: (b, 0, 0))],
        out_shape=[jax.ShapeDtypeStruct((B, T, S2), F32), jax.ShapeDtypeStruct((B, 1, S2), F32)],
        scratch_shapes=[pltpu.VMEM((1, S2), F32)],
        compiler_params=_params("parallel", "arbitrary"),
    )(bu, a_row, x0)


def _glu_body(y_ref, wa_ref, wb_ref, o_ref):
    y = y_ref[...]
    a = jnp.dot(y, wa_ref[...], preferred_element_type=F32)
    b = jnp.dot(y, wb_ref[...], preferred_element_type=F32)
    o_ref[...] = (a * _sigmoid(b)).astype(o_ref.dtype)


def _glu(y, wa, wb, tm=512):
    M, K = y.shape
    N = wa.shape[1]
    tm = min(tm, M)
    return pl.pallas_call(
        _glu_body, name="s5_glu", grid=(M // tm,),
        in_specs=[pl.BlockSpec((tm, K), lambda i: (i, 0)), pl.BlockSpec((K, N), lambda i: (0, 0)),
                  pl.BlockSpec((K, N), lambda i: (0, 0))],
        out_specs=pl.BlockSpec((tm, N), lambda i: (i, 0)),
        out_shape=jax.ShapeDtypeStruct((M, N), BF16),
        compiler_params=_params("parallel"),
    )(y, wa, wb)


def _merge_body(oa_ref, ob_ref, oc_ref, wa_ref, wb_ref, wc_ref, ga_ref, gb_ref, gc_ref, o_ref):
    dot = lambda x, w: jnp.dot(x[...], w[...], preferred_element_type=F32)
    m = (_sigmoid(ga_ref[...]) * dot(oa_ref, wa_ref) + _sigmoid(gb_ref[...]) * dot(ob_ref, wb_ref)
         + _sigmoid(gc_ref[...]) * dot(oc_ref, wc_ref))
    o_ref[...] = m.astype(o_ref.dtype)


def _merge(o_a, o_b, o_c, wa, wb, wc, gl, tm=512, tn=512):
    M, K = o_a.shape
    D = wa.shape[1]
    tm = min(tm, M)
    nd = D // tn
    ospec = pl.BlockSpec((tm, K), lambda i, j: (i, 0))
    wspec = pl.BlockSpec((K, tn), lambda i, j: (0, j))
    gspec = lambda g: pl.BlockSpec((tm, tn), lambda i, j, g=g: (i, g * nd + j))
    return pl.pallas_call(
        _merge_body, name="branch_merge", grid=(M // tm, nd),
        in_specs=[ospec, ospec, ospec, wspec, wspec, wspec, gspec(0), gspec(1), gspec(2)],
        out_specs=pl.BlockSpec((tm, tn), lambda i, j: (i, j)),
        out_shape=jax.ShapeDtypeStruct((M, D), BF16),
        compiler_params=_params("parallel", "parallel"),
    )(o_a, o_b, o_c, wa, wb, wc, gl, gl, gl)


def _top16_rows(s, n_rows):
    iota = lax.broadcasted_iota(I32, s.shape, 0)
    rank = jnp.full(s.shape, P_TOPK, I32)
    vals = []
    work = s
    for k in range(P_TOPK):
        m = jnp.max(work, axis=0, keepdims=True)
        idx = jnp.min(jnp.where(work == m, iota, n_rows), axis=0, keepdims=True)
        hit = iota == idx
        rank = jnp.where(hit, k, rank)
        work = jnp.where(hit, -jnp.inf, work)
        vals.append(m)
    return rank, vals


def _peer_select_body(q_ref, keys_ref, a1_ref, n1_ref, a2_ref, r2_ref, cand_sc, sel_sc):
    tt = q_ref.shape[0]
    nt = (((1,), (1,)), ((), ()))
    for h in range(P_HEADS):
        s1 = lax.dot_general(keys_ref[2 * h], q_ref[:, (2 * h) * N_KEYS:(2 * h + 1) * N_KEYS], nt,
                             preferred_element_type=F32)
        s2 = lax.dot_general(keys_ref[2 * h + 1], q_ref[:, (2 * h + 1) * N_KEYS:(2 * h + 2) * N_KEYS], nt,
                             preferred_element_type=F32)
        r1, v1 = _top16_rows(s1, N_KEYS)
        r2, v2 = _top16_rows(s2, N_KEYS)
        for r, (i, j) in enumerate(_PAIRS):
            cand_sc[r:r + 1, :] = v1[i] + v2[j]
        cand_sc[len(_PAIRS):, :] = jnp.full((_N_CAND - len(_PAIRS), tt), -jnp.inf, F32)
        rc, cv = _top16_rows(cand_sc[...], _N_CAND)
        z = jnp.zeros((1, tt), F32)
        for k in range(P_TOPK):
            z = z + jnp.exp(cv[k] - cv[0])
        sel_sc[...] = jnp.where(rc < P_TOPK, 1, 0)
        cnt = [jnp.zeros((1, tt), I32) for _ in range(P_TOPK)]
        for r, (i, j) in enumerate(_PAIRS):
            cnt[i] = cnt[i] + sel_sc[r:r + 1, :]
        n1 = jnp.zeros((N_KEYS, tt), I32)
        for i in range(P_TOPK):
            n1 = jnp.where(r1 == i, cnt[i], n1)
        a1_ref[h] = jnp.exp(s1 - v1[0]) / z
        n1_ref[h] = n1.astype(F32)
        a2_ref[h] = jnp.exp(s2 - v2[0]).astype(a2_ref.dtype)
        r2_ref[h] = r2.astype(F32).astype(r2_ref.dtype)


def _peer_select(q, keys, tt=256):
    M = q.shape[0]
    tt = min(tt, M)
    tab = pl.BlockSpec((P_HEADS, N_KEYS, tt), lambda i: (0, 0, i))
    shp = lambda dt: jax.ShapeDtypeStruct((P_HEADS, N_KEYS, M), dt)
    return pl.pallas_call(
        _peer_select_body, name="peer_select", grid=(M // tt,),
        in_specs=[pl.BlockSpec((tt, q.shape[1]), lambda i: (i, 0)),
                  pl.BlockSpec(keys.shape, lambda i: (0, 0, 0))],
        out_specs=[tab, tab, tab, tab],
        out_shape=[shp(F32), shp(F32), shp(BF16), shp(BF16)],
        scratch_shapes=[pltpu.VMEM((_N_CAND, tt), F32), pltpu.VMEM((_N_CAND, tt), I32)],
        compiler_params=_params("parallel"),
    )(q, keys)


def _peer_dense_body(h_ref, u_ref, vt_ref, a1_ref, n1_ref, a2_ref, r2_ref, x_ref, gt_ref, o_ref,
                     acc_sc, act_sc, ga_sc, *, te):
    j = pl.program_id(1)

    @pl.when(j == 0)
    def _():
        acc_sc[...] = jnp.zeros(acc_sc.shape, F32)
        act_sc[1] = jnp.zeros(act_sc.shape[1:], F32)

    def step(slot):
        act_sc[slot] = lax.dot_general(u_ref[...], h_ref[...], (((1,), (1,)), ((), ())),
                                       preferred_element_type=F32)
        for r in range(te // N_KEYS):
            rows = slice(r * N_KEYS, (r + 1) * N_KEYS)
            wt = None
            for h in range(P_HEADS):
                a1 = a1_ref[h, r:r + 1, :].astype(BF16)
                n1 = n1_ref[h, r:r + 1, :].astype(BF16)
                term = jnp.where(r2_ref[h] < n1, a1 * a2_ref[h], jnp.zeros((), BF16))
                wt = term if wt is None else wt + term
            ga_sc[rows, :] = wt * _gelu(act_sc[1 - slot, rows, :]).astype(BF16)
        acc_sc[...] += jnp.dot(vt_ref[...], ga_sc[...], preferred_element_type=F32)

    pl.when(lax.rem(j, 2) == 0)(functools.partial(step, 0))
    pl.when(lax.rem(j, 2) == 1)(functools.partial(step, 1))

    @pl.when(j == pl.num_programs(1) - 1)
    def _():
        o_ref[...] = x_ref[...] + gt_ref[...] * acc_sc[...].T


def _peer_dense(h2, u, v_t, tabs, x, gt, rows_per_group, tt=512, te=1024):
    M, D = h2.shape
    E = u.shape[0]
    tt = min(tt, M)
    n_e1 = te // N_KEYS
    nc = E // te
    chunk = lambda j: jnp.clip(j, 0, nc - 1)
    tab1 = pl.BlockSpec((P_HEADS, n_e1, tt), lambda i, j: (0, chunk(j - 1), i))
    tab2 = pl.BlockSpec((P_HEADS, N_KEYS, tt), lambda i, j: (0, 0, i))
    r = gt.shape[1]
    return pl.pallas_call(
        functools.partial(_peer_dense_body, te=te),
        name="peer_dense", grid=(M // tt, nc + 1),
        in_specs=[pl.BlockSpec((tt, D), lambda i, j: (i, 0)),
                  pl.BlockSpec((te, D), lambda i, j: (chunk(j), 0)),
                  pl.BlockSpec((D, te), lambda i, j: (0, chunk(j - 1))),
                  tab1, tab1, tab2, tab2,
                  pl.BlockSpec((tt, D), lambda i, j: (i, 0)),
                  pl.BlockSpec((None, r, D), lambda i, j: ((i * tt) // rows_per_group, 0, 0))],
        out_specs=pl.BlockSpec((tt, D), lambda i, j: (i, 0)),
        out_shape=jax.ShapeDtypeStruct((M, D), F32),
        scratch_shapes=[pltpu.VMEM((D, tt), F32), pltpu.VMEM((2, te, tt), F32), pltpu.VMEM((te, tt), BF16)],
        compiler_params=_params("parallel", "arbitrary"),
    )(h2, u, v_t, *tabs, x, gt)


def _peer(h2, x, gt, rows_per_group, wq, keys, u, v_t):
    (q,) = _mm(h2, wq, [BF16], tm=512, tn=1024, name="peer_query")
    tabs = _peer_select(q, keys)
    return _peer_dense(h2, u, v_t, tabs, x, gt, rows_per_group)


def _s5_prepare(lam_re, lam_im, log_dt, b_re, b_im, c_re, c_im):
    G, N, P = b_re.shape
    S, W = G * N, G * P
    ab_re, ab_im, bb_re, bb_im = _s5_discretise(lam_re, lam_im, log_dt, b_re, b_im)
    eye = jnp.eye(G, dtype=F32)
    blk = lambda m, rows, cols: (jnp.swapaxes(m, 1, 2)[:, :, None, :] * eye[:, None, :, None]).reshape(rows, cols)
    b_blk = jnp.concatenate([blk(bb_re, W, S), blk(bb_im, W, S)], axis=1).astype(BF16)
    c_blk = jnp.concatenate([blk(c_re.astype(F32), S, W), -blk(c_im.astype(F32), S, W)], axis=0).astype(BF16)
    a_row = jnp.concatenate([ab_re.reshape(1, S), ab_im.reshape(1, S)], axis=1)
    return a_row, b_blk, c_blk


def _s5_in_body(u_ref, b_ref, o_ref):
    o_ref[...] = jnp.dot(u_ref[...].astype(BF16), b_ref[...], preferred_element_type=F32)


def _s5_out_body(xr_ref, xi_ref, cr_ref, ci_ref, u_ref, d_ref, o_ref):
    y = (jnp.dot(xr_ref[...].astype(BF16), cr_ref[...], preferred_element_type=F32)
         + jnp.dot(xi_ref[...].astype(BF16), ci_ref[...], preferred_element_type=F32)
         + d_ref[...] * u_ref[...])
    o_ref[...] = _gelu(y).astype(o_ref.dtype)


def _s5(u, T, x0_re, x0_im, prep, d_skip):
    M, W = u.shape
    B = M // T
    G, N = x0_re.shape[1:]
    S = G * N
    a_row, b_blk, c_blk = prep
    tm = min(512, M)
    cg = LANES // (W // G)
    ks = cg * N
    n_ct = W // LANES
    bu = pl.pallas_call(
        _s5_in_body, name="s5_input", grid=(M // tm, 2 * n_ct),
        in_specs=[pl.BlockSpec((tm, LANES), lambda i, j: (i, j % n_ct)),
                  pl.BlockSpec((LANES, ks), lambda i, j: (j % n_ct, j))],
        out_specs=pl.BlockSpec((tm, ks), lambda i, j: (i, j)),
        out_shape=jax.ShapeDtypeStruct((M, 2 * S), F32),
        compiler_params=_params("parallel", "parallel"),
    )(u, b_blk)
    x0 = jnp.concatenate([x0_re.reshape(B, 1, S), x0_im.reshape(B, 1, S)], axis=2).astype(F32)
    xs, x_last = _s5_scan(bu.reshape(B, T, 2 * S), a_row, x0)
    xs2 = xs.reshape(M, 2 * S)
    yg = pl.pallas_call(
        _s5_out_body, name="s5_output", grid=(M // tm, n_ct),
        in_specs=[pl.BlockSpec((tm, ks), lambda i, j: (i, j)),
                  pl.BlockSpec((tm, ks), lambda i, j: (i, n_ct + j)),
                  pl.BlockSpec((ks, LANES), lambda i, j: (j, j)),
                  pl.BlockSpec((ks, LANES), lambda i, j: (n_ct + j, j)),
                  pl.BlockSpec((tm, LANES), lambda i, j: (i, j)),
                  pl.BlockSpec((1, LANES), lambda i, j: (0, j))],
        out_specs=pl.BlockSpec((tm, LANES), lambda i, j: (i, j)),
        out_shape=jax.ShapeDtypeStruct((M, W), BF16),
        compiler_params=_params("parallel", "parallel"),
    )(xs2, xs2, c_blk, c_blk, u, d_skip.reshape(1, W).astype(F32))
    return yg, x_last[:, 0, :S].reshape(B, G, N), x_last[:, 0, S:].reshape(B, G, N)


def _logsig_body(f_ref, b_ref, o_ref):
    o_ref[...] = _log_sigmoid(f_ref[...] + b_ref[...])


def _logsig(f, b_row):
    return pl.pallas_call(_logsig_body, out_shape=jax.ShapeDtypeStruct(f.shape, F32))(f, b_row)


def _head_lane_select(n_lanes):
    lane = lax.broadcasted_iota(I32, (N_HEADS, n_lanes), 1)
    head = lax.broadcasted_iota(I32, (N_HEADS, n_lanes), 0)
    return (lane & (N_HEADS - 1)) == head, lane


def _page_prefix_body(lf_ref, pre_ref, tot_ref):
    lf = lf_ref[...]
    n_lanes = lf.shape[1]
    lane = lax.broadcasted_iota(I32, lf.shape, 1)
    pre, tot = lf, lf
    sh = N_HEADS
    while sh < n_lanes:
        pre = pre + jnp.where(lane >= sh, pltpu.roll(pre, sh, axis=1), 0.0)
        tot = tot + pltpu.roll(tot, sh, axis=1)
        sh *= 2
    pre_ref[...] = pre
    tot_ref[...] = tot


def _page_prefix(logf_rows):
    R, n_lanes = logf_rows.shape
    tr = math.gcd(R, 256)
    spec = pl.BlockSpec((tr, n_lanes), lambda i: (i, 0))
    return pl.pallas_call(
        _page_prefix_body, name="fox_page_prefix", grid=(R // tr,), in_specs=[spec], out_specs=[spec, spec],
        out_shape=[jax.ShapeDtypeStruct((R, n_lanes), F32)] * 2,
        compiler_params=_params("parallel"),
    )(logf_rows)


def _fox_sample_body(pt_ref, q_ref, kn_ref, vn_ref, lfn_ref, k_hbm, v_hbm, pre_hbm, tot_hbm, o_ref,
                     kbuf, vbuf, pbuf, tbuf, sems, m_sc, l_sc, acc_sc, carry_sc,
                     *, layer, n_pool, n_pages, pp, scale):
    g = pl.program_id(1)
    ng = pl.num_programs(1)
    step = pl.program_id(0) * ng + g
    n_steps = pl.num_programs(0) * ng
    slot = lax.rem(step, 2)
    page_lanes = PAGE * N_HEADS
    n_lanes = pp * page_lanes
    nt = (((1,), (1,)), ((), ()))

    def copies(st, sl):
        first = (st // ng) * n_pages + lax.rem(st, ng) * pp
        out = []
        for k in range(pp):
            pg = pt_ref[first + k]
            row = layer * n_pool + pg
            out += [pltpu.make_async_copy(k_hbm.at[layer, pg], kbuf.at[sl, pl.ds(k * PAGE, PAGE)], sems.at[0, sl]),
                    pltpu.make_async_copy(v_hbm.at[layer, pg], vbuf.at[sl, pl.ds(k * PAGE, PAGE)], sems.at[1, sl]),
                    pltpu.make_async_copy(pre_hbm.at[pl.ds(row, 1)], pbuf.at[sl, pl.ds(k, 1)], sems.at[2, sl]),
                    pltpu.make_async_copy(tot_hbm.at[pl.ds(row, 1)], tbuf.at[sl, pl.ds(k, 1)], sems.at[3, sl])]
        return out

    @pl.when(step == 0)
    def _():
        for c in copies(step, slot):
            c.start()

    @pl.when(step + 1 < n_steps)
    def _():
        for c in copies(step + 1, 1 - slot):
            c.start()

    @pl.when(g == 0)
    def _():
        m_sc[...] = jnp.full(m_sc.shape, -jnp.inf, F32)
        l_sc[...] = jnp.zeros(l_sc.shape, F32)
        acc_sc[...] = jnp.zeros(acc_sc.shape, F32)
        carry_sc[...] = jnp.zeros(carry_sc.shape, F32)

    for c in copies(step, slot):
        c.wait()

    hsel, lane8 = _head_lane_select(n_lanes)
    q = q_ref[0]
    kp = kbuf[slot].reshape(n_lanes, HEAD_DIM).astype(BF16)
    s = lax.dot_general(q, kp, nt, preferred_element_type=F32) * scale
    carry = carry_sc[...]
    f_pages = []
    for k in range(pp):
        f_pages.append(carry + pbuf[slot, k:k + 1, :])
        carry = carry + tbuf[slot, k:k + 1, :]
    carry_sc[...] = carry
    f_k = jnp.concatenate(f_pages, axis=1) if pp > 1 else f_pages[0]
    s = jnp.where(hsel, s - f_k, NEG)
    m_prev = m_sc[...]
    m_new = jnp.maximum(m_prev, jnp.max(s, axis=1, keepdims=True))
    alpha = jnp.exp(m_prev - m_new)
    pr = jnp.exp(s - m_new)
    l_sc[...] = alpha * l_sc[...] + jnp.sum(pr, axis=1, keepdims=True)
    vp = vbuf[slot].reshape(n_lanes, HEAD_DIM).astype(BF16)
    acc_sc[...] = alpha * acc_sc[...] + jnp.dot(pr.astype(BF16), vp, preferred_element_type=F32)
    m_sc[...] = m_new

    @pl.when(g == ng - 1)
    def _():
        hsel_p, lane_p = _head_lane_select(page_lanes)
        f_col = jnp.sum(jnp.where(hsel_p, jnp.where(lane_p < N_HEADS, carry_sc[...], 0.0), 0.0),
                        axis=1, keepdims=True)
        kn = kn_ref[0].astype(BF16).astype(F32)
        vn = vn_ref[0].astype(BF16).astype(F32)
        s_n = jnp.sum(q.astype(F32) * kn, axis=1, keepdims=True) * scale - (f_col + lfn_ref[0])
        m_prev = m_sc[...]
        m_new = jnp.maximum(m_prev, s_n)
        alpha = jnp.exp(m_prev - m_new)
        p_n = jnp.exp(s_n - m_new)
        l = alpha * l_sc[...] + p_n
        acc = alpha * acc_sc[...] + p_n.astype(BF16).astype(F32) * vn
        o_ref[0] = acc / l


def _fox_sample(layer, pt_flat, q, k_new, v_new, logf_new, cache_k, cache_v, page_pre, page_tot, pp=8):
    Bd = q.shape[0]
    n_pool = cache_k.shape[1]
    n_pages = pt_flat.shape[0] // Bd
    pp = math.gcd(pp, n_pages)
    tok = lambda w: pl.BlockSpec((1, N_HEADS, w), lambda b, g, pt: (b, 0, 0))
    hbm = pl.BlockSpec(memory_space=pl.ANY)
    grid_spec = pltpu.PrefetchScalarGridSpec(
        num_scalar_prefetch=1, grid=(Bd, n_pages // pp),
        in_specs=[tok(HEAD_DIM), tok(HEAD_DIM), tok(HEAD_DIM), tok(1), hbm, hbm, hbm, hbm],
        out_specs=pl.BlockSpec((1, N_HEADS, HEAD_DIM), lambda b, g, pt: (b, 0, 0)),
        scratch_shapes=[pltpu.VMEM((2, pp * PAGE, N_HEADS, HEAD_DIM), F32),
                        pltpu.VMEM((2, pp * PAGE, N_HEADS, HEAD_DIM), F32),
                        pltpu.VMEM((2, pp, PAGE * N_HEADS), F32), pltpu.VMEM((2, pp, PAGE * N_HEADS), F32),
                        pltpu.SemaphoreType.DMA((4, 2)),
                        pltpu.VMEM((N_HEADS, 1), F32), pltpu.VMEM((N_HEADS, 1), F32),
                        pltpu.VMEM((N_HEADS, HEAD_DIM), F32), pltpu.VMEM((1, PAGE * N_HEADS), F32)])
    return pl.pallas_call(
        functools.partial(_fox_sample_body, layer=layer, n_pool=n_pool, n_pages=n_pages, pp=pp,
                          scale=HEAD_DIM ** -0.5),
        name="fox_sample", grid_spec=grid_spec,
        out_shape=jax.ShapeDtypeStruct((Bd, N_HEADS, HEAD_DIM), F32),
        compiler_params=_params("arbitrary", "arbitrary"),
    )(pt_flat, q, k_new, v_new, logf_new, cache_k, cache_v, page_pre, page_tot)


def _idx_scores(qi, w, ki_rows):
    d = lax.dot_general(qi, ki_rows.astype(BF16), (((1,), (1,)), ((), ())), preferred_element_type=F32)
    return jnp.sum(w * (H_IDX ** -0.5 * D_IDX ** -0.5) * jnp.maximum(d, 0.0), axis=0, keepdims=True)


def _idx_scores_body(qi_ref, w_ref, ki_ref, o_ref):
    o_ref[0] = _idx_scores(qi_ref[0], w_ref[0], ki_ref[...])


def _idx_scores_paged_body(pt_ref, qi_ref, w_ref, ki_hbm, o_ref, buf, sem, *, layer, n_pages, chunk):
    b = pl.program_id(0)

    def page_copy(p):
        return pltpu.make_async_copy(ki_hbm.at[layer, pt_ref[b * n_pages + p]],
                                     buf.at[pl.ds(pl.multiple_of(p * PAGE, PAGE), PAGE)], sem.at[0])

    def start(p, c):
        page_copy(p).start()
        return c

    def wait(p, c):
        page_copy(p).wait()
        return c

    lax.fori_loop(0, n_pages, start, 0)
    lax.fori_loop(0, n_pages, wait, 0)
    for c in range(n_pages * PAGE // chunk):
        o_ref[0, :, c * chunk:(c + 1) * chunk] = _idx_scores(qi_ref[0], w_ref[0], buf[c * chunk:(c + 1) * chunk, :])


def _idx_scores_paged(layer, pt_flat, qi, w, cache_ki, chunk=2048):
    Bd = qi.shape[0]
    n_pages = pt_flat.shape[0] // Bd
    chunk = math.gcd(chunk, n_pages * PAGE)
    grid_spec = pltpu.PrefetchScalarGridSpec(
        num_scalar_prefetch=1, grid=(Bd,),
        in_specs=[pl.BlockSpec((1, H_IDX, D_IDX), lambda b, pt: (b, 0, 0)),
                  pl.BlockSpec((1, H_IDX, 1), lambda b, pt: (b, 0, 0)),
                  pl.BlockSpec(memory_space=pl.ANY)],
        out_specs=pl.BlockSpec((1, 1, n_pages * PAGE), lambda b, pt: (b, 0, 0)),
        scratch_shapes=[pltpu.VMEM((n_pages * PAGE, D_IDX), F32), pltpu.SemaphoreType.DMA((1,))])
    return pl.pallas_call(
        functools.partial(_idx_scores_paged_body, layer=layer, n_pages=n_pages, chunk=chunk),
        name="dsa_sample_scores", grid_spec=grid_spec,
        out_shape=jax.ShapeDtypeStruct((Bd, 1, n_pages * PAGE), F32),
        compiler_params=_params("arbitrary"),
    )(pt_flat, qi, w, cache_ki)


def _idx_scores_new(qi, w, ki_rows):
    Bd, R, _ = ki_rows.shape
    return pl.pallas_call(
        _idx_scores_body, grid=(Bd,),
        in_specs=[pl.BlockSpec((1, H_IDX, D_IDX), lambda b: (b, 0, 0)),
                  pl.BlockSpec((1, H_IDX, 1), lambda b: (b, 0, 0)),
                  pl.BlockSpec((None, R, D_IDX), lambda b: (b, 0, 0))],
        out_specs=pl.BlockSpec((1, 1, R), lambda b: (b, 0, 0)),
        out_shape=jax.ShapeDtypeStruct((Bd, 1, R), F32),
        compiler_params=_params("parallel"),
    )(qi, w, ki_rows)


def _dsa_sample_select_body(sc_ref, scn_ref, rel_ref, idx_ref, sb_ref, nb_ref, rs_sc, *, topk, chunk):
    Bd, P = sc_ref.shape
    key_p = _sortable_key(sc_ref[...])
    key_n = _sortable_key(scn_ref[:, 0:1])
    col = lax.broadcasted_iota(I32, (Bd, P), 1)

    one = lambda pred: jnp.where(pred, 1, 0)

    def count(ind_p, ind_n):
        return jnp.sum(ind_p, axis=1, keepdims=True) + ind_n

    thr = _kth_largest_key(lambda t: count(one(key_p >= t), one(key_n >= t)), (Bd, 1), topk)
    n_gt = count(one(key_p > thr), one(key_n > thr))
    n_ge = count(one(key_p >= thr), one(key_n >= thr))
    need = topk - n_gt
    nbits = (P + 1).bit_length()

    def body(b, cut):
        cand = cut | lax.shift_left(jnp.int32(1), nbits - 1 - b)
        cnt = count(jnp.where(key_p == thr, one(col < cand), 0), jnp.where(key_n == thr, one(P < cand), 0))
        return jnp.where(cnt < need, cand, cut)
    cut = lax.fori_loop(0, nbits, body, jnp.zeros((Bd, 1), I32))
    cut = jnp.where(n_ge > topk, cut, jnp.int32(2 ** 30))
    sel_p = jnp.where(key_p > thr, 1, jnp.where(key_p == thr, one(col <= cut), 0))
    sel_n = jnp.where(key_n > thr, 1, jnp.where(key_n == thr, one(P <= cut), 0))
    rank = sel_p
    sh = 1
    while sh < P:
        rank = rank + jnp.where(col >= sh, pltpu.roll(rank, sh, axis=1), 0)
        sh *= 2
    rs_sc[...] = sel_p * rank
    n_past = jnp.sum(sel_p, axis=1, keepdims=True)
    nb_ref[...] = jnp.where(sel_n > 0, rel_ref[0:1, :], NEG)

    slot = lax.broadcasted_iota(I32, (topk, 1), 0)
    ccol = lax.broadcasted_iota(I32, (topk, chunk), 1)
    for b in range(Bd):
        idx = jnp.zeros((topk, 1), I32)
        for c in range(P // chunk):
            rs = rs_sc[b:b + 1, c * chunk:(c + 1) * chunk]
            idx = idx + jnp.sum(jnp.where(rs == slot + 1, ccol + c * chunk, 0), axis=1, keepdims=True)
        idx_ref[b] = idx
        bucket = _t5_bucket(P - idx)
        bias = jnp.zeros((topk, N_HEADS), F32)
        for k in range(N_BUCKETS):
            bias = bias + jnp.where(bucket == k, rel_ref[k:k + 1, :], 0.0)
        sb_ref[b] = jnp.where(slot < n_past[b:b + 1, :], bias, NEG)


def _dsa_sample_select(sc, sc_new, rel_table, topk, chunk=2048):
    Bd, P = sc.shape
    return pl.pallas_call(
        functools.partial(_dsa_sample_select_body, topk=topk, chunk=min(chunk, P)),
        out_shape=[jax.ShapeDtypeStruct((Bd, topk, 1), I32), jax.ShapeDtypeStruct((Bd, topk, N_HEADS), F32),
                   jax.ShapeDtypeStruct((Bd, N_HEADS), F32)],
        scratch_shapes=[pltpu.VMEM((Bd, P), I32)],
        compiler_params=pltpu.CompilerParams(vmem_limit_bytes=V7X_VMEM_LIMIT_BYTES),
    )(sc, sc_new, rel_table)


def _dsa_sample_attend_body(idx_ref, pt_ref, q_ref, kn_ref, vn_ref, sb_ref, nb_ref, kc_hbm, vc_hbm, o_ref,
                            kbuf, vbuf, sems, *, layer, topk, n_pages, scale):
    b = pl.program_id(0)

    def row_copies(j):
        i = idx_ref[b * topk + j]
        pg = pt_ref[b * n_pages + lax.shift_right_logical(i, 7)]
        off = i & (PAGE - 1)
        return (pltpu.make_async_copy(kc_hbm.at[layer, pg, off], kbuf.at[j], sems.at[0]),
                pltpu.make_async_copy(vc_hbm.at[layer, pg, off], vbuf.at[j], sems.at[1]))

    def start(j, c):
        ck, cv = row_copies(j)
        ck.start()
        cv.start()
        return c

    def wait(j, c):
        ck, cv = row_copies(j)
        ck.wait()
        cv.wait()
        return c

    lax.fori_loop(0, topk, start, 0)
    lax.fori_loop(0, topk, wait, 0)

    n_lanes = topk * N_HEADS
    hsel, _ = _head_lane_select(n_lanes)
    q = q_ref[0]
    kb = kbuf[...].reshape(n_lanes, HEAD_DIM).astype(BF16)
    s = lax.dot_general(q, kb, (((1,), (1,)), ((), ())), preferred_element_type=F32) * scale
    s = jnp.where(hsel, s + sb_ref[0], NEG)
    kn = kn_ref[0].astype(BF16).astype(F32)
    vn = vn_ref[0].astype(BF16).astype(F32)
    s_n = jnp.sum(q.astype(F32) * kn, axis=1, keepdims=True) * scale + nb_ref[0]
    m = jnp.maximum(jnp.max(s, axis=1, keepdims=True), s_n)
    p = jnp.exp(s - m)
    p_n = jnp.exp(s_n - m)
    l = jnp.sum(p, axis=1, keepdims=True) + p_n
    vb = vbuf[...].reshape(n_lanes, HEAD_DIM).astype(BF16)
    acc = jnp.dot(p.astype(BF16), vb, preferred_element_type=F32) + p_n.astype(BF16).astype(F32) * vn
    o_ref[0] = acc / l


def _dsa_sample_attend(layer, idx_flat, pt_flat, q, k_new, v_new, slot_bias, new_bias, cache_k, cache_v, topk):
    Bd = q.shape[0]
    n_pages = pt_flat.shape[0] // Bd
    tok = lambda w: pl.BlockSpec((1, N_HEADS, w), lambda b, idx, pt: (b, 0, 0))
    grid_spec = pltpu.PrefetchScalarGridSpec(
        num_scalar_prefetch=2, grid=(Bd,),
        in_specs=[tok(HEAD_DIM), tok(HEAD_DIM), tok(HEAD_DIM),
                  pl.BlockSpec((1, 1, topk * N_HEADS), lambda b, idx, pt: (b, 0, 0)),
                  tok(1),
                  pl.BlockSpec(memory_space=pl.ANY), pl.BlockSpec(memory_space=pl.ANY)],
        out_specs=pl.BlockSpec((1, N_HEADS, HEAD_DIM), lambda b, idx, pt: (b, 0, 0)),
        scratch_shapes=[pltpu.VMEM((topk, N_HEADS, HEAD_DIM), F32), pltpu.VMEM((topk, N_HEADS, HEAD_DIM), F32),
                        pltpu.SemaphoreType.DMA((2,))])
    return pl.pallas_call(
        functools.partial(_dsa_sample_attend_body, layer=layer, topk=topk, n_pages=n_pages,
                          scale=HEAD_DIM ** -0.5),
        grid_spec=grid_spec,
        out_shape=jax.ShapeDtypeStruct((Bd, N_HEADS, HEAD_DIM), F32),
        compiler_params=_params("arbitrary"),
    )(idx_flat, pt_flat, q, k_new, v_new, slot_bias, new_bias, cache_k, cache_v)


W_ATT = N_HEADS * HEAD_DIM
_PROJ_NAMES = ("fq", "fk", "fv", "ff", "bq", "bk", "bv", "iq", "ik", "iw", "su", "gl")
_FF_LANES = slice(0, N_HEADS)
_IW_LANES = slice(N_HEADS, N_HEADS + H_IDX)
_IK_LANES = slice(N_HEADS + H_IDX, N_HEADS + H_IDX + D_IDX)


def _layer_weights(l, p):
    D = p["w_in"].shape[1]
    w_c = p["w_glu"].shape[1]
    sizes = (W_ATT, W_ATT, W_ATT, N_HEADS, W_ATT, W_ATT, W_ATT, H_IDX * D_IDX, D_IDX, H_IDX, w_c, 3 * D)
    w_in = p["w_in"][l]
    cols, off = {}, 0
    for name, n in zip(_PROJ_NAMES, sizes):
        cols[name] = w_in[:, off:off + n]
        off += n
    pad = jnp.zeros((D, LANES - (N_HEADS + H_IDX + D_IDX)), w_in.dtype)
    lw = {n: cols[n].astype(BF16) for n in ("fq", "fk", "fv", "bq", "bk", "bv", "iq", "su", "gl")}
    lw["small"] = jnp.concatenate([cols["ff"], cols["iw"], cols["ik"], pad], axis=1).astype(BF16)
    w_glu, w_br = p["w_glu"][l], p["w_br"][l]
    lw["glu_a"], lw["glu_b"] = w_glu[:, :w_c].astype(BF16), w_glu[:, w_c:].astype(BF16)
    lw["wa"] = w_br[:W_ATT].astype(BF16)
    lw["wb"] = w_br[W_ATT:2 * W_ATT].astype(BF16)
    lw["wc"] = w_br[2 * W_ATT:].astype(BF16)
    lw["w_out"] = p["w_out"][l].astype(BF16)
    lw["wq"] = p["peer_wq"][l].astype(BF16)
    lw["keys"] = p["peer_keys"][l].reshape(2 * P_HEADS, N_KEYS, -1).astype(BF16)
    lw["u"] = p["peer_u"][l].astype(BF16)
    lw["v_t"] = p["peer_v"][l].T.astype(BF16)
    lw["s5"] = _s5_prepare(p["s5_lam_re"][l], p["s5_lam_im"][l], p["s5_log_dt"][l], p["s5_b_re"][l],
                           p["s5_b_im"][l], p["s5_c_re"][l], p["s5_c_im"][l])
    lw["s5_d"] = p["s5_d"][l]
    lw["norm1_g"], lw["norm2_g"] = p["norm1_g"][l], p["norm2_g"][l]
    return lw


def _layer(x, mods, rows_per_group, T, lw, s5_re0, s5_im0, attend):
    M, D = x.shape
    sh1, sc1, gt1, sh2, sc2, gt2 = mods
    tm = min(512, M)
    h = _norm_mod(x, lw["norm1_g"], sc1, sh1, rows_per_group)
    proj = lambda name, dts: _mm(h, lw[name], dts, tm=tm, tn=1024, order="nm", name="proj_" + name)
    (fq,) = proj("fq", [BF16])
    fk, fk16 = proj("fk", [F32, BF16])
    fv, fv16 = proj("fv", [F32, BF16])
    (bq,) = proj("bq", [BF16])
    bk, bk16 = proj("bk", [F32, BF16])
    bv, bv16 = proj("bv", [F32, BF16])
    (iq,) = proj("iq", [BF16])
    (su,) = proj("su", [F32])
    (small,) = proj("small", [F32])
    (gl,) = proj("gl", [F32])
    o_a, o_b, logf = attend(fq, fk, fk16, fv, fv16, bq, bk, bk16, bv, bv16, iq, small)
    yg, s5_re, s5_im = _s5(su, T, s5_re0, s5_im0, lw["s5"], lw["s5_d"])
    o_c = _glu(yg, lw["glu_a"], lw["glu_b"])
    merged = _merge(o_a, o_b, o_c, lw["wa"], lw["wb"], lw["wc"], gl)
    tn = 512
    (x1,) = _mm(merged, lw["w_out"], [F32], tm=tm, tn=tn,
                epilogue=lambda acc, x_, g_: (x_ + g_ * acc,),
                extras=[(x, (tm, tn), lambda i, j: (i, j)), _mod_extra(gt1, rows_per_group, tm, tn)],
                name="out_proj")
    h2 = _norm_mod(x1, lw["norm2_g"], sc2, sh2, rows_per_group)
    if M < LANES:
        padr = lambda a: jnp.pad(a, ((0, LANES - M), (0, 0)))
        gt2p = jnp.pad(gt2, ((0, 0), (0, LANES - M), (0, 0)))
        x2 = _peer(padr(h2), padr(x1), gt2p, LANES, lw["wq"], lw["keys"], lw["u"], lw["v_t"])[:M]
    else:
        x2 = _peer(h2, x1, gt2, rows_per_group, lw["wq"], lw["keys"], lw["u"], lw["v_t"])
    return x2, (fk, fv, logf, bk, bv, small[:, _IK_LANES], s5_re, s5_im)


def kernel(x_prompt, x_sample, cache_fox_k, cache_fox_v, cache_fox_logf, cache_dsa_k, cache_dsa_v,
           cache_dsa_idx_k, state_s5_re, state_s5_im, page_table, c_prompt, c_sample,
           w_ada, b_ada, norm1_g, norm2_g, w_in, b_f, rel_table, s5_lam_re, s5_lam_im, s5_log_dt,
           s5_b_re, s5_b_im, s5_c_re, s5_c_im, s5_d, w_glu, w_br, w_out,
           peer_wq, peer_keys, peer_u, peer_v, final_norm_g):
    p = dict(w_in=w_in, w_glu=w_glu, w_br=w_br, w_out=w_out, peer_wq=peer_wq, peer_keys=peer_keys,
             peer_u=peer_u, peer_v=peer_v, s5_lam_re=s5_lam_re, s5_lam_im=s5_lam_im, s5_log_dt=s5_log_dt,
             s5_b_re=s5_b_re, s5_b_im=s5_b_im, s5_c_re=s5_c_re, s5_c_im=s5_c_im, s5_d=s5_d,
             norm1_g=norm1_g, norm2_g=norm2_g)
    depth = w_in.shape[0]
    B, T, D = x_prompt.shape
    Bd = x_sample.shape[0]
    assert x_sample.shape[1] == 1
    n_pool = cache_fox_k.shape[1]
    n_pages = page_table.shape[1]
    past = n_pages * PAGE
    tile = min(512, T)
    topk_p = min(TOPK_MAX, T // 4)
    topk_s = min(TOPK_MAX, (past + 1) // 4)
    pt_flat = page_table.reshape(-1).astype(I32)
    page_pre, page_tot = _page_prefix(cache_fox_logf.reshape(depth * n_pool, PAGE * N_HEADS).astype(F32))
    rel = rel_table.astype(F32)
    tz = _t5_tiles(rel)

    n_c = B + Bd
    c_all = jnp.pad(jnp.concatenate([c_prompt, c_sample], axis=0), ((0, (-n_c) % 8), (0, 0)))
    xp = x_prompt.reshape(B * T, D)
    xs = x_sample.reshape(Bd, D)
    rows_p, rows_s = [], []
    for l in range(depth):
        lw = _layer_weights(l, p)
        m = _adaln(c_all, w_ada[l], b_ada[l])
        mods_p = [a[:B, None, :] for a in jnp.split(m, 6, axis=1)]
        mods_s = [a[None, B:n_c, :] for a in jnp.split(m, 6, axis=1)]
        b_f_l = b_f[l].astype(F32)

        def attend_prompt(fq, fk, fk16, fv, fv16, bq, bk, bk16, bv, bv16, iq, small):
            r3 = lambda a: a.reshape(B, T, a.shape[-1])
            small3 = r3(small)
            f_t = jnp.moveaxis(small3[:, :, _FF_LANES], -1, 1).reshape(B * N_HEADS, T)
            logf_t, cum = _gate_cumsum(f_t, jnp.tile(b_f_l, B).reshape(B * N_HEADS, 1))
            logf = jnp.moveaxis(logf_t.reshape(B, N_HEADS, T), 1, -1)
            o_a = _flash("fox", r3(fq), r3(fk16), r3(fv16), cum.reshape(B, N_HEADS, T) * LOG2E, tile=tile)
            ki_t = jnp.swapaxes(small3[:, :, _IK_LANES], 1, 2).astype(BF16)
            mask = _dsa_select(r3(iq), small3, ki_t, topk_p)
            o_b = _flash("dsa", r3(bq), r3(bk16), r3(bv16), mask, tz, tile=tile)
            return o_a.reshape(B * T, W_ATT), o_b.reshape(B * T, W_ATT), logf

        def attend_sample(fq, fk, fk16, fv, fv16, bq, bk, bk16, bv, bv16, iq, small, l=l):
            h3 = lambda a: a.reshape(Bd, N_HEADS, HEAD_DIM)
            logf = _logsig(small[:, _FF_LANES], b_f_l.reshape(1, N_HEADS))
            o_a = _fox_sample(l, pt_flat, h3(fq), h3(fk), h3(fv), logf.reshape(Bd, N_HEADS, 1),
                              cache_fox_k, cache_fox_v, page_pre, page_tot)
            qi3 = iq.reshape(Bd, H_IDX, D_IDX)
            w3 = small[:, _IW_LANES].reshape(Bd, H_IDX, 1)
            sc = _idx_scores_paged(l, pt_flat, qi3, w3, cache_dsa_idx_k)
            ki_new = jnp.pad(small[:, _IK_LANES].reshape(Bd, 1, D_IDX), ((0, 0), (0, LANES - 1), (0, 0)))
            sc_new = _idx_scores_new(qi3, w3, ki_new)
            idx, sb, nb = _dsa_sample_select(sc.reshape(Bd, past), sc_new.reshape(Bd, LANES), rel, topk_s)
            o_b = _dsa_sample_attend(l, idx.reshape(-1), pt_flat, h3(bq), h3(bk), h3(bv),
                                     sb.reshape(Bd, 1, topk_s * N_HEADS), nb.reshape(Bd, N_HEADS, 1),
                                     cache_dsa_k, cache_dsa_v, topk_s)
            return (o_a.reshape(Bd, W_ATT).astype(BF16), o_b.reshape(Bd, W_ATT).astype(BF16), logf)

        zero_state = jnp.zeros((B,) + state_s5_re.shape[2:], F32)
        xp, rp = _layer(xp, mods_p, T, T, lw, zero_state, zero_state, attend_prompt)
        xs, rs = _layer(xs, mods_s, Bd, 1, lw, state_s5_re[l], state_s5_im[l], attend_sample)
        rows_p.append(rp)
        rows_s.append(rs)

    y_prompt = _norm(xp, final_norm_g).reshape(B, T, D)
    y_sample = _norm(xs, final_norm_g).reshape(Bd, 1, D)

    def leaves(rows, nb, nt):
        fk, fv, fl, bk, bv, ik, sr, si = [jnp.stack(a) for a in zip(*rows)]
        hd = (depth, nb, nt, N_HEADS, HEAD_DIM)
        return (fk.reshape(hd), fv.reshape(hd), fl.reshape(depth, nb, nt, N_HEADS), bk.reshape(hd),
                bv.reshape(hd), ik.reshape(depth, nb, nt, D_IDX), sr, si)

    return (y_prompt, y_sample) + leaves(rows_p, B, T) + leaves(rows_s, Bd, 1)
```

```python
import functools
import math

import jax
import jax.numpy as jnp
from jax import lax
from jax.experimental import pallas as pl
from jax.experimental.pallas import tpu as pltpu

F32 = jnp.float32
BF16 = jnp.bfloat16
I32 = jnp.int32

N_HEADS = 8
HEAD_DIM = 128
H_IDX = 16
D_IDX = 64
TOPK_MAX = 256
GROUP = 16
N_STATE = 64
N_BUCKETS = 32
MAX_DIST = 128
N_KEYS = 128
P_HEADS = 8
P_TOPK = 16
PAGE = 128
EPS = 1e-6

V7X_VMEM_LIMIT_BYTES = 56 * 1024 * 1024
LANES = 128
NEG = -1e30
LOG2E = math.log2(math.e)
INT_MIN = -2 ** 31

_PAIRS = sorted([(i, j) for i in range(P_TOPK) for j in range(P_TOPK) if (i + 1) * (j + 1) <= P_TOPK],
                key=lambda p: p[0] * P_TOPK + p[1])
_N_CAND = 56


def _params(*sem):
    return pltpu.CompilerParams(dimension_semantics=sem, vmem_limit_bytes=V7X_VMEM_LIMIT_BYTES)


def _gelu(x):
    return 0.5 * x * (1.0 + jnp.tanh(math.sqrt(2.0 / math.pi) * (x + 0.044715 * (x * x * x))))


def _sigmoid(x):
    return 1.0 / (1.0 + jnp.exp(-x))


def _mm_body(*refs, n_extra, epilogue):
    a_ref, b_ref = refs[0], refs[1]
    extra = refs[2:2 + n_extra]
    outs = refs[2 + n_extra:]
    acc = jnp.dot(a_ref[...].astype(BF16), b_ref[...].astype(BF16), preferred_element_type=F32)
    vals = epilogue(acc, *[e[...] for e in extra]) if epilogue is not None else (acc,) * len(outs)
    for o, v in zip(outs, vals):
        o[...] = v.astype(o.dtype)


def _mm(a, b, out_dtypes, *, tm, tn, order="mn", epilogue=None, extras=(), name="mm", cols=None):
    M, K = a.shape
    col0, N = cols if cols is not None else (0, b.shape[1])
    tm, tn = min(tm, M), min(tn, N)
    assert M % tm == 0 and N % tn == 0 and col0 % tn == 0, (M, N, tm, tn, col0)
    jb0 = col0 // tn
    if order == "mn":
        grid = (M // tm, N // tn)
        ij = lambda g0, g1: (g0, g1)
    else:
        grid = (N // tn, M // tm)
        ij = lambda g0, g1: (g1, g0)
    in_specs = [pl.BlockSpec((tm, K), lambda g0, g1: (ij(g0, g1)[0], 0)),
                pl.BlockSpec((K, tn), lambda g0, g1: (0, jb0 + ij(g0, g1)[1]))]
    args = [a, b]
    for arr, bshape, imap in extras:
        in_specs.append(pl.BlockSpec(bshape, lambda g0, g1, imap=imap: imap(*ij(g0, g1))))
        args.append(arr)
    out_specs = [pl.BlockSpec((tm, tn), lambda g0, g1: ij(g0, g1)) for _ in out_dtypes]
    out_shape = [jax.ShapeDtypeStruct((M, N), dt) for dt in out_dtypes]
    return pl.pallas_call(
        functools.partial(_mm_body, n_extra=len(extras), epilogue=epilogue),
        name=name, grid=grid, in_specs=in_specs, out_specs=out_specs, out_shape=out_shape,
        compiler_params=_params("parallel", "parallel"),
    )(*args)


def _mod_extra(mod, rows_per_group, tm, tn):
    r = mod.shape[1]
    return (mod, (None, r, tn), lambda i, j: ((i * tm) // rows_per_group, 0, j))


def _adaln_body(c_ref, w_ref, b_ref, o_ref):
    c = c_ref[...]
    s = c * _sigmoid(c)
    o_ref[...] = jnp.dot(s.astype(BF16), w_ref[...].astype(BF16), preferred_element_type=F32) + b_ref[...]


def _adaln(c, w_ada, b_ada, tn=1024):
    R, D = c.shape
    N = w_ada.shape[1]
    return pl.pallas_call(
        _adaln_body, name="adaln", grid=(N // tn,),
        in_specs=[pl.BlockSpec((R, D), lambda j: (0, 0)),
                  pl.BlockSpec((D, tn), lambda j: (0, j)),
                  pl.BlockSpec((1, tn), lambda j: (0, j))],
        out_specs=pl.BlockSpec((R, tn), lambda j: (0, j)),
        out_shape=jax.ShapeDtypeStruct((R, N), F32),
        compiler_params=_params("parallel"),
    )(c, w_ada, b_ada.reshape(1, N))


def _norm_mod_body(x_ref, g_ref, sc_ref, sh_ref, o_ref):
    x = x_ref[...]
    y = x * lax.rsqrt(jnp.mean(x * x, axis=-1, keepdims=True) + EPS) * g_ref[...]
    o_ref[...] = (y * (1.0 + sc_ref[...]) + sh_ref[...]).astype(o_ref.dtype)


def _norm_mod(x, g, sc, sh, rows_per_group, tm=512):
    M, D = x.shape
    tm = min(tm, M)
    r = sc.shape[1]
    mod_spec = pl.BlockSpec((None, r, D), lambda i: ((i * tm) // rows_per_group, 0, 0))
    return pl.pallas_call(
        _norm_mod_body, name="norm_mod", grid=(M // tm,),
        in_specs=[pl.BlockSpec((tm, D), lambda i: (i, 0)), pl.BlockSpec((1, D), lambda i: (0, 0)),
                  mod_spec, mod_spec],
        out_specs=pl.BlockSpec((tm, D), lambda i: (i, 0)),
        out_shape=jax.ShapeDtypeStruct((M, D), BF16),
        compiler_params=_params("parallel"),
    )(x, g.reshape(1, D), sc, sh)


def _norm_body(x_ref, g_ref, o_ref):
    x = x_ref[...]
    o_ref[...] = x * lax.rsqrt(jnp.mean(x * x, axis=-1, keepdims=True) + EPS) * g_ref[...]


def _norm(x, g, tm=512):
    M, D = x.shape
    tm = min(tm, M)
    return pl.pallas_call(
        _norm_body, name="final_norm", grid=(M // tm,),
        in_specs=[pl.BlockSpec((tm, D), lambda i: (i, 0)), pl.BlockSpec((1, D), lambda i: (0, 0))],
        out_specs=pl.BlockSpec((tm, D), lambda i: (i, 0)),
        out_shape=jax.ShapeDtypeStruct((M, D), F32),
        compiler_params=_params("parallel"),
    )(x, g.reshape(1, D))


def _log_sigmoid(z):
    return jnp.minimum(z, 0.0) - jnp.log(1.0 + jnp.exp(-jnp.abs(z)))


def _gate_body(f_ref, b_ref, logf_ref, cum_ref):
    logf = _log_sigmoid(f_ref[...] + b_ref[...])
    logf_ref[...] = logf
    T = logf.shape[1]
    lane = lax.broadcasted_iota(I32, logf.shape, 1)
    x = logf
    sh = 1
    while sh < T:
        x = x + jnp.where(lane >= sh, pltpu.roll(x, sh, axis=1), 0.0)
        sh *= 2
    cum_ref[...] = x


def _gate_cumsum(f_t, b_col):
    R, T = f_t.shape
    return pl.pallas_call(
        _gate_body, name="fox_gate",
        out_shape=[jax.ShapeDtypeStruct((R, T), F32), jax.ShapeDtypeStruct((R, T), F32)],
        compiler_params=pltpu.CompilerParams(vmem_limit_bytes=V7X_VMEM_LIMIT_BYTES),
    )(f_t, b_col)


def _flash_body(*refs, mode, tile, scale):
    if mode == "fox":
        q_ref, k_ref, v_ref, fk_ref, o_ref, m_sc, acc_sc = refs
    else:
        q_ref, k_ref, v_ref, mask_ref, tz_ref, o_ref, m_sc, acc_sc = refs
    qi = pl.program_id(1)
    ki = pl.program_id(2)

    @pl.when(ki == 0)
    def _():
        m_sc[...] = jnp.full(m_sc.shape, -jnp.inf, F32)
        acc_sc[...] = jnp.zeros(acc_sc.shape, F32)

    nsub = tile // LANES

    def t5_bias(h, near):
        zero = jnp.zeros((LANES, LANES), F32)
        if near == "diag":
            pick = lambda a, b: tz_ref[h, 0] if a == b else (tz_ref[h, 1] if a == b + 1 else zero)
        else:
            pick = lambda a, b: tz_ref[h, 1] if (a == 0 and b == nsub - 1) else zero
        rows = [jnp.concatenate([pick(a, b) for b in range(nsub)], axis=1) if nsub > 1 else pick(a, 0)
                for a in range(nsub)]
        return jnp.concatenate(rows, axis=0) if nsub > 1 else rows[0]

    def step(near):
        if mode == "fox":
            if near == "diag":
                row = lax.broadcasted_iota(I32, (tile, tile), 0)
                col = lax.broadcasted_iota(I32, (tile, tile), 1)
                keep = row >= col
        else:
            shared = mask_ref[0]
        ones = jnp.ones((tile, HEAD_DIM), BF16)
        for h in range(N_HEADS):
            sl = slice(h * HEAD_DIM, (h + 1) * HEAD_DIM)
            s = lax.dot_general(q_ref[0, :, sl], k_ref[0, :, sl], (((1,), (1,)), ((), ())),
                                preferred_element_type=F32) * (scale * LOG2E)
            if mode == "fox":
                s = s - fk_ref[0, h:h + 1, :]
                if near == "diag":
                    s = jnp.where(keep, s, NEG)
            else:
                s = s + shared
                if near != "far":
                    s = s + t5_bias(h, near)
            m_prev = m_sc[h]
            m_new = jnp.maximum(m_prev, jnp.max(s, axis=1, keepdims=True))
            alpha = jnp.exp2(m_prev - m_new)
            p = jnp.exp2(s - jnp.concatenate([m_new] * nsub, axis=1))
            pv = jnp.dot(p.astype(BF16), jnp.concatenate([v_ref[0, :, sl], ones], axis=1),
                         preferred_element_type=F32)
            acc_sc[h] = jnp.concatenate([alpha] * (2 * HEAD_DIM // LANES), axis=1) * acc_sc[h] + pv
            m_sc[h] = m_new

    if mode == "fox":
        pl.when(ki < qi)(lambda: step("far"))
        pl.when(ki == qi)(lambda: step("diag"))
    else:
        pl.when(ki < qi - 1)(lambda: step("far"))
        pl.when(ki == qi - 1)(lambda: step("next"))
        pl.when(ki == qi)(lambda: step("diag"))

    @pl.when(ki == qi)
    def _():
        for h in range(N_HEADS):
            sl = slice(h * HEAD_DIM, (h + 1) * HEAD_DIM)
            o_ref[0, :, sl] = (acc_sc[h, :, :HEAD_DIM] / acc_sc[h, :, HEAD_DIM:]).astype(o_ref.dtype)


def _flash(mode, q, k, v, *side, tile):
    B, T, W = q.shape
    nt = T // tile
    qspec = pl.BlockSpec((1, tile, W), lambda b, qi, ki: (b, qi, 0))
    kspec = pl.BlockSpec((1, tile, W), lambda b, qi, ki: (b, jnp.minimum(ki, qi), 0))
    if mode == "fox":
        side_specs = [pl.BlockSpec((1, N_HEADS, tile), lambda b, qi, ki: (b, 0, jnp.minimum(ki, qi)))]
    else:
        side_specs = [pl.BlockSpec((1, tile, tile), lambda b, qi, ki: (b, qi, jnp.minimum(ki, qi))),
                      pl.BlockSpec((N_HEADS, 2, LANES, LANES), lambda b, qi, ki: (0, 0, 0, 0))]
    return pl.pallas_call(
        functools.partial(_flash_body, mode=mode, tile=tile, scale=HEAD_DIM ** -0.5),
        name="flash_" + mode, grid=(B, nt, nt),
        in_specs=[qspec, kspec, kspec] + side_specs,
        out_specs=pl.BlockSpec((1, tile, W), lambda b, qi, ki: (b, qi, 0)),
        out_shape=jax.ShapeDtypeStruct((B, T, W), BF16),
        scratch_shapes=[pltpu.VMEM((N_HEADS, tile, LANES), F32), pltpu.VMEM((N_HEADS, tile, 2 * HEAD_DIM), F32)],
        compiler_params=_params("parallel", "parallel", "arbitrary"),
    )(q, k, v, *side)


def _sortable_key(x):
    bits = pltpu.bitcast(x, I32)
    return jnp.where(bits < 0, bits ^ jnp.int32(0x7FFFFFFF), bits)


def _kth_largest_key(count_ge, shape, k):
    def body(b, cur):
        cand = cur | lax.shift_left(jnp.int32(1), 31 - b)
        cnt = count_ge(cand ^ jnp.int32(INT_MIN))
        return jnp.where(cnt >= k, cand, cur)
    cur = lax.fori_loop(0, 32, body, jnp.zeros(shape, I32))
    return cur ^ jnp.int32(INT_MIN)


def _dsa_select_body(qi_ref, small_ref, kit_ref, mask_ref, key_sc, cut_sc, *, tq, cw, bw, T, topk):
    q0 = pl.program_id(1) * tq
    w = small_ref[0][:, N_HEADS:N_HEADS + H_IDX] * (H_IDX ** -0.5 * D_IDX ** -0.5)
    bucket = (q0 + tq + bw - 1) // bw
    n_valid = bucket * (bw // cw)
    row_c = q0 + lax.broadcasted_iota(I32, (tq, cw), 0)
    lane_c = lax.broadcasted_iota(I32, (tq, cw), 1)

    def score_chunk(c, carry):
        c0 = pl.multiple_of(c * cw, cw)
        kt = kit_ref[0, :, pl.ds(c0, cw)]
        acc = jnp.zeros((tq, cw), F32)
        for h in range(H_IDX):
            d = jnp.dot(qi_ref[0, :, h * D_IDX:(h + 1) * D_IDX], kt, preferred_element_type=F32)
            acc = acc + w[:, h:h + 1] * jnp.maximum(d, 0.0)
        key_sc[:, pl.ds(c0, cw)] = jnp.where(c0 + lane_c <= row_c, _sortable_key(acc), jnp.int32(INT_MIN))
        return carry

    lax.fori_loop(0, n_valid, score_chunk, 0)

    lane = lax.broadcasted_iota(I32, (tq, LANES), 1)
    row = q0 + lax.broadcasted_iota(I32, (tq, LANES), 0)
    one = lambda pred: jnp.where(pred, 1, 0)

    def select(n_lt):
        def count(ind):
            part = jnp.zeros((tq, LANES), I32)
            for c in range(n_lt):
                part = part + ind(key_sc[:, c * LANES:(c + 1) * LANES], c * LANES + lane)
            return jnp.sum(part, axis=1, keepdims=True)

        thr = _kth_largest_key(lambda t: count(lambda k, col: one(k >= t)), (tq, 1), topk)
        n_gt = count(lambda k, col: one(k > thr))
        n_ge = count(lambda k, col: one(k >= thr))
        excess = jnp.where(thr > jnp.int32(INT_MIN), n_ge - topk, 0)
        cut_sc[...] = jnp.full((tq, 1), T, I32)

        @pl.when(jnp.max(excess) > 0)
        def _():
            need = topk - n_gt
            nbits = max(1, (T - 1).bit_length())

            def body(b, cut):
                cand = cut | lax.shift_left(jnp.int32(1), nbits - 1 - b)
                cnt = count(lambda k, col: jnp.where(k == thr, one(col < cand), 0))
                return jnp.where(cnt < need, cand, cut)
            cut = lax.fori_loop(0, nbits, body, jnp.zeros((tq, 1), I32))
            cut_sc[...] = jnp.where(excess > 0, cut, T)

        cut = cut_sc[...]
        for c in range(n_lt):
            k = key_sc[:, c * LANES:(c + 1) * LANES]
            col = c * LANES + lane
            val = jnp.where(k > thr, 0.0, jnp.where(k == thr, jnp.where(col <= cut, 0.0, NEG), NEG))
            mask_ref[0, :, c * LANES:(c + 1) * LANES] = jnp.where(col <= row, val, NEG)
        if n_lt * LANES < T:
            mask_ref[0, :, n_lt * LANES:] = jnp.full((tq, T - n_lt * LANES), NEG, F32)

    for k in range(1, T // bw + 1):
        pl.when(bucket == k)(functools.partial(select, k * bw // LANES))


def _dsa_select(qi, small, ki_t, topk, tq=256, cw=256, bw=512):
    B, T, _ = qi.shape
    tq, cw, bw = min(tq, T), min(cw, T), min(bw, T)
    assert bw % cw == 0 and T % bw == 0
    return pl.pallas_call(
        functools.partial(_dsa_select_body, tq=tq, cw=cw, bw=bw, T=T, topk=topk),
        name="dsa_select", grid=(B, T // tq),
        in_specs=[pl.BlockSpec((1, tq, H_IDX * D_IDX), lambda b, i: (b, i, 0)),
                  pl.BlockSpec((1, tq, LANES), lambda b, i: (b, i, 0)),
                  pl.BlockSpec((1, D_IDX, T), lambda b, i: (b, 0, 0))],
        out_specs=pl.BlockSpec((1, tq, T), lambda b, i: (b, i, 0)),
        out_shape=jax.ShapeDtypeStruct((B, T, T), F32),
        scratch_shapes=[pltpu.VMEM((tq, T), I32), pltpu.VMEM((tq, 1), I32)],
        compiler_params=_params("parallel", "parallel"),
    )(qi, small, ki_t)


def _t5_bucket(dist):
    n = jnp.maximum(dist, 0)
    max_exact = N_BUCKETS // 2
    nf = jnp.maximum(n, 1).astype(F32)
    large = max_exact + (jnp.log(nf / max_exact) / math.log(MAX_DIST / max_exact)
                         * (N_BUCKETS - max_exact)).astype(I32)
    large = jnp.minimum(large, N_BUCKETS - 1)
    return jnp.where(n < max_exact, n, large)


def _t5_tiles(rel_table):
    assert LANES >= MAX_DIST
    i = jnp.arange(LANES)
    bucket = _t5_bucket((jnp.arange(2) * LANES)[:, None, None] + i[None, :, None] - i[None, None, :])
    rel = rel_table.astype(F32)
    tz = jnp.zeros((rel.shape[1],) + bucket.shape, F32)
    for k in range(N_BUCKETS - 1):
        tz = tz + jnp.where(bucket[None] == k, (rel[k] - rel[N_BUCKETS - 1])[:, None, None, None], 0.0)
    return tz * LOG2E


def _s5_disc_body(lr_ref, li_ref, ldt_ref, br_ref, bi_ref, abr_ref, abi_ref, bbr_ref, bbi_ref):
    lr, li = lr_ref[...], li_ref[...]
    dt = jnp.exp(ldt_ref[...])
    mag = jnp.exp(lr * dt)
    ab_re, ab_im = mag * jnp.cos(li * dt), mag * jnp.sin(li * dt)
    den = lr * lr + li * li
    nr = ab_re - 1.0
    k_re = (nr * lr + ab_im * li) / den
    k_im = (ab_im * lr - nr * li) / den
    br, bi = br_ref[...], bi_ref[...]
    abr_ref[...] = ab_re
    abi_ref[...] = ab_im
    bbr_ref[...] = k_re * br - k_im * bi
    bbi_ref[...] = k_re * bi + k_im * br


def _s5_discretise(lam_re, lam_im, log_dt, b_re, b_im):
    G, N, P = b_re.shape
    rep = lambda a: jnp.broadcast_to(a.reshape(G * N, 1), (G * N, P))
    ldt = jnp.broadcast_to(log_dt.reshape(G, 1, 1), (G, N, P)).reshape(G * N, P)
    shp = jax.ShapeDtypeStruct((G * N, P), F32)
    abr, abi, bbr, bbi = pl.pallas_call(
        _s5_disc_body, name="s5_discretise", out_shape=[shp] * 4,
        compiler_params=pltpu.CompilerParams(vmem_limit_bytes=V7X_VMEM_LIMIT_BYTES),
    )(rep(lam_re), rep(lam_im), ldt, b_re.reshape(G * N, P), b_im.reshape(G * N, P))
    return (abr[:, 0].reshape(G, N), abi[:, 0].reshape(G, N),
            bbr.reshape(G, N, P), bbi.reshape(G, N, P))


def _s5_body(u_ref, bc_ref, cc_ref, a_ref, d_ref, x0_ref, y_ref, xT_ref, xs_sc, carry_sc, *, tc, S, lc):
    c = pl.program_id(1)

    @pl.when(c == 0)
    def _():
        carry_sc[...] = x0_ref[0]

    n_ct = u_ref.shape[2] // LANES
    ks = S // n_ct
    ch = lambda t: slice(t * LANES, (t + 1) * LANES)
    st = lambda j: slice(j * ks, (j + 1) * ks)
    for j in range(2 * n_ct):
        xs_sc[:, st(j)] = jnp.dot(u_ref[0, :, ch(j % n_ct)].astype(BF16), bc_ref[j], preferred_element_type=F32)

    for j in range(S // lc):
        slr = slice(j * lc, (j + 1) * lc)
        sli = slice(S + j * lc, S + (j + 1) * lc)
        ar, ai = a_ref[:, slr], a_ref[:, sli]

        def step(t, carry, slr=slr, sli=sli, ar=ar, ai=ai):
            xr, xi = carry
            nr = ar * xr - ai * xi + xs_sc[pl.ds(t, 1), slr]
            ni = ar * xi + ai * xr + xs_sc[pl.ds(t, 1), sli]
            xs_sc[pl.ds(t, 1), slr] = nr
            xs_sc[pl.ds(t, 1), sli] = ni
            return nr, ni

        xr, xi = lax.fori_loop(0, tc, step, (carry_sc[:, slr], carry_sc[:, sli]), unroll=min(8, tc))
        carry_sc[:, slr] = xr
        carry_sc[:, sli] = xi

    for t in range(n_ct):
        y = (jnp.dot(xs_sc[:, st(t)].astype(BF16), cc_ref[t], preferred_element_type=F32)
             + jnp.dot(xs_sc[:, st(n_ct + t)].astype(BF16), cc_ref[n_ct + t], preferred_element_type=F32)
             + d_ref[:, ch(t)] * u_ref[0, :, ch(t)])
        y_ref[0, :, ch(t)] = _gelu(y).astype(y_ref.dtype)
    xT_ref[0] = carry_sc[...]


def _glu_body(y_ref, wa_ref, wb_ref, o_ref):
    y = y_ref[...]
    a = jnp.dot(y, wa_ref[...], preferred_element_type=F32)
    b = jnp.dot(y, wb_ref[...], preferred_element_type=F32)
    o_ref[...] = (a * _sigmoid(b)).astype(o_ref.dtype)


def _glu(y, wa, wb, tm=512):
    M, K = y.shape
    N = wa.shape[1]
    tm = min(tm, M)
    return pl.pallas_call(
        _glu_body, name="s5_glu", grid=(M // tm,),
        in_specs=[pl.BlockSpec((tm, K), lambda i: (i, 0)), pl.BlockSpec((K, N), lambda i: (0, 0)),
                  pl.BlockSpec((K, N), lambda i: (0, 0))],
        out_specs=pl.BlockSpec((tm, N), lambda i: (i, 0)),
        out_shape=jax.ShapeDtypeStruct((M, N), BF16),
        compiler_params=_params("parallel"),
    )(y, wa, wb)


def _merge_body(oa_ref, ob_ref, oc_ref, wa_ref, wb_ref, wc_ref, ga_ref, gb_ref, gc_ref, o_ref):
    dot = lambda x, w: jnp.dot(x[...], w[...], preferred_element_type=F32)
    m = (_sigmoid(ga_ref[...]) * dot(oa_ref, wa_ref) + _sigmoid(gb_ref[...]) * dot(ob_ref, wb_ref)
         + _sigmoid(gc_ref[...]) * dot(oc_ref, wc_ref))
    o_ref[...] = m.astype(o_ref.dtype)


def _merge(o_a, o_b, o_c, wa, wb, wc, gl, tm=512, tn=512):
    M, K = o_a.shape
    D = wa.shape[1]
    tm = min(tm, M)
    nd = D // tn
    ospec = pl.BlockSpec((tm, K), lambda i, j: (i, 0))
    wspec = pl.BlockSpec((K, tn), lambda i, j: (0, j))
    gspec = lambda g: pl.BlockSpec((tm, tn), lambda i, j, g=g: (i, g * nd + j))
    return pl.pallas_call(
        _merge_body, name="branch_merge", grid=(M // tm, nd),
        in_specs=[ospec, ospec, ospec, wspec, wspec, wspec, gspec(0), gspec(1), gspec(2)],
        out_specs=pl.BlockSpec((tm, tn), lambda i, j: (i, j)),
        out_shape=jax.ShapeDtypeStruct((M, D), BF16),
        compiler_params=_params("parallel", "parallel"),
    )(o_a, o_b, o_c, wa, wb, wc, gl, gl, gl)


def _top16_rows(s, n_rows):
    iota = lax.broadcasted_iota(I32, s.shape, 0)
    rank = jnp.full(s.shape, P_TOPK, I32)
    vals = []
    work = s
    for k in range(P_TOPK):
        m = jnp.max(work, axis=0, keepdims=True)
        idx = jnp.min(jnp.where(work == m, iota, n_rows), axis=0, keepdims=True)
        hit = iota == idx
        rank = jnp.where(hit, k, rank)
        work = jnp.where(hit, -jnp.inf, work)
        vals.append(m)
    return rank, vals


def _peer_select_body(q_ref, keys_ref, a1_ref, n1_ref, a2_ref, r2_ref, cand_sc, sel_sc):
    tt = q_ref.shape[0]
    nt = (((1,), (1,)), ((), ()))
    for h in range(P_HEADS):
        s1 = lax.dot_general(keys_ref[2 * h], q_ref[:, (2 * h) * N_KEYS:(2 * h + 1) * N_KEYS], nt,
                             preferred_element_type=F32)
        s2 = lax.dot_general(keys_ref[2 * h + 1], q_ref[:, (2 * h + 1) * N_KEYS:(2 * h + 2) * N_KEYS], nt,
                             preferred_element_type=F32)
        r1, v1 = _top16_rows(s1, N_KEYS)
        r2, v2 = _top16_rows(s2, N_KEYS)
        for r, (i, j) in enumerate(_PAIRS):
            cand_sc[r:r + 1, :] = v1[i] + v2[j]
        cand_sc[len(_PAIRS):, :] = jnp.full((_N_CAND - len(_PAIRS), tt), -jnp.inf, F32)
        rc, cv = _top16_rows(cand_sc[...], _N_CAND)
        z = jnp.zeros((1, tt), F32)
        for k in range(P_TOPK):
            z = z + jnp.exp(cv[k] - cv[0])
        sel_sc[...] = jnp.where(rc < P_TOPK, 1, 0)
        cnt = [jnp.zeros((1, tt), I32) for _ in range(P_TOPK)]
        for r, (i, j) in enumerate(_PAIRS):
            cnt[i] = cnt[i] + sel_sc[r:r + 1, :]
        n1 = jnp.zeros((N_KEYS, tt), I32)
        for i in range(P_TOPK):
            n1 = jnp.where(r1 == i, cnt[i], n1)
        a1_ref[h] = jnp.exp(s1 - v1[0]) / z
        n1_ref[h] = n1.astype(F32)
        a2_ref[h] = jnp.exp(s2 - v2[0]).astype(a2_ref.dtype)
        r2_ref[h] = r2.astype(F32).astype(r2_ref.dtype)


def _peer_select(q, keys, tt=256):
    M = q.shape[0]
    tt = min(tt, M)
    tab = pl.BlockSpec((P_HEADS, N_KEYS, tt), lambda i: (0, 0, i))
    shp = lambda dt: jax.ShapeDtypeStruct((P_HEADS, N_KEYS, M), dt)
    return pl.pallas_call(
        _peer_select_body, name="peer_select", grid=(M // tt,),
        in_specs=[pl.BlockSpec((tt, q.shape[1]), lambda i: (i, 0)),
                  pl.BlockSpec(keys.shape, lambda i: (0, 0, 0))],
        out_specs=[tab, tab, tab, tab],
        out_shape=[shp(F32), shp(F32), shp(BF16), shp(BF16)],
        scratch_shapes=[pltpu.VMEM((_N_CAND, tt), F32), pltpu.VMEM((_N_CAND, tt), I32)],
        compiler_params=_params("parallel"),
    )(q, keys)


def _peer_dense_body(h_ref, u_ref, vt_ref, a1_ref, n1_ref, a2_ref, r2_ref, x_ref, gt_ref, o_ref,
                     acc_sc, act_sc, ga_sc, *, te):
    j = pl.program_id(1)

    @pl.when(j == 0)
    def _():
        acc_sc[...] = jnp.zeros(acc_sc.shape, F32)
        act_sc[1] = jnp.zeros(act_sc.shape[1:], F32)

    def step(slot):
        act_sc[slot] = lax.dot_general(u_ref[...], h_ref[...], (((1,), (1,)), ((), ())),
                                       preferred_element_type=F32)
        for r in range(te // N_KEYS):
            rows = slice(r * N_KEYS, (r + 1) * N_KEYS)
            wt = None
            for h in range(P_HEADS):
                a1 = a1_ref[h, r:r + 1, :].astype(BF16)
                n1 = n1_ref[h, r:r + 1, :].astype(BF16)
                term = jnp.where(r2_ref[h] < n1, a1 * a2_ref[h], jnp.zeros((), BF16))
                wt = term if wt is None else wt + term
            ga_sc[rows, :] = wt * _gelu(act_sc[1 - slot, rows, :]).astype(BF16)
        acc_sc[...] += jnp.dot(vt_ref[...], ga_sc[...], preferred_element_type=F32)

    pl.when(lax.rem(j, 2) == 0)(functools.partial(step, 0))
    pl.when(lax.rem(j, 2) == 1)(functools.partial(step, 1))

    @pl.when(j == pl.num_programs(1) - 1)
    def _():
        o_ref[...] = x_ref[...] + gt_ref[...] * acc_sc[...].T


def _peer_dense(h2, u, v_t, tabs, x, gt, rows_per_group, tt=512, te=1024):
    M, D = h2.shape
    E = u.shape[0]
    tt = min(tt, M)
    n_e1 = te // N_KEYS
    nc = E // te
    chunk = lambda j: jnp.clip(j, 0, nc - 1)
    tab1 = pl.BlockSpec((P_HEADS, n_e1, tt), lambda i, j: (0, chunk(j - 1), i))
    tab2 = pl.BlockSpec((P_HEADS, N_KEYS, tt), lambda i, j: (0, 0, i))
    r = gt.shape[1]
    return pl.pallas_call(
        functools.partial(_peer_dense_body, te=te),
        name="peer_dense", grid=(M // tt, nc + 1),
        in_specs=[pl.BlockSpec((tt, D), lambda i, j: (i, 0)),
                  pl.BlockSpec((te, D), lambda i, j: (chunk(j), 0)),
                  pl.BlockSpec((D, te), lambda i, j: (0, chunk(j - 1))),
                  tab1, tab1, tab2, tab2,
                  pl.BlockSpec((tt, D), lambda i, j: (i, 0)),
                  pl.BlockSpec((None, r, D), lambda i, j: ((i * tt) // rows_per_group, 0, 0))],
        out_specs=pl.BlockSpec((tt, D), lambda i, j: (i, 0)),
        out_shape=jax.ShapeDtypeStruct((M, D), F32),
        scratch_shapes=[pltpu.VMEM((D, tt), F32), pltpu.VMEM((2, te, tt), F32), pltpu.VMEM((te, tt), BF16)],
        compiler_params=_params("parallel", "arbitrary"),
    )(h2, u, v_t, *tabs, x, gt)


def _peer(h2, x, gt, rows_per_group, wq, keys, u, v_t):
    (q,) = _mm(h2, wq, [BF16], tm=512, tn=1024, name="peer_query")
    tabs = _peer_select(q, keys)
    return _peer_dense(h2, u, v_t, tabs, x, gt, rows_per_group)


def _s5_prepare(lam_re, lam_im, log_dt, b_re, b_im, c_re, c_im):
    G, N, P = b_re.shape
    S, W = G * N, G * P
    ab_re, ab_im, bb_re, bb_im = _s5_discretise(lam_re, lam_im, log_dt, b_re, b_im)
    cg = LANES // P
    n_ct = G // cg
    eye = jnp.eye(cg, dtype=F32)

    def tiles(m, rows, cols):
        t = jnp.swapaxes(m, 1, 2).reshape(n_ct, cg, m.shape[2], m.shape[1])
        return (t[:, :, :, None, :] * eye[None, :, None, :, None]).reshape(n_ct, rows, cols)

    bc = jnp.concatenate([tiles(bb_re, LANES, cg * N), tiles(bb_im, LANES, cg * N)], axis=0).astype(BF16)
    cc = jnp.concatenate([tiles(c_re.astype(F32), cg * N, LANES), -tiles(c_im.astype(F32), cg * N, LANES)],
                         axis=0).astype(BF16)
    a_row = jnp.concatenate([ab_re.reshape(1, S), ab_im.reshape(1, S)], axis=1)
    return a_row, bc, cc


def _s5(u, T, x0_re, x0_im, prep, d_skip, tc=128, lc=1024):
    M, W = u.shape
    B = M // T
    G, N = x0_re.shape[1:]
    S = G * N
    a_row, bc, cc = prep
    tc = min(tc, T)
    x0 = jnp.concatenate([x0_re.reshape(B, 1, S), x0_im.reshape(B, 1, S)], axis=2).astype(F32)
    yg, x_last = pl.pallas_call(
        functools.partial(_s5_body, tc=tc, S=S, lc=lc),
        name="s5", grid=(B, T // tc),
        in_specs=[pl.BlockSpec((1, tc, W), lambda b, c: (b, c, 0)),
                  pl.BlockSpec(bc.shape, lambda b, c: (0, 0, 0)),
                  pl.BlockSpec(cc.shape, lambda b, c: (0, 0, 0)),
                  pl.BlockSpec((1, 2 * S), lambda b, c: (0, 0)),
                  pl.BlockSpec((1, W), lambda b, c: (0, 0)),
                  pl.BlockSpec((1, 1, 2 * S), lambda b, c: (b, 0, 0))],
        out_specs=[pl.BlockSpec((1, tc, W), lambda b, c: (b, c, 0)),
                   pl.BlockSpec((1, 1, 2 * S), lambda b, c: (b, 0, 0))],
        out_shape=[jax.ShapeDtypeStruct((B, T, W), BF16), jax.ShapeDtypeStruct((B, 1, 2 * S), F32)],
        scratch_shapes=[pltpu.VMEM((tc, 2 * S), F32), pltpu.VMEM((1, 2 * S), F32)],
        compiler_params=_params("parallel", "arbitrary"),
    )(u.reshape(B, T, W), bc, cc, a_row, d_skip.reshape(1, W).astype(F32), x0)
    return yg.reshape(M, W), x_last[:, 0, :S].reshape(B, G, N), x_last[:, 0, S:].reshape(B, G, N)


def _logsig_body(f_ref, b_ref, o_ref):
    o_ref[...] = _log_sigmoid(f_ref[...] + b_ref[...])


def _logsig(f, b_row):
    return pl.pallas_call(_logsig_body, out_shape=jax.ShapeDtypeStruct(f.shape, F32))(f, b_row)


def _head_lane_select(n_lanes):
    lane = lax.broadcasted_iota(I32, (N_HEADS, n_lanes), 1)
    head = lax.broadcasted_iota(I32, (N_HEADS, n_lanes), 0)
    return (lane & (N_HEADS - 1)) == head, lane


def _page_prefix_body(lf_ref, pre_ref, tot_ref):
    lf = lf_ref[...]
    n_lanes = lf.shape[1]
    lane = lax.broadcasted_iota(I32, lf.shape, 1)
    pre, tot = lf, lf
    sh = N_HEADS
    while sh < n_lanes:
        pre = pre + jnp.where(lane >= sh, pltpu.roll(pre, sh, axis=1), 0.0)
        tot = tot + pltpu.roll(tot, sh, axis=1)
        sh *= 2
    pre_ref[...] = pre
    tot_ref[...] = tot


def _page_prefix(logf_rows):
    R, n_lanes = logf_rows.shape
    tr = math.gcd(R, 256)
    spec = pl.BlockSpec((tr, n_lanes), lambda i: (i, 0))
    return pl.pallas_call(
        _page_prefix_body, name="fox_page_prefix", grid=(R // tr,), in_specs=[spec], out_specs=[spec, spec],
        out_shape=[jax.ShapeDtypeStruct((R, n_lanes), F32)] * 2,
        compiler_params=_params("parallel"),
    )(logf_rows)


def _fox_sample_body(pt_ref, q_ref, kn_ref, vn_ref, lfn_ref, k_hbm, v_hbm, pre_hbm, tot_hbm, o_ref,
                     kbuf, vbuf, pbuf, tbuf, sems, m_sc, l_sc, acc_sc, carry_sc,
                     *, layer, n_pool, n_pages, pp, scale):
    g = pl.program_id(1)
    ng = pl.num_programs(1)
    step = pl.program_id(0) * ng + g
    n_steps = pl.num_programs(0) * ng
    slot = lax.rem(step, 2)
    page_lanes = PAGE * N_HEADS
    n_lanes = pp * page_lanes
    nt = (((1,), (1,)), ((), ()))

    def copies(st, sl):
        first = (st // ng) * n_pages + lax.rem(st, ng) * pp
        out = []
        for k in range(pp):
            pg = pt_ref[first + k]
            row = layer * n_pool + pg
            out += [pltpu.make_async_copy(k_hbm.at[layer, pg], kbuf.at[sl, pl.ds(k * PAGE, PAGE)], sems.at[0, sl]),
                    pltpu.make_async_copy(v_hbm.at[layer, pg], vbuf.at[sl, pl.ds(k * PAGE, PAGE)], sems.at[1, sl]),
                    pltpu.make_async_copy(pre_hbm.at[pl.ds(row, 1)], pbuf.at[sl, pl.ds(k, 1)], sems.at[2, sl]),
                    pltpu.make_async_copy(tot_hbm.at[pl.ds(row, 1)], tbuf.at[sl, pl.ds(k, 1)], sems.at[3, sl])]
        return out

    @pl.when(step == 0)
    def _():
        for c in copies(step, slot):
            c.start()

    @pl.when(step + 1 < n_steps)
    def _():
        for c in copies(step + 1, 1 - slot):
            c.start()

    @pl.when(g == 0)
    def _():
        m_sc[...] = jnp.full(m_sc.shape, -jnp.inf, F32)
        l_sc[...] = jnp.zeros(l_sc.shape, F32)
        acc_sc[...] = jnp.zeros(acc_sc.shape, F32)
        carry_sc[...] = jnp.zeros(carry_sc.shape, F32)

    for c in copies(step, slot):
        c.wait()

    hsel, lane8 = _head_lane_select(n_lanes)
    q = q_ref[0]
    kp = kbuf[slot].reshape(n_lanes, HEAD_DIM).astype(BF16)
    s = lax.dot_general(q, kp, nt, preferred_element_type=F32) * scale
    carry = carry_sc[...]
    f_pages = []
    for k in range(pp):
        f_pages.append(carry + pbuf[slot, k:k + 1, :])
        carry = carry + tbuf[slot, k:k + 1, :]
    carry_sc[...] = carry
    f_k = jnp.concatenate(f_pages, axis=1) if pp > 1 else f_pages[0]
    s = jnp.where(hsel, s - f_k, NEG)
    m_prev = m_sc[...]
    m_new = jnp.maximum(m_prev, jnp.max(s, axis=1, keepdims=True))
    alpha = jnp.exp(m_prev - m_new)
    pr = jnp.exp(s - m_new)
    l_sc[...] = alpha * l_sc[...] + jnp.sum(pr, axis=1, keepdims=True)
    vp = vbuf[slot].reshape(n_lanes, HEAD_DIM).astype(BF16)
    acc_sc[...] = alpha * acc_sc[...] + jnp.dot(pr.astype(BF16), vp, preferred_element_type=F32)
    m_sc[...] = m_new

    @pl.when(g == ng - 1)
    def _():
        hsel_p, lane_p = _head_lane_select(page_lanes)
        f_col = jnp.sum(jnp.where(hsel_p, jnp.where(lane_p < N_HEADS, carry_sc[...], 0.0), 0.0),
                        axis=1, keepdims=True)
        kn = kn_ref[0].astype(BF16).astype(F32)
        vn = vn_ref[0].astype(BF16).astype(F32)
        s_n = jnp.sum(q.astype(F32) * kn, axis=1, keepdims=True) * scale - (f_col + lfn_ref[0])
        m_prev = m_sc[...]
        m_new = jnp.maximum(m_prev, s_n)
        alpha = jnp.exp(m_prev - m_new)
        p_n = jnp.exp(s_n - m_new)
        l = alpha * l_sc[...] + p_n
        acc = alpha * acc_sc[...] + p_n.astype(BF16).astype(F32) * vn
        o_ref[0] = acc / l


def _fox_sample(layer, pt_flat, q, k_new, v_new, logf_new, cache_k, cache_v, page_pre, page_tot, pp=8):
    Bd = q.shape[0]
    n_pool = cache_k.shape[1]
    n_pages = pt_flat.shape[0] // Bd
    pp = math.gcd(pp, n_pages)
    tok = lambda w: pl.BlockSpec((1, N_HEADS, w), lambda b, g, pt: (b, 0, 0))
    hbm = pl.BlockSpec(memory_space=pl.ANY)
    grid_spec = pltpu.PrefetchScalarGridSpec(
        num_scalar_prefetch=1, grid=(Bd, n_pages // pp),
        in_specs=[tok(HEAD_DIM), tok(HEAD_DIM), tok(HEAD_DIM), tok(1), hbm, hbm, hbm, hbm],
        out_specs=pl.BlockSpec((1, N_HEADS, HEAD_DIM), lambda b, g, pt: (b, 0, 0)),
        scratch_shapes=[pltpu.VMEM((2, pp * PAGE, N_HEADS, HEAD_DIM), F32),
                        pltpu.VMEM((2, pp * PAGE, N_HEADS, HEAD_DIM), F32),
                        pltpu.VMEM((2, pp, PAGE * N_HEADS), F32), pltpu.VMEM((2, pp, PAGE * N_HEADS), F32),
                        pltpu.SemaphoreType.DMA((4, 2)),
                        pltpu.VMEM((N_HEADS, 1), F32), pltpu.VMEM((N_HEADS, 1), F32),
                        pltpu.VMEM((N_HEADS, HEAD_DIM), F32), pltpu.VMEM((1, PAGE * N_HEADS), F32)])
    return pl.pallas_call(
        functools.partial(_fox_sample_body, layer=layer, n_pool=n_pool, n_pages=n_pages, pp=pp,
                          scale=HEAD_DIM ** -0.5),
        name="fox_sample", grid_spec=grid_spec,
        out_shape=jax.ShapeDtypeStruct((Bd, N_HEADS, HEAD_DIM), F32),
        compiler_params=_params("arbitrary", "arbitrary"),
    )(pt_flat, q, k_new, v_new, logf_new, cache_k, cache_v, page_pre, page_tot)


def _idx_scores(qi, w, ki_rows):
    d = lax.dot_general(qi, ki_rows.astype(BF16), (((1,), (1,)), ((), ())), preferred_element_type=F32)
    return jnp.sum(w * (H_IDX ** -0.5 * D_IDX ** -0.5) * jnp.maximum(d, 0.0), axis=0, keepdims=True)


def _idx_scores_body(qi_ref, w_ref, ki_ref, o_ref):
    o_ref[0] = _idx_scores(qi_ref[0], w_ref[0], ki_ref[...])


def _idx_scores_paged_body(pt_ref, qi_ref, w_ref, ki_hbm, o_ref, buf, sem, *, layer, n_pages, chunk):
    b = pl.program_id(0)

    def page_copy(p):
        return pltpu.make_async_copy(ki_hbm.at[layer, pt_ref[b * n_pages + p]],
                                     buf.at[pl.ds(pl.multiple_of(p * PAGE, PAGE), PAGE)], sem.at[0])

    def start(p, c):
        page_copy(p).start()
        return c

    def wait(p, c):
        page_copy(p).wait()
        return c

    lax.fori_loop(0, n_pages, start, 0)
    lax.fori_loop(0, n_pages, wait, 0)
    for c in range(n_pages * PAGE // chunk):
        o_ref[0, :, c * chunk:(c + 1) * chunk] = _idx_scores(qi_ref[0], w_ref[0], buf[c * chunk:(c + 1) * chunk, :])


def _idx_scores_paged(layer, pt_flat, qi, w, cache_ki, chunk=2048):
    Bd = qi.shape[0]
    n_pages = pt_flat.shape[0] // Bd
    chunk = math.gcd(chunk, n_pages * PAGE)
    grid_spec = pltpu.PrefetchScalarGridSpec(
        num_scalar_prefetch=1, grid=(Bd,),
        in_specs=[pl.BlockSpec((1, H_IDX, D_IDX), lambda b, pt: (b, 0, 0)),
                  pl.BlockSpec((1, H_IDX, 1), lambda b, pt: (b, 0, 0)),
                  pl.BlockSpec(memory_space=pl.ANY)],
        out_specs=pl.BlockSpec((1, 1, n_pages * PAGE), lambda b, pt: (b, 0, 0)),
        scratch_shapes=[pltpu.VMEM((n_pages * PAGE, D_IDX), F32), pltpu.SemaphoreType.DMA((1,))])
    return pl.pallas_call(
        functools.partial(_idx_scores_paged_body, layer=layer, n_pages=n_pages, chunk=chunk),
        name="dsa_sample_scores", grid_spec=grid_spec,
        out_shape=jax.ShapeDtypeStruct((Bd, 1, n_pages * PAGE), F32),
        compiler_params=_params("arbitrary"),
    )(pt_flat, qi, w, cache_ki)


def _idx_scores_new(qi, w, ki_rows):
    Bd, R, _ = ki_rows.shape
    return pl.pallas_call(
        _idx_scores_body, grid=(Bd,),
        in_specs=[pl.BlockSpec((1, H_IDX, D_IDX), lambda b: (b, 0, 0)),
                  pl.BlockSpec((1, H_IDX, 1), lambda b: (b, 0, 0)),
                  pl.BlockSpec((None, R, D_IDX), lambda b: (b, 0, 0))],
        out_specs=pl.BlockSpec((1, 1, R), lambda b: (b, 0, 0)),
        out_shape=jax.ShapeDtypeStruct((Bd, 1, R), F32),
        compiler_params=_params("parallel"),
    )(qi, w, ki_rows)


def _dsa_sample_select_body(sc_ref, scn_ref, rel_ref, idx_ref, sb_ref, nb_ref, rs_sc, *, topk, chunk):
    Bd, P = sc_ref.shape
    key_p = _sortable_key(sc_ref[...])
    key_n = _sortable_key(scn_ref[:, 0:1])
    col = lax.broadcasted_iota(I32, (Bd, P), 1)

    one = lambda pred: jnp.where(pred, 1, 0)

    def count(ind_p, ind_n):
        return jnp.sum(ind_p, axis=1, keepdims=True) + ind_n

    thr = _kth_largest_key(lambda t: count(one(key_p >= t), one(key_n >= t)), (Bd, 1), topk)
    n_gt = count(one(key_p > thr), one(key_n > thr))
    n_ge = count(one(key_p >= thr), one(key_n >= thr))
    need = topk - n_gt
    nbits = (P + 1).bit_length()

    def body(b, cut):
        cand = cut | lax.shift_left(jnp.int32(1), nbits - 1 - b)
        cnt = count(jnp.where(key_p == thr, one(col < cand), 0), jnp.where(key_n == thr, one(P < cand), 0))
        return jnp.where(cnt < need, cand, cut)
    cut = lax.fori_loop(0, nbits, body, jnp.zeros((Bd, 1), I32))
    cut = jnp.where(n_ge > topk, cut, jnp.int32(2 ** 30))
    sel_p = jnp.where(key_p > thr, 1, jnp.where(key_p == thr, one(col <= cut), 0))
    sel_n = jnp.where(key_n > thr, 1, jnp.where(key_n == thr, one(P <= cut), 0))
    rank = sel_p
    sh = 1
    while sh < P:
        rank = rank + jnp.where(col >= sh, pltpu.roll(rank, sh, axis=1), 0)
        sh *= 2
    rs_sc[...] = sel_p * rank
    n_past = jnp.sum(sel_p, axis=1, keepdims=True)
    nb_ref[...] = jnp.where(sel_n > 0, rel_ref[0:1, :], NEG)

    slot = lax.broadcasted_iota(I32, (topk, 1), 0)
    ccol = lax.broadcasted_iota(I32, (topk, chunk), 1)
    for b in range(Bd):
        idx = jnp.zeros((topk, 1), I32)
        for c in range(P // chunk):
            rs = rs_sc[b:b + 1, c * chunk:(c + 1) * chunk]
            idx = idx + jnp.sum(jnp.where(rs == slot + 1, ccol + c * chunk, 0), axis=1, keepdims=True)
        idx_ref[b] = idx
        bucket = _t5_bucket(P - idx)
        bias = jnp.zeros((topk, N_HEADS), F32)
        for k in range(N_BUCKETS):
            bias = bias + jnp.where(bucket == k, rel_ref[k:k + 1, :], 0.0)
        sb_ref[b] = jnp.where(slot < n_past[b:b + 1, :], bias, NEG)


def _dsa_sample_select(sc, sc_new, rel_table, topk, chunk=2048):
    Bd, P = sc.shape
    return pl.pallas_call(
        functools.partial(_dsa_sample_select_body, topk=topk, chunk=min(chunk, P)),
        out_shape=[jax.ShapeDtypeStruct((Bd, topk, 1), I32), jax.ShapeDtypeStruct((Bd, topk, N_HEADS), F32),
                   jax.ShapeDtypeStruct((Bd, N_HEADS), F32)],
        scratch_shapes=[pltpu.VMEM((Bd, P), I32)],
        compiler_params=pltpu.CompilerParams(vmem_limit_bytes=V7X_VMEM_LIMIT_BYTES),
    )(sc, sc_new, rel_table)


def _dsa_sample_attend_body(idx_ref, pt_ref, q_ref, kn_ref, vn_ref, sb_ref, nb_ref, kc_hbm, vc_hbm, o_ref,
                            kbuf, vbuf, sems, *, layer, topk, n_pages, scale):
    b = pl.program_id(0)

    def row_copies(j):
        i = idx_ref[b * topk + j]
        pg = pt_ref[b * n_pages + lax.shift_right_logical(i, 7)]
        off = i & (PAGE - 1)
        return (pltpu.make_async_copy(kc_hbm.at[layer, pg, off], kbuf.at[j], sems.at[0]),
                pltpu.make_async_copy(vc_hbm.at[layer, pg, off], vbuf.at[j], sems.at[1]))

    def start(j, c):
        ck, cv = row_copies(j)
        ck.start()
        cv.start()
        return c

    def wait(j, c):
        ck, cv = row_copies(j)
        ck.wait()
        cv.wait()
        return c

    lax.fori_loop(0, topk, start, 0)
    lax.fori_loop(0, topk, wait, 0)

    n_lanes = topk * N_HEADS
    hsel, _ = _head_lane_select(n_lanes)
    q = q_ref[0]
    kb = kbuf[...].reshape(n_lanes, HEAD_DIM).astype(BF16)
    s = lax.dot_general(q, kb, (((1,), (1,)), ((), ())), preferred_element_type=F32) * scale
    s = jnp.where(hsel, s + sb_ref[0], NEG)
    kn = kn_ref[0].astype(BF16).astype(F32)
    vn = vn_ref[0].astype(BF16).astype(F32)
    s_n = jnp.sum(q.astype(F32) * kn, axis=1, keepdims=True) * scale + nb_ref[0]
    m = jnp.maximum(jnp.max(s, axis=1, keepdims=True), s_n)
    p = jnp.exp(s - m)
    p_n = jnp.exp(s_n - m)
    l = jnp.sum(p, axis=1, keepdims=True) + p_n
    vb = vbuf[...].reshape(n_lanes, HEAD_DIM).astype(BF16)
    acc = jnp.dot(p.astype(BF16), vb, preferred_element_type=F32) + p_n.astype(BF16).astype(F32) * vn
    o_ref[0] = acc / l


def _dsa_sample_attend(layer, idx_flat, pt_flat, q, k_new, v_new, slot_bias, new_bias, cache_k, cache_v, topk):
    Bd = q.shape[0]
    n_pages = pt_flat.shape[0] // Bd
    tok = lambda w: pl.BlockSpec((1, N_HEADS, w), lambda b, idx, pt: (b, 0, 0))
    grid_spec = pltpu.PrefetchScalarGridSpec(
        num_scalar_prefetch=2, grid=(Bd,),
        in_specs=[tok(HEAD_DIM), tok(HEAD_DIM), tok(HEAD_DIM),
                  pl.BlockSpec((1, 1, topk * N_HEADS), lambda b, idx, pt: (b, 0, 0)),
                  tok(1),
                  pl.BlockSpec(memory_space=pl.ANY), pl.BlockSpec(memory_space=pl.ANY)],
        out_specs=pl.BlockSpec((1, N_HEADS, HEAD_DIM), lambda b, idx, pt: (b, 0, 0)),
        scratch_shapes=[pltpu.VMEM((topk, N_HEADS, HEAD_DIM), F32), pltpu.VMEM((topk, N_HEADS, HEAD_DIM), F32),
                        pltpu.SemaphoreType.DMA((2,))])
    return pl.pallas_call(
        functools.partial(_dsa_sample_attend_body, layer=layer, topk=topk, n_pages=n_pages,
                          scale=HEAD_DIM ** -0.5),
        grid_spec=grid_spec,
        out_shape=jax.ShapeDtypeStruct((Bd, N_HEADS, HEAD_DIM), F32),
        compiler_params=_params("arbitrary"),
    )(idx_flat, pt_flat, q, k_new, v_new, slot_bias, new_bias, cache_k, cache_v)


W_ATT = N_HEADS * HEAD_DIM
_PROJ_NAMES = ("fq", "fk", "fv", "ff", "bq", "bk", "bv", "iq", "ik", "iw", "su", "gl")
_FF_LANES = slice(0, N_HEADS)
_IW_LANES = slice(N_HEADS, N_HEADS + H_IDX)
_IK_LANES = slice(N_HEADS + H_IDX, N_HEADS + H_IDX + D_IDX)


def _layer_weights(l, p):
    D = p["w_in"].shape[1]
    w_c = p["w_glu"].shape[1]
    sizes = (W_ATT, W_ATT, W_ATT, N_HEADS, W_ATT, W_ATT, W_ATT, H_IDX * D_IDX, D_IDX, H_IDX, w_c, 3 * D)
    w_in = p["w_in"][l]
    cols, off = {}, 0
    for name, n in zip(_PROJ_NAMES, sizes):
        cols[name] = w_in[:, off:off + n]
        off += n
    pad = jnp.zeros((D, LANES - (N_HEADS + H_IDX + D_IDX)), w_in.dtype)
    order = ("fq", "fk", "fv", "bq", "bk", "bv", "iq", "su", "gl")
    lw = {"w_in": jnp.concatenate([cols[n] for n in order] + [cols["ff"], cols["iw"], cols["ik"], pad],
                                  axis=1).astype(BF16)}
    off = 0
    for n in order:
        lw["cols_" + n] = (off, cols[n].shape[1])
        off += cols[n].shape[1]
    lw["cols_small"] = (off, LANES)
    w_glu, w_br = p["w_glu"][l], p["w_br"][l]
    lw["glu_a"], lw["glu_b"] = w_glu[:, :w_c].astype(BF16), w_glu[:, w_c:].astype(BF16)
    lw["wa"] = w_br[:W_ATT].astype(BF16)
    lw["wb"] = w_br[W_ATT:2 * W_ATT].astype(BF16)
    lw["wc"] = w_br[2 * W_ATT:].astype(BF16)
    lw["w_out"] = p["w_out"][l].astype(BF16)
    lw["wq"] = p["peer_wq"][l].astype(BF16)
    lw["keys"] = p["peer_keys"][l].reshape(2 * P_HEADS, N_KEYS, -1).astype(BF16)
    lw["u"] = p["peer_u"][l].astype(BF16)
    lw["v_t"] = p["peer_v"][l].T.astype(BF16)
    lw["s5"] = _s5_prepare(p["s5_lam_re"][l], p["s5_lam_im"][l], p["s5_log_dt"][l], p["s5_b_re"][l],
                           p["s5_b_im"][l], p["s5_c_re"][l], p["s5_c_im"][l])
    lw["s5_d"] = p["s5_d"][l]
    lw["norm1_g"], lw["norm2_g"] = p["norm1_g"][l], p["norm2_g"][l]
    return lw


def _layer(x, mods, rows_per_group, T, lw, s5_re0, s5_im0, attend):
    M, D = x.shape
    sh1, sc1, gt1, sh2, sc2, gt2 = mods
    tm = min(512, M)
    h = _norm_mod(x, lw["norm1_g"], sc1, sh1, rows_per_group)
    proj = lambda name, dts: _mm(h, lw["w_in"], dts, tm=tm, tn=1024, order="nm", name="proj_" + name,
                                 cols=lw["cols_" + name])
    (fq,) = proj("fq", [BF16])
    fk, fk16 = proj("fk", [F32, BF16])
    fv, fv16 = proj("fv", [F32, BF16])
    (bq,) = proj("bq", [BF16])
    bk, bk16 = proj("bk", [F32, BF16])
    bv, bv16 = proj("bv", [F32, BF16])
    (iq,) = proj("iq", [BF16])
    (su,) = proj("su", [F32])
    (small,) = proj("small", [F32])
    (gl,) = proj("gl", [F32])
    o_a, o_b, logf = attend(fq, fk, fk16, fv, fv16, bq, bk, bk16, bv, bv16, iq, small)
    yg, s5_re, s5_im = _s5(su, T, s5_re0, s5_im0, lw["s5"], lw["s5_d"])
    o_c = _glu(yg, lw["glu_a"], lw["glu_b"])
    merged = _merge(o_a, o_b, o_c, lw["wa"], lw["wb"], lw["wc"], gl)
    tn = 512
    (x1,) = _mm(merged, lw["w_out"], [F32], tm=tm, tn=tn,
                epilogue=lambda acc, x_, g_: (x_ + g_ * acc,),
                extras=[(x, (tm, tn), lambda i, j: (i, j)), _mod_extra(gt1, rows_per_group, tm, tn)],
                name="out_proj")
    h2 = _norm_mod(x1, lw["norm2_g"], sc2, sh2, rows_per_group)
    if M < LANES:
        padr = lambda a: jnp.pad(a, ((0, LANES - M), (0, 0)))
        gt2p = jnp.pad(gt2, ((0, 0), (0, LANES - M), (0, 0)))
        x2 = _peer(padr(h2), padr(x1), gt2p, LANES, lw["wq"], lw["keys"], lw["u"], lw["v_t"])[:M]
    else:
        x2 = _peer(h2, x1, gt2, rows_per_group, lw["wq"], lw["keys"], lw["u"], lw["v_t"])
    return x2, (fk, fv, logf, bk, bv, small[:, _IK_LANES], s5_re, s5_im)


def kernel(x_prompt, x_sample, cache_fox_k, cache_fox_v, cache_fox_logf, cache_dsa_k, cache_dsa_v,
           cache_dsa_idx_k, state_s5_re, state_s5_im, page_table, c_prompt, c_sample,
           w_ada, b_ada, norm1_g, norm2_g, w_in, b_f, rel_table, s5_lam_re, s5_lam_im, s5_log_dt,
           s5_b_re, s5_b_im, s5_c_re, s5_c_im, s5_d, w_glu, w_br, w_out,
           peer_wq, peer_keys, peer_u, peer_v, final_norm_g):
    p = dict(w_in=w_in, w_glu=w_glu, w_br=w_br, w_out=w_out, peer_wq=peer_wq, peer_keys=peer_keys,
             peer_u=peer_u, peer_v=peer_v, s5_lam_re=s5_lam_re, s5_lam_im=s5_lam_im, s5_log_dt=s5_log_dt,
             s5_b_re=s5_b_re, s5_b_im=s5_b_im, s5_c_re=s5_c_re, s5_c_im=s5_c_im, s5_d=s5_d,
             norm1_g=norm1_g, norm2_g=norm2_g)
    depth = w_in.shape[0]
    B, T, D = x_prompt.shape
    Bd = x_sample.shape[0]
    assert x_sample.shape[1] == 1
    n_pool = cache_fox_k.shape[1]
    n_pages = page_table.shape[1]
    past = n_pages * PAGE
    tile = min(512, T)
    topk_p = min(TOPK_MAX, T // 4)
    topk_s = min(TOPK_MAX, (past + 1) // 4)
    pt_flat = page_table.reshape(-1).astype(I32)
    page_pre, page_tot = _page_prefix(cache_fox_logf.reshape(depth * n_pool, PAGE * N_HEADS).astype(F32))
    rel = rel_table.astype(F32)
    tz = _t5_tiles(rel)

    n_c = B + Bd
    c_all = jnp.pad(jnp.concatenate([c_prompt, c_sample], axis=0), ((0, (-n_c) % 8), (0, 0)))
    xp = x_prompt.reshape(B * T, D)
    xs = x_sample.reshape(Bd, D)
    rows_p, rows_s = [], []
    for l in range(depth):
        lw = _layer_weights(l, p)
        m = _adaln(c_all, w_ada[l], b_ada[l])
        mods_p = [a[:B, None, :] for a in jnp.split(m, 6, axis=1)]
        mods_s = [a[None, B:n_c, :] for a in jnp.split(m, 6, axis=1)]
        b_f_l = b_f[l].astype(F32)

        def attend_prompt(fq, fk, fk16, fv, fv16, bq, bk, bk16, bv, bv16, iq, small):
            r3 = lambda a: a.reshape(B, T, a.shape[-1])
            small3 = r3(small)
            f_t = jnp.moveaxis(small3[:, :, _FF_LANES], -1, 1).reshape(B * N_HEADS, T)
            logf_t, cum = _gate_cumsum(f_t, jnp.tile(b_f_l, B).reshape(B * N_HEADS, 1))
            logf = jnp.moveaxis(logf_t.reshape(B, N_HEADS, T), 1, -1)
            o_a = _flash("fox", r3(fq), r3(fk16), r3(fv16), cum.reshape(B, N_HEADS, T) * LOG2E, tile=tile)
            ki_t = jnp.swapaxes(small3[:, :, _IK_LANES], 1, 2).astype(BF16)
            mask = _dsa_select(r3(iq), small3, ki_t, topk_p)
            o_b = _flash("dsa", r3(bq), r3(bk16), r3(bv16), mask, tz, tile=tile)
            return o_a.reshape(B * T, W_ATT), o_b.reshape(B * T, W_ATT), logf

        def attend_sample(fq, fk, fk16, fv, fv16, bq, bk, bk16, bv, bv16, iq, small, l=l):
            h3 = lambda a: a.reshape(Bd, N_HEADS, HEAD_DIM)
            logf = _logsig(small[:, _FF_LANES], b_f_l.reshape(1, N_HEADS))
            o_a = _fox_sample(l, pt_flat, h3(fq), h3(fk), h3(fv), logf.reshape(Bd, N_HEADS, 1),
                              cache_fox_k, cache_fox_v, page_pre, page_tot)
            qi3 = iq.reshape(Bd, H_IDX, D_IDX)
            w3 = small[:, _IW_LANES].reshape(Bd, H_IDX, 1)
            sc = _idx_scores_paged(l, pt_flat, qi3, w3, cache_dsa_idx_k)
            ki_new = jnp.pad(small[:, _IK_LANES].reshape(Bd, 1, D_IDX), ((0, 0), (0, LANES - 1), (0, 0)))
            sc_new = _idx_scores_new(qi3, w3, ki_new)
            idx, sb, nb = _dsa_sample_select(sc.reshape(Bd, past), sc_new.reshape(Bd, LANES), rel, topk_s)
            o_b = _dsa_sample_attend(l, idx.reshape(-1), pt_flat, h3(bq), h3(bk), h3(bv),
                                     sb.reshape(Bd, 1, topk_s * N_HEADS), nb.reshape(Bd, N_HEADS, 1),
                                     cache_dsa_k, cache_dsa_v, topk_s)
            return (o_a.reshape(Bd, W_ATT).astype(BF16), o_b.reshape(Bd, W_ATT).astype(BF16), logf)

        zero_state = jnp.zeros((B,) + state_s5_re.shape[2:], F32)
        xp, rp = _layer(xp, mods_p, T, T, lw, zero_state, zero_state, attend_prompt)
        xs, rs = _layer(xs, mods_s, Bd, 1, lw, state_s5_re[l], state_s5_im[l], attend_sample)
        rows_p.append(rp)
        rows_s.append(rs)

    y_prompt = _norm(xp, final_norm_g).reshape(B, T, D)
    y_sample = _norm(xs, final_norm_g).reshape(Bd, 1, D)

    def leaves(rows, nb, nt):
        fk, fv, fl, bk, bv, ik, sr, si = [jnp.stack(a) for a in zip(*rows)]
        hd = (depth, nb, nt, N_HEADS, HEAD_DIM)
        return (fk.reshape(hd), fv.reshape(hd), fl.reshape(depth, nb, nt, N_HEADS), bk.reshape(hd),
                bv.reshape(hd), ik.reshape(depth, nb, nt, D_IDX), sr, si)

    return (y_prompt, y_sample) + leaves(rows_p, B, T) + leaves(rows_s, Bd, 1)
```

```python
import functools
import math

import jax
import jax.numpy as jnp
from jax import lax
from jax.experimental import pallas as pl
from jax.experimental.pallas import tpu as pltpu

F32 = jnp.float32
BF16 = jnp.bfloat16
I32 = jnp.int32

N_HEADS = 8
HEAD_DIM = 128
H_IDX = 16
D_IDX = 64
TOPK_MAX = 256
GROUP = 16
N_STATE = 64
N_BUCKETS = 32
MAX_DIST = 128
N_KEYS = 128
P_HEADS = 8
P_TOPK = 16
PAGE = 128
EPS = 1e-6

V7X_VMEM_LIMIT_BYTES = 56 * 1024 * 1024
LANES = 128
NEG = -1e30
LOG2E = math.log2(math.e)
INT_MIN = -2 ** 31

_PAIRS = sorted([(i, j) for i in range(P_TOPK) for j in range(P_TOPK) if (i + 1) * (j + 1) <= P_TOPK],
                key=lambda p: p[0] * P_TOPK + p[1])
_N_CAND = 56


def _params(*sem):
    return pltpu.CompilerParams(dimension_semantics=sem, vmem_limit_bytes=V7X_VMEM_LIMIT_BYTES)


def _gelu(x):
    return 0.5 * x * (1.0 + jnp.tanh(math.sqrt(2.0 / math.pi) * (x + 0.044715 * (x * x * x))))


def _sigmoid(x):
    return 1.0 / (1.0 + jnp.exp(-x))


def _mm_body(*refs, n_extra, epilogue):
    a_ref, b_ref = refs[0], refs[1]
    extra = refs[2:2 + n_extra]
    outs = refs[2 + n_extra:]
    acc = jnp.dot(a_ref[...].astype(BF16), b_ref[...].astype(BF16), preferred_element_type=F32)
    vals = epilogue(acc, *[e[...] for e in extra]) if epilogue is not None else (acc,) * len(outs)
    for o, v in zip(outs, vals):
        o[...] = v.astype(o.dtype)


def _mm(a, b, out_dtypes, *, tm, tn, order="mn", epilogue=None, extras=(), name="mm", cols=None, layer=None):
    M, K = a.shape
    col0, N = cols if cols is not None else (0, b.shape[-1])
    tm, tn = min(tm, M), min(tn, N)
    assert M % tm == 0 and N % tn == 0 and col0 % tn == 0, (M, N, tm, tn, col0)
    jb0 = col0 // tn
    if order == "mn":
        grid = (M // tm, N // tn)
        ij = lambda g0, g1: (g0, g1)
    else:
        grid = (N // tn, M // tm)
        ij = lambda g0, g1: (g1, g0)
    if layer is None:
        b_spec = pl.BlockSpec((K, tn), lambda g0, g1: (0, jb0 + ij(g0, g1)[1]))
    else:
        b_spec = pl.BlockSpec((None, K, tn), lambda g0, g1: (layer, 0, jb0 + ij(g0, g1)[1]))
    in_specs = [pl.BlockSpec((tm, K), lambda g0, g1: (ij(g0, g1)[0], 0)), b_spec]
    args = [a, b]
    for arr, bshape, imap in extras:
        in_specs.append(pl.BlockSpec(bshape, lambda g0, g1, imap=imap: imap(*ij(g0, g1))))
        args.append(arr)
    out_specs = [pl.BlockSpec((tm, tn), lambda g0, g1: ij(g0, g1)) for _ in out_dtypes]
    out_shape = [jax.ShapeDtypeStruct((M, N), dt) for dt in out_dtypes]
    return pl.pallas_call(
        functools.partial(_mm_body, n_extra=len(extras), epilogue=epilogue),
        name=name, grid=grid, in_specs=in_specs, out_specs=out_specs, out_shape=out_shape,
        compiler_params=_params("parallel", "parallel"),
    )(*args)


def _mod_extra(mod, rows_per_group, tm, tn):
    r = mod.shape[1]
    return (mod, (None, r, tn), lambda i, j: ((i * tm) // rows_per_group, 0, j))


def _adaln_body(c_ref, w_ref, b_ref, o_ref):
    c = c_ref[...]
    s = c * _sigmoid(c)
    o_ref[...] = jnp.dot(s.astype(BF16), w_ref[...].astype(BF16), preferred_element_type=F32) + b_ref[...]


def _adaln(c, w_ada, b_ada, layer, tn=1024):
    R, D = c.shape
    depth, _, N = w_ada.shape
    return pl.pallas_call(
        _adaln_body, name="adaln", grid=(N // tn,),
        in_specs=[pl.BlockSpec((R, D), lambda j: (0, 0)),
                  pl.BlockSpec((None, D, tn), lambda j: (layer, 0, j)),
                  pl.BlockSpec((None, 1, tn), lambda j: (layer, 0, j))],
        out_specs=pl.BlockSpec((R, tn), lambda j: (0, j)),
        out_shape=jax.ShapeDtypeStruct((R, N), F32),
        compiler_params=_params("parallel"),
    )(c, w_ada, b_ada.reshape(depth, 1, N))


def _norm_mod_body(x_ref, g_ref, sc_ref, sh_ref, o_ref):
    x = x_ref[...]
    y = x * lax.rsqrt(jnp.mean(x * x, axis=-1, keepdims=True) + EPS) * g_ref[...]
    o_ref[...] = (y * (1.0 + sc_ref[...]) + sh_ref[...]).astype(o_ref.dtype)


def _norm_mod(x, g, sc, sh, rows_per_group, tm=512):
    M, D = x.shape
    tm = min(tm, M)
    r = sc.shape[1]
    mod_spec = pl.BlockSpec((None, r, D), lambda i: ((i * tm) // rows_per_group, 0, 0))
    return pl.pallas_call(
        _norm_mod_body, name="norm_mod", grid=(M // tm,),
        in_specs=[pl.BlockSpec((tm, D), lambda i: (i, 0)), pl.BlockSpec((1, D), lambda i: (0, 0)),
                  mod_spec, mod_spec],
        out_specs=pl.BlockSpec((tm, D), lambda i: (i, 0)),
        out_shape=jax.ShapeDtypeStruct((M, D), BF16),
        compiler_params=_params("parallel"),
    )(x, g.reshape(1, D), sc, sh)


def _norm_body(x_ref, g_ref, o_ref):
    x = x_ref[...]
    o_ref[...] = x * lax.rsqrt(jnp.mean(x * x, axis=-1, keepdims=True) + EPS) * g_ref[...]


def _norm(x, g, tm=512):
    M, D = x.shape
    tm = min(tm, M)
    return pl.pallas_call(
        _norm_body, name="final_norm", grid=(M // tm,),
        in_specs=[pl.BlockSpec((tm, D), lambda i: (i, 0)), pl.BlockSpec((1, D), lambda i: (0, 0))],
        out_specs=pl.BlockSpec((tm, D), lambda i: (i, 0)),
        out_shape=jax.ShapeDtypeStruct((M, D), F32),
        compiler_params=_params("parallel"),
    )(x, g.reshape(1, D))


def _log_sigmoid(z):
    return jnp.minimum(z, 0.0) - jnp.log(1.0 + jnp.exp(-jnp.abs(z)))


def _gate_body(f_ref, b_ref, logf_ref, cum_ref):
    logf = _log_sigmoid(f_ref[...] + b_ref[...])
    logf_ref[...] = logf
    T = logf.shape[1]
    lane = lax.broadcasted_iota(I32, logf.shape, 1)
    x = logf
    sh = 1
    while sh < T:
        x = x + jnp.where(lane >= sh, pltpu.roll(x, sh, axis=1), 0.0)
        sh *= 2
    cum_ref[...] = x


def _gate_cumsum(f_t, b_col):
    R, T = f_t.shape
    return pl.pallas_call(
        _gate_body, name="fox_gate",
        out_shape=[jax.ShapeDtypeStruct((R, T), F32), jax.ShapeDtypeStruct((R, T), F32)],
        compiler_params=pltpu.CompilerParams(vmem_limit_bytes=V7X_VMEM_LIMIT_BYTES),
    )(f_t, b_col)


def _flash_body(*refs, mode, tile, scale):
    if mode == "fox":
        q_ref, k_ref, v_ref, fk_ref, o_ref, m_sc, acc_sc = refs
    else:
        q_ref, k_ref, v_ref, mask_ref, tz_ref, o_ref, m_sc, acc_sc = refs
    qi = pl.program_id(1)
    ki = pl.program_id(2)

    @pl.when(ki == 0)
    def _():
        m_sc[...] = jnp.full(m_sc.shape, -jnp.inf, F32)
        acc_sc[...] = jnp.zeros(acc_sc.shape, F32)

    nsub = tile // LANES

    def t5_bias(h, near):
        zero = jnp.zeros((LANES, LANES), F32)
        if near == "diag":
            pick = lambda a, b: tz_ref[h, 0] if a == b else (tz_ref[h, 1] if a == b + 1 else zero)
        else:
            pick = lambda a, b: tz_ref[h, 1] if (a == 0 and b == nsub - 1) else zero
        rows = [jnp.concatenate([pick(a, b) for b in range(nsub)], axis=1) if nsub > 1 else pick(a, 0)
                for a in range(nsub)]
        return jnp.concatenate(rows, axis=0) if nsub > 1 else rows[0]

    def step(near):
        if mode == "fox":
            if near == "diag":
                row = lax.broadcasted_iota(I32, (tile, tile), 0)
                col = lax.broadcasted_iota(I32, (tile, tile), 1)
                keep = row >= col
        else:
            shared = mask_ref[0]
        ones = jnp.ones((tile, HEAD_DIM), BF16)
        for h in range(N_HEADS):
            sl = slice(h * HEAD_DIM, (h + 1) * HEAD_DIM)
            s = lax.dot_general(q_ref[0, :, sl], k_ref[0, :, sl], (((1,), (1,)), ((), ())),
                                preferred_element_type=F32) * (scale * LOG2E)
            if mode == "fox":
                s = s - fk_ref[0, h:h + 1, :]
                if near == "diag":
                    s = jnp.where(keep, s, NEG)
            else:
                s = s + shared
                if near != "far":
                    s = s + t5_bias(h, near)
            m_prev = m_sc[h]
            m_new = jnp.maximum(m_prev, jnp.max(s, axis=1, keepdims=True))
            alpha = jnp.exp2(m_prev - m_new)
            p = jnp.exp2(s - jnp.concatenate([m_new] * nsub, axis=1))
            pv = jnp.dot(p.astype(BF16), jnp.concatenate([v_ref[0, :, sl], ones], axis=1),
                         preferred_element_type=F32)
            acc_sc[h] = jnp.concatenate([alpha] * (2 * HEAD_DIM // LANES), axis=1) * acc_sc[h] + pv
            m_sc[h] = m_new

    if mode == "fox":
        pl.when(ki < qi)(lambda: step("far"))
        pl.when(ki == qi)(lambda: step("diag"))
    else:
        pl.when(ki < qi - 1)(lambda: step("far"))
        pl.when(ki == qi - 1)(lambda: step("next"))
        pl.when(ki == qi)(lambda: step("diag"))

    @pl.when(ki == qi)
    def _():
        for h in range(N_HEADS):
            sl = slice(h * HEAD_DIM, (h + 1) * HEAD_DIM)
            o_ref[0, :, sl] = (acc_sc[h, :, :HEAD_DIM] / acc_sc[h, :, HEAD_DIM:]).astype(o_ref.dtype)


def _flash(mode, q, k, v, *side, tile):
    B, T, W = q.shape
    nt = T // tile
    qspec = pl.BlockSpec((1, tile, W), lambda b, qi, ki: (b, qi, 0))
    kspec = pl.BlockSpec((1, tile, W), lambda b, qi, ki: (b, jnp.minimum(ki, qi), 0))
    if mode == "fox":
        side_specs = [pl.BlockSpec((1, N_HEADS, tile), lambda b, qi, ki: (b, 0, jnp.minimum(ki, qi)))]
    else:
        side_specs = [pl.BlockSpec((1, tile, tile), lambda b, qi, ki: (b, qi, jnp.minimum(ki, qi))),
                      pl.BlockSpec((N_HEADS, 2, LANES, LANES), lambda b, qi, ki: (0, 0, 0, 0))]
    return pl.pallas_call(
        functools.partial(_flash_body, mode=mode, tile=tile, scale=HEAD_DIM ** -0.5),
        name="flash_" + mode, grid=(B, nt, nt),
        in_specs=[qspec, kspec, kspec] + side_specs,
        out_specs=pl.BlockSpec((1, tile, W), lambda b, qi, ki: (b, qi, 0)),
        out_shape=jax.ShapeDtypeStruct((B, T, W), BF16),
        scratch_shapes=[pltpu.VMEM((N_HEADS, tile, LANES), F32), pltpu.VMEM((N_HEADS, tile, 2 * HEAD_DIM), F32)],
        compiler_params=_params("parallel", "parallel", "arbitrary"),
    )(q, k, v, *side)


def _sortable_key(x):
    bits = pltpu.bitcast(x, I32)
    return jnp.where(bits < 0, bits ^ jnp.int32(0x7FFFFFFF), bits)


def _kth_largest_key(count_ge, shape, k):
    def body(b, cur):
        cand = cur | lax.shift_left(jnp.int32(1), 31 - b)
        cnt = count_ge(cand ^ jnp.int32(INT_MIN))
        return jnp.where(cnt >= k, cand, cur)
    cur = lax.fori_loop(0, 32, body, jnp.zeros(shape, I32))
    return cur ^ jnp.int32(INT_MIN)


def _dsa_select_body(qi_ref, small_ref, kit_ref, mask_ref, key_sc, cut_sc, *, tq, cw, bw, T, topk):
    q0 = pl.program_id(1) * tq
    w = small_ref[0][:, N_HEADS:N_HEADS + H_IDX] * (H_IDX ** -0.5 * D_IDX ** -0.5)
    bucket = (q0 + tq + bw - 1) // bw
    n_valid = bucket * (bw // cw)
    row_c = q0 + lax.broadcasted_iota(I32, (tq, cw), 0)
    lane_c = lax.broadcasted_iota(I32, (tq, cw), 1)

    def score_chunk(c, carry):
        c0 = pl.multiple_of(c * cw, cw)
        kt = kit_ref[0, :, pl.ds(c0, cw)]
        acc = jnp.zeros((tq, cw), F32)
        for h in range(H_IDX):
            d = jnp.dot(qi_ref[0, :, h * D_IDX:(h + 1) * D_IDX], kt, preferred_element_type=F32)
            acc = acc + w[:, h:h + 1] * jnp.maximum(d, 0.0)
        key_sc[:, pl.ds(c0, cw)] = jnp.where(c0 + lane_c <= row_c, _sortable_key(acc), jnp.int32(INT_MIN))
        return carry

    lax.fori_loop(0, n_valid, score_chunk, 0)

    lane = lax.broadcasted_iota(I32, (tq, LANES), 1)
    row = q0 + lax.broadcasted_iota(I32, (tq, LANES), 0)
    one = lambda pred: jnp.where(pred, 1, 0)

    def select(n_lt):
        def count(ind):
            part = jnp.zeros((tq, LANES), I32)
            for c in range(n_lt):
                part = part + ind(key_sc[:, c * LANES:(c + 1) * LANES], c * LANES + lane)
            return jnp.sum(part, axis=1, keepdims=True)

        thr = _kth_largest_key(lambda t: count(lambda k, col: one(k >= t)), (tq, 1), topk)
        n_gt = count(lambda k, col: one(k > thr))
        n_ge = count(lambda k, col: one(k >= thr))
        excess = jnp.where(thr > jnp.int32(INT_MIN), n_ge - topk, 0)
        cut_sc[...] = jnp.full((tq, 1), T, I32)

        @pl.when(jnp.max(excess) > 0)
        def _():
            need = topk - n_gt
            nbits = max(1, (T - 1).bit_length())

            def body(b, cut):
                cand = cut | lax.shift_left(jnp.int32(1), nbits - 1 - b)
                cnt = count(lambda k, col: jnp.where(k == thr, one(col < cand), 0))
                return jnp.where(cnt < need, cand, cut)
            cut = lax.fori_loop(0, nbits, body, jnp.zeros((tq, 1), I32))
            cut_sc[...] = jnp.where(excess > 0, cut, T)

        cut = cut_sc[...]
        for c in range(n_lt):
            k = key_sc[:, c * LANES:(c + 1) * LANES]
            col = c * LANES + lane
            val = jnp.where(k > thr, 0.0, jnp.where(k == thr, jnp.where(col <= cut, 0.0, NEG), NEG))
            mask_ref[0, :, c * LANES:(c + 1) * LANES] = jnp.where(col <= row, val, NEG)
        if n_lt * LANES < T:
            mask_ref[0, :, n_lt * LANES:] = jnp.full((tq, T - n_lt * LANES), NEG, F32)

    for k in range(1, T // bw + 1):
        pl.when(bucket == k)(functools.partial(select, k * bw // LANES))


def _dsa_select(qi, small, ki_t, topk, tq=256, cw=256, bw=512):
    B, T, _ = qi.shape
    tq, cw, bw = min(tq, T), min(cw, T), min(bw, T)
    assert bw % cw == 0 and T % bw == 0
    return pl.pallas_call(
        functools.partial(_dsa_select_body, tq=tq, cw=cw, bw=bw, T=T, topk=topk),
        name="dsa_select", grid=(B, T // tq),
        in_specs=[pl.BlockSpec((1, tq, H_IDX * D_IDX), lambda b, i: (b, i, 0)),
                  pl.BlockSpec((1, tq, LANES), lambda b, i: (b, i, 0)),
                  pl.BlockSpec((1, D_IDX, T), lambda b, i: (b, 0, 0))],
        out_specs=pl.BlockSpec((1, tq, T), lambda b, i: (b, i, 0)),
        out_shape=jax.ShapeDtypeStruct((B, T, T), F32),
        scratch_shapes=[pltpu.VMEM((tq, T), I32), pltpu.VMEM((tq, 1), I32)],
        compiler_params=_params("parallel", "parallel"),
    )(qi, small, ki_t)


def _t5_bucket(dist):
    n = jnp.maximum(dist, 0)
    max_exact = N_BUCKETS // 2
    nf = jnp.maximum(n, 1).astype(F32)
    large = max_exact + (jnp.log(nf / max_exact) / math.log(MAX_DIST / max_exact)
                         * (N_BUCKETS - max_exact)).astype(I32)
    large = jnp.minimum(large, N_BUCKETS - 1)
    return jnp.where(n < max_exact, n, large)


def _t5_tiles(rel_table):
    assert LANES >= MAX_DIST
    i = jnp.arange(LANES)
    bucket = _t5_bucket((jnp.arange(2) * LANES)[:, None, None] + i[None, :, None] - i[None, None, :])
    rel = rel_table.astype(F32)
    tz = jnp.zeros((rel.shape[1],) + bucket.shape, F32)
    for k in range(N_BUCKETS - 1):
        tz = tz + jnp.where(bucket[None] == k, (rel[k] - rel[N_BUCKETS - 1])[:, None, None, None], 0.0)
    return tz * LOG2E


def _s5_disc_body(lr_ref, li_ref, ldt_ref, br_ref, bi_ref, abr_ref, abi_ref, bbr_ref, bbi_ref):
    lr, li = lr_ref[...], li_ref[...]
    dt = jnp.exp(ldt_ref[...])
    mag = jnp.exp(lr * dt)
    ab_re, ab_im = mag * jnp.cos(li * dt), mag * jnp.sin(li * dt)
    den = lr * lr + li * li
    nr = ab_re - 1.0
    k_re = (nr * lr + ab_im * li) / den
    k_im = (ab_im * lr - nr * li) / den
    br, bi = br_ref[...], bi_ref[...]
    abr_ref[...] = ab_re
    abi_ref[...] = ab_im
    bbr_ref[...] = k_re * br - k_im * bi
    bbi_ref[...] = k_re * bi + k_im * br


def _s5_discretise(lam_re, lam_im, log_dt, b_re, b_im):
    G, N, P = b_re.shape
    rep = lambda a: jnp.broadcast_to(a.reshape(G * N, 1), (G * N, P))
    ldt = jnp.broadcast_to(log_dt.reshape(G, 1, 1), (G, N, P)).reshape(G * N, P)
    shp = jax.ShapeDtypeStruct((G * N, P), F32)
    abr, abi, bbr, bbi = pl.pallas_call(
        _s5_disc_body, name="s5_discretise", out_shape=[shp] * 4,
        compiler_params=pltpu.CompilerParams(vmem_limit_bytes=V7X_VMEM_LIMIT_BYTES),
    )(rep(lam_re), rep(lam_im), ldt, b_re.reshape(G * N, P), b_im.reshape(G * N, P))
    return (abr[:, 0].reshape(G, N), abi[:, 0].reshape(G, N),
            bbr.reshape(G, N, P), bbi.reshape(G, N, P))


def _s5_body(u_ref, bc_ref, cc_ref, a_ref, d_ref, x0_ref, y_ref, xT_ref, xs_sc, carry_sc, *, tc, S, lc):
    c = pl.program_id(1)

    @pl.when(c == 0)
    def _():
        carry_sc[...] = x0_ref[0]

    n_ct = u_ref.shape[2] // LANES
    ks = S // n_ct
    ch = lambda t: slice(t * LANES, (t + 1) * LANES)
    st = lambda j: slice(j * ks, (j + 1) * ks)
    for j in range(2 * n_ct):
        xs_sc[:, st(j)] = jnp.dot(u_ref[0, :, ch(j % n_ct)].astype(BF16), bc_ref[j], preferred_element_type=F32)

    for j in range(S // lc):
        slr = slice(j * lc, (j + 1) * lc)
        sli = slice(S + j * lc, S + (j + 1) * lc)
        ar, ai = a_ref[:, slr], a_ref[:, sli]

        def step(t, carry, slr=slr, sli=sli, ar=ar, ai=ai):
            xr, xi = carry
            nr = ar * xr - ai * xi + xs_sc[pl.ds(t, 1), slr]
            ni = ar * xi + ai * xr + xs_sc[pl.ds(t, 1), sli]
            xs_sc[pl.ds(t, 1), slr] = nr
            xs_sc[pl.ds(t, 1), sli] = ni
            return nr, ni

        xr, xi = lax.fori_loop(0, tc, step, (carry_sc[:, slr], carry_sc[:, sli]), unroll=min(8, tc))
        carry_sc[:, slr] = xr
        carry_sc[:, sli] = xi

    for t in range(n_ct):
        y = (jnp.dot(xs_sc[:, st(t)].astype(BF16), cc_ref[t], preferred_element_type=F32)
             + jnp.dot(xs_sc[:, st(n_ct + t)].astype(BF16), cc_ref[n_ct + t], preferred_element_type=F32)
             + d_ref[:, ch(t)] * u_ref[0, :, ch(t)])
        y_ref[0, :, ch(t)] = _gelu(y).astype(y_ref.dtype)
    xT_ref[0] = carry_sc[...]


def _glu_body(y_ref, wa_ref, wb_ref, o_ref):
    y = y_ref[...]
    a = jnp.dot(y, wa_ref[...], preferred_element_type=F32)
    b = jnp.dot(y, wb_ref[...], preferred_element_type=F32)
    o_ref[...] = (a * _sigmoid(b)).astype(o_ref.dtype)


def _glu(y, wa, wb, tm=512):
    M, K = y.shape
    N = wa.shape[1]
    tm = min(tm, M)
    return pl.pallas_call(
        _glu_body, name="s5_glu", grid=(M // tm,),
        in_specs=[pl.BlockSpec((tm, K), lambda i: (i, 0)), pl.BlockSpec((K, N), lambda i: (0, 0)),
                  pl.BlockSpec((K, N), lambda i: (0, 0))],
        out_specs=pl.BlockSpec((tm, N), lambda i: (i, 0)),
        out_shape=jax.ShapeDtypeStruct((M, N), BF16),
        compiler_params=_params("parallel"),
    )(y, wa, wb)


def _merge_body(oa_ref, ob_ref, oc_ref, wa_ref, wb_ref, wc_ref, ga_ref, gb_ref, gc_ref, o_ref):
    dot = lambda x, w: jnp.dot(x[...], w[...], preferred_element_type=F32)
    m = (_sigmoid(ga_ref[...]) * dot(oa_ref, wa_ref) + _sigmoid(gb_ref[...]) * dot(ob_ref, wb_ref)
         + _sigmoid(gc_ref[...]) * dot(oc_ref, wc_ref))
    o_ref[...] = m.astype(o_ref.dtype)


def _merge(o_a, o_b, o_c, wa, wb, wc, gl, tm=512, tn=512):
    M, K = o_a.shape
    D = wa.shape[1]
    tm = min(tm, M)
    nd = D // tn
    ospec = pl.BlockSpec((tm, K), lambda i, j: (i, 0))
    wspec = pl.BlockSpec((K, tn), lambda i, j: (0, j))
    gspec = lambda g: pl.BlockSpec((tm, tn), lambda i, j, g=g: (i, g * nd + j))
    return pl.pallas_call(
        _merge_body, name="branch_merge", grid=(M // tm, nd),
        in_specs=[ospec, ospec, ospec, wspec, wspec, wspec, gspec(0), gspec(1), gspec(2)],
        out_specs=pl.BlockSpec((tm, tn), lambda i, j: (i, j)),
        out_shape=jax.ShapeDtypeStruct((M, D), BF16),
        compiler_params=_params("parallel", "parallel"),
    )(o_a, o_b, o_c, wa, wb, wc, gl, gl, gl)


def _top16_rows(s, n_rows):
    iota = lax.broadcasted_iota(I32, s.shape, 0)
    rank = jnp.full(s.shape, P_TOPK, I32)
    vals = []
    work = s
    for k in range(P_TOPK):
        m = jnp.max(work, axis=0, keepdims=True)
        idx = jnp.min(jnp.where(work == m, iota, n_rows), axis=0, keepdims=True)
        hit = iota == idx
        rank = jnp.where(hit, k, rank)
        work = jnp.where(hit, -jnp.inf, work)
        vals.append(m)
    return rank, vals


def _peer_select_body(q_ref, keys_ref, a1_ref, n1_ref, a2_ref, r2_ref, cand_sc, sel_sc):
    tt = q_ref.shape[0]
    nt = (((1,), (1,)), ((), ()))
    for h in range(P_HEADS):
        s1 = lax.dot_general(keys_ref[2 * h], q_ref[:, (2 * h) * N_KEYS:(2 * h + 1) * N_KEYS], nt,
                             preferred_element_type=F32)
        s2 = lax.dot_general(keys_ref[2 * h + 1], q_ref[:, (2 * h + 1) * N_KEYS:(2 * h + 2) * N_KEYS], nt,
                             preferred_element_type=F32)
        r1, v1 = _top16_rows(s1, N_KEYS)
        r2, v2 = _top16_rows(s2, N_KEYS)
        for r, (i, j) in enumerate(_PAIRS):
            cand_sc[r:r + 1, :] = v1[i] + v2[j]
        cand_sc[len(_PAIRS):, :] = jnp.full((_N_CAND - len(_PAIRS), tt), -jnp.inf, F32)
        rc, cv = _top16_rows(cand_sc[...], _N_CAND)
        z = jnp.zeros((1, tt), F32)
        for k in range(P_TOPK):
            z = z + jnp.exp(cv[k] - cv[0])
        sel_sc[...] = jnp.where(rc < P_TOPK, 1, 0)
        cnt = [jnp.zeros((1, tt), I32) for _ in range(P_TOPK)]
        for r, (i, j) in enumerate(_PAIRS):
            cnt[i] = cnt[i] + sel_sc[r:r + 1, :]
        n1 = jnp.zeros((N_KEYS, tt), I32)
        for i in range(P_TOPK):
            n1 = jnp.where(r1 == i, cnt[i], n1)
        a1_ref[h] = jnp.exp(s1 - v1[0]) / z
        n1_ref[h] = n1.astype(F32)
        a2_ref[h] = jnp.exp(s2 - v2[0]).astype(a2_ref.dtype)
        r2_ref[h] = r2.astype(F32).astype(r2_ref.dtype)


def _peer_select(q, keys, tt=256):
    M = q.shape[0]
    tt = min(tt, M)
    tab = pl.BlockSpec((P_HEADS, N_KEYS, tt), lambda i: (0, 0, i))
    shp = lambda dt: jax.ShapeDtypeStruct((P_HEADS, N_KEYS, M), dt)
    return pl.pallas_call(
        _peer_select_body, name="peer_select", grid=(M // tt,),
        in_specs=[pl.BlockSpec((tt, q.shape[1]), lambda i: (i, 0)),
                  pl.BlockSpec(keys.shape, lambda i: (0, 0, 0))],
        out_specs=[tab, tab, tab, tab],
        out_shape=[shp(F32), shp(F32), shp(BF16), shp(BF16)],
        scratch_shapes=[pltpu.VMEM((_N_CAND, tt), F32), pltpu.VMEM((_N_CAND, tt), I32)],
        compiler_params=_params("parallel"),
    )(q, keys)


def _peer_dense_body(h_ref, u_ref, vt_ref, a1_ref, n1_ref, a2_ref, r2_ref, x_ref, gt_ref, o_ref,
                     acc_sc, act_sc, ga_sc, *, te):
    j = pl.program_id(1)

    @pl.when(j == 0)
    def _():
        acc_sc[...] = jnp.zeros(acc_sc.shape, F32)
        act_sc[1] = jnp.zeros(act_sc.shape[1:], F32)

    def step(slot):
        act_sc[slot] = lax.dot_general(u_ref[...], h_ref[...], (((1,), (1,)), ((), ())),
                                       preferred_element_type=F32)
        for r in range(te // N_KEYS):
            rows = slice(r * N_KEYS, (r + 1) * N_KEYS)
            wt = None
            for h in range(P_HEADS):
                a1 = a1_ref[h, r:r + 1, :].astype(BF16)
                n1 = n1_ref[h, r:r + 1, :].astype(BF16)
                term = jnp.where(r2_ref[h] < n1, a1 * a2_ref[h], jnp.zeros((), BF16))
                wt = term if wt is None else wt + term
            ga_sc[rows, :] = wt * _gelu(act_sc[1 - slot, rows, :]).astype(BF16)
        acc_sc[...] += jnp.dot(vt_ref[...], ga_sc[...], preferred_element_type=F32)

    pl.when(lax.rem(j, 2) == 0)(functools.partial(step, 0))
    pl.when(lax.rem(j, 2) == 1)(functools.partial(step, 1))

    @pl.when(j == pl.num_programs(1) - 1)
    def _():
        o_ref[...] = x_ref[...] + gt_ref[...] * acc_sc[...].T


def _peer_dense(h2, u, v_t, layer, tabs, x, gt, rows_per_group, tt=512, te=1024):
    M, D = h2.shape
    E = u.shape[1]
    tt = min(tt, M)
    n_e1 = te // N_KEYS
    nc = E // te
    chunk = lambda j: jnp.clip(j, 0, nc - 1)
    tab1 = pl.BlockSpec((P_HEADS, n_e1, tt), lambda i, j: (0, chunk(j - 1), i))
    tab2 = pl.BlockSpec((P_HEADS, N_KEYS, tt), lambda i, j: (0, 0, i))
    r = gt.shape[1]
    return pl.pallas_call(
        functools.partial(_peer_dense_body, te=te),
        name="peer_dense", grid=(M // tt, nc + 1),
        in_specs=[pl.BlockSpec((tt, D), lambda i, j: (i, 0)),
                  pl.BlockSpec((None, te, D), lambda i, j: (layer, chunk(j), 0)),
                  pl.BlockSpec((None, D, te), lambda i, j: (layer, 0, chunk(j - 1))),
                  tab1, tab1, tab2, tab2,
                  pl.BlockSpec((tt, D), lambda i, j: (i, 0)),
                  pl.BlockSpec((None, r, D), lambda i, j: ((i * tt) // rows_per_group, 0, 0))],
        out_specs=pl.BlockSpec((tt, D), lambda i, j: (i, 0)),
        out_shape=jax.ShapeDtypeStruct((M, D), F32),
        scratch_shapes=[pltpu.VMEM((D, tt), F32), pltpu.VMEM((2, te, tt), F32), pltpu.VMEM((te, tt), BF16)],
        compiler_params=_params("parallel", "arbitrary"),
    )(h2, u, v_t, *tabs, x, gt)


def _peer(h2, x, gt, rows_per_group, wq, keys, u, v_t, layer):
    (q,) = _mm(h2, wq, [BF16], tm=512, tn=1024, name="peer_query")
    tabs = _peer_select(q, keys)
    return _peer_dense(h2, u, v_t, layer, tabs, x, gt, rows_per_group)


def _s5_prepare(lam_re, lam_im, log_dt, b_re, b_im, c_re, c_im):
    G, N, P = b_re.shape
    S, W = G * N, G * P
    ab_re, ab_im, bb_re, bb_im = _s5_discretise(lam_re, lam_im, log_dt, b_re, b_im)
    cg = LANES // P
    n_ct = G // cg
    eye = jnp.eye(cg, dtype=F32)

    def tiles(m, rows, cols):
        t = jnp.swapaxes(m, 1, 2).reshape(n_ct, cg, m.shape[2], m.shape[1])
        return (t[:, :, :, None, :] * eye[None, :, None, :, None]).reshape(n_ct, rows, cols)

    bc = jnp.concatenate([tiles(bb_re, LANES, cg * N), tiles(bb_im, LANES, cg * N)], axis=0).astype(BF16)
    cc = jnp.concatenate([tiles(c_re.astype(F32), cg * N, LANES), -tiles(c_im.astype(F32), cg * N, LANES)],
                         axis=0).astype(BF16)
    a_row = jnp.concatenate([ab_re.reshape(1, S), ab_im.reshape(1, S)], axis=1)
    return a_row, bc, cc


def _s5(u, T, x0_re, x0_im, prep, d_skip, tc=128, lc=1024):
    M, W = u.shape
    B = M // T
    G, N = x0_re.shape[1:]
    S = G * N
    a_row, bc, cc = prep
    tc = min(tc, T)
    x0 = jnp.concatenate([x0_re.reshape(B, 1, S), x0_im.reshape(B, 1, S)], axis=2).astype(F32)
    yg, x_last = pl.pallas_call(
        functools.partial(_s5_body, tc=tc, S=S, lc=lc),
        name="s5", grid=(B, T // tc),
        in_specs=[pl.BlockSpec((1, tc, W), lambda b, c: (b, c, 0)),
                  pl.BlockSpec(bc.shape, lambda b, c: (0, 0, 0)),
                  pl.BlockSpec(cc.shape, lambda b, c: (0, 0, 0)),
                  pl.BlockSpec((1, 2 * S), lambda b, c: (0, 0)),
                  pl.BlockSpec((1, W), lambda b, c: (0, 0)),
                  pl.BlockSpec((1, 1, 2 * S), lambda b, c: (b, 0, 0))],
        out_specs=[pl.BlockSpec((1, tc, W), lambda b, c: (b, c, 0)),
                   pl.BlockSpec((1, 1, 2 * S), lambda b, c: (b, 0, 0))],
        out_shape=[jax.ShapeDtypeStruct((B, T, W), BF16), jax.ShapeDtypeStruct((B, 1, 2 * S), F32)],
        scratch_shapes=[pltpu.VMEM((tc, 2 * S), F32), pltpu.VMEM((1, 2 * S), F32)],
        compiler_params=_params("parallel", "arbitrary"),
    )(u.reshape(B, T, W), bc, cc, a_row, d_skip.reshape(1, W).astype(F32), x0)
    return yg.reshape(M, W), x_last[:, 0, :S].reshape(B, G, N), x_last[:, 0, S:].reshape(B, G, N)


def _logsig_body(f_ref, b_ref, o_ref):
    o_ref[...] = _log_sigmoid(f_ref[...] + b_ref[...])


def _logsig(f, b_row):
    return pl.pallas_call(_logsig_body, out_shape=jax.ShapeDtypeStruct(f.shape, F32))(f, b_row)


def _head_lane_select(n_lanes):
    lane = lax.broadcasted_iota(I32, (N_HEADS, n_lanes), 1)
    head = lax.broadcasted_iota(I32, (N_HEADS, n_lanes), 0)
    return (lane & (N_HEADS - 1)) == head, lane


def _page_prefix_body(lf_ref, pre_ref, tot_ref):
    lf = lf_ref[...]
    n_lanes = lf.shape[1]
    lane = lax.broadcasted_iota(I32, lf.shape, 1)
    pre, tot = lf, lf
    sh = N_HEADS
    while sh < n_lanes:
        pre = pre + jnp.where(lane >= sh, pltpu.roll(pre, sh, axis=1), 0.0)
        tot = tot + pltpu.roll(tot, sh, axis=1)
        sh *= 2
    pre_ref[...] = pre
    tot_ref[...] = tot


def _page_prefix(logf_rows):
    R, n_lanes = logf_rows.shape
    tr = math.gcd(R, 256)
    spec = pl.BlockSpec((tr, n_lanes), lambda i: (i, 0))
    return pl.pallas_call(
        _page_prefix_body, name="fox_page_prefix", grid=(R // tr,), in_specs=[spec], out_specs=[spec, spec],
        out_shape=[jax.ShapeDtypeStruct((R, n_lanes), F32)] * 2,
        compiler_params=_params("parallel"),
    )(logf_rows)


def _fox_sample_body(pt_ref, q_ref, kn_ref, vn_ref, lfn_ref, k_hbm, v_hbm, pre_hbm, tot_hbm, o_ref,
                     kbuf, vbuf, pbuf, tbuf, sems, m_sc, l_sc, acc_sc, carry_sc,
                     *, layer, n_pool, n_pages, pp, scale):
    g = pl.program_id(1)
    ng = pl.num_programs(1)
    step = pl.program_id(0) * ng + g
    n_steps = pl.num_programs(0) * ng
    slot = lax.rem(step, 2)
    page_lanes = PAGE * N_HEADS
    n_lanes = pp * page_lanes
    nt = (((1,), (1,)), ((), ()))

    def copies(st, sl):
        first = (st // ng) * n_pages + lax.rem(st, ng) * pp
        out = []
        for k in range(pp):
            pg = pt_ref[first + k]
            row = layer * n_pool + pg
            out += [pltpu.make_async_copy(k_hbm.at[layer, pg], kbuf.at[sl, pl.ds(k * PAGE, PAGE)], sems.at[0, sl]),
                    pltpu.make_async_copy(v_hbm.at[layer, pg], vbuf.at[sl, pl.ds(k * PAGE, PAGE)], sems.at[1, sl]),
                    pltpu.make_async_copy(pre_hbm.at[pl.ds(row, 1)], pbuf.at[sl, pl.ds(k, 1)], sems.at[2, sl]),
                    pltpu.make_async_copy(tot_hbm.at[pl.ds(row, 1)], tbuf.at[sl, pl.ds(k, 1)], sems.at[3, sl])]
        return out

    @pl.when(step == 0)
    def _():
        for c in copies(step, slot):
            c.start()

    @pl.when(step + 1 < n_steps)
    def _():
        for c in copies(step + 1, 1 - slot):
            c.start()

    @pl.when(g == 0)
    def _():
        m_sc[...] = jnp.full(m_sc.shape, -jnp.inf, F32)
        l_sc[...] = jnp.zeros(l_sc.shape, F32)
        acc_sc[...] = jnp.zeros(acc_sc.shape, F32)
        carry_sc[...] = jnp.zeros(carry_sc.shape, F32)

    for c in copies(step, slot):
        c.wait()

    hsel, lane8 = _head_lane_select(n_lanes)
    q = q_ref[0]
    kp = kbuf[slot].reshape(n_lanes, HEAD_DIM).astype(BF16)
    s = lax.dot_general(q, kp, nt, preferred_element_type=F32) * scale
    carry = carry_sc[...]
    f_pages = []
    for k in range(pp):
        f_pages.append(carry + pbuf[slot, k:k + 1, :])
        carry = carry + tbuf[slot, k:k + 1, :]
    carry_sc[...] = carry
    f_k = jnp.concatenate(f_pages, axis=1) if pp > 1 else f_pages[0]
    s = jnp.where(hsel, s - f_k, NEG)
    m_prev = m_sc[...]
    m_new = jnp.maximum(m_prev, jnp.max(s, axis=1, keepdims=True))
    alpha = jnp.exp(m_prev - m_new)
    pr = jnp.exp(s - m_new)
    l_sc[...] = alpha * l_sc[...] + jnp.sum(pr, axis=1, keepdims=True)
    vp = vbuf[slot].reshape(n_lanes, HEAD_DIM).astype(BF16)
    acc_sc[...] = alpha * acc_sc[...] + jnp.dot(pr.astype(BF16), vp, preferred_element_type=F32)
    m_sc[...] = m_new

    @pl.when(g == ng - 1)
    def _():
        hsel_p, lane_p = _head_lane_select(page_lanes)
        f_col = jnp.sum(jnp.where(hsel_p, jnp.where(lane_p < N_HEADS, carry_sc[...], 0.0), 0.0),
                        axis=1, keepdims=True)
        kn = kn_ref[0].astype(BF16).astype(F32)
        vn = vn_ref[0].astype(BF16).astype(F32)
        s_n = jnp.sum(q.astype(F32) * kn, axis=1, keepdims=True) * scale - (f_col + lfn_ref[0])
        m_prev = m_sc[...]
        m_new = jnp.maximum(m_prev, s_n)
        alpha = jnp.exp(m_prev - m_new)
        p_n = jnp.exp(s_n - m_new)
        l = alpha * l_sc[...] + p_n
        acc = alpha * acc_sc[...] + p_n.astype(BF16).astype(F32) * vn
        o_ref[0] = acc / l


def _fox_sample(layer, pt_flat, q, k_new, v_new, logf_new, cache_k, cache_v, page_pre, page_tot, pp=8):
    Bd = q.shape[0]
    n_pool = cache_k.shape[1]
    n_pages = pt_flat.shape[0] // Bd
    pp = math.gcd(pp, n_pages)
    tok = lambda w: pl.BlockSpec((1, N_HEADS, w), lambda b, g, pt: (b, 0, 0))
    hbm = pl.BlockSpec(memory_space=pl.ANY)
    grid_spec = pltpu.PrefetchScalarGridSpec(
        num_scalar_prefetch=1, grid=(Bd, n_pages // pp),
        in_specs=[tok(HEAD_DIM), tok(HEAD_DIM), tok(HEAD_DIM), tok(1), hbm, hbm, hbm, hbm],
        out_specs=pl.BlockSpec((1, N_HEADS, HEAD_DIM), lambda b, g, pt: (b, 0, 0)),
        scratch_shapes=[pltpu.VMEM((2, pp * PAGE, N_HEADS, HEAD_DIM), F32),
                        pltpu.VMEM((2, pp * PAGE, N_HEADS, HEAD_DIM), F32),
                        pltpu.VMEM((2, pp, PAGE * N_HEADS), F32), pltpu.VMEM((2, pp, PAGE * N_HEADS), F32),
                        pltpu.SemaphoreType.DMA((4, 2)),
                        pltpu.VMEM((N_HEADS, 1), F32), pltpu.VMEM((N_HEADS, 1), F32),
                        pltpu.VMEM((N_HEADS, HEAD_DIM), F32), pltpu.VMEM((1, PAGE * N_HEADS), F32)])
    return pl.pallas_call(
        functools.partial(_fox_sample_body, layer=layer, n_pool=n_pool, n_pages=n_pages, pp=pp,
                          scale=HEAD_DIM ** -0.5),
        name="fox_sample", grid_spec=grid_spec,
        out_shape=jax.ShapeDtypeStruct((Bd, N_HEADS, HEAD_DIM), F32),
        compiler_params=_params("arbitrary", "arbitrary"),
    )(pt_flat, q, k_new, v_new, logf_new, cache_k, cache_v, page_pre, page_tot)


def _idx_scores(qi, w, ki, keys_on_lanes=False):
    contract = (((1,), (0,)), ((), ())) if keys_on_lanes else (((1,), (1,)), ((), ()))
    d = lax.dot_general(qi, ki.astype(BF16), contract, preferred_element_type=F32)
    return jnp.sum(w * (H_IDX ** -0.5 * D_IDX ** -0.5) * jnp.maximum(d, 0.0), axis=0, keepdims=True)


def _idx_scores_body(qi_ref, w_ref, ki_ref, o_ref):
    o_ref[0] = _idx_scores(qi_ref[0], w_ref[0], ki_ref[...])


def _idx_scores_paged_body(pt_ref, qi_ref, w_ref, ki_hbm, o_ref, buf, sem, *, layer, n_pages, chunk):
    b = pl.program_id(0)

    def page_copy(p):
        return pltpu.make_async_copy(ki_hbm.at[layer, pt_ref[b * n_pages + p]],
                                     buf.at[:, pl.ds(pl.multiple_of(p * PAGE, PAGE), PAGE)], sem.at[0])

    def start(p, c):
        page_copy(p).start()
        return c

    def wait(p, c):
        page_copy(p).wait()
        return c

    lax.fori_loop(0, n_pages, start, 0)
    lax.fori_loop(0, n_pages, wait, 0)
    for c in range(n_pages * PAGE // chunk):
        o_ref[0, :, c * chunk:(c + 1) * chunk] = _idx_scores(qi_ref[0], w_ref[0], buf[:, c * chunk:(c + 1) * chunk],
                                                             keys_on_lanes=True)


def _idx_scores_paged(layer, pt_flat, qi, w, cache_ki_t, chunk=2048):
    Bd = qi.shape[0]
    n_pages = pt_flat.shape[0] // Bd
    chunk = math.gcd(chunk, n_pages * PAGE)
    grid_spec = pltpu.PrefetchScalarGridSpec(
        num_scalar_prefetch=1, grid=(Bd,),
        in_specs=[pl.BlockSpec((1, H_IDX, D_IDX), lambda b, pt: (b, 0, 0)),
                  pl.BlockSpec((1, H_IDX, 1), lambda b, pt: (b, 0, 0)),
                  pl.BlockSpec(memory_space=pl.ANY)],
        out_specs=pl.BlockSpec((1, 1, n_pages * PAGE), lambda b, pt: (b, 0, 0)),
        scratch_shapes=[pltpu.VMEM((D_IDX, n_pages * PAGE), F32), pltpu.SemaphoreType.DMA((1,))])
    return pl.pallas_call(
        functools.partial(_idx_scores_paged_body, layer=layer, n_pages=n_pages, chunk=chunk),
        name="dsa_sample_scores", grid_spec=grid_spec,
        out_shape=jax.ShapeDtypeStruct((Bd, 1, n_pages * PAGE), F32),
        compiler_params=_params("arbitrary"),
    )(pt_flat, qi, w, cache_ki_t)


def _idx_scores_new(qi, w, ki_rows):
    Bd, R, _ = ki_rows.shape
    return pl.pallas_call(
        _idx_scores_body, grid=(Bd,),
        in_specs=[pl.BlockSpec((1, H_IDX, D_IDX), lambda b: (b, 0, 0)),
                  pl.BlockSpec((1, H_IDX, 1), lambda b: (b, 0, 0)),
                  pl.BlockSpec((None, R, D_IDX), lambda b: (b, 0, 0))],
        out_specs=pl.BlockSpec((1, 1, R), lambda b: (b, 0, 0)),
        out_shape=jax.ShapeDtypeStruct((Bd, 1, R), F32),
        compiler_params=_params("parallel"),
    )(qi, w, ki_rows)


def _dsa_sample_select_body(sc_ref, scn_ref, rel_ref, idx_ref, sb_ref, nb_ref, rs_sc, *, topk, chunk):
    Bd, P = sc_ref.shape
    key_p = _sortable_key(sc_ref[...])
    key_n = _sortable_key(scn_ref[:, 0:1])
    col = lax.broadcasted_iota(I32, (Bd, P), 1)

    one = lambda pred: jnp.where(pred, 1, 0)

    def count(ind_p, ind_n):
        return jnp.sum(ind_p, axis=1, keepdims=True) + ind_n

    thr = _kth_largest_key(lambda t: count(one(key_p >= t), one(key_n >= t)), (Bd, 1), topk)
    n_gt = count(one(key_p > thr), one(key_n > thr))
    n_ge = count(one(key_p >= thr), one(key_n >= thr))
    need = topk - n_gt
    nbits = (P + 1).bit_length()

    def body(b, cut):
        cand = cut | lax.shift_left(jnp.int32(1), nbits - 1 - b)
        cnt = count(jnp.where(key_p == thr, one(col < cand), 0), jnp.where(key_n == thr, one(P < cand), 0))
        return jnp.where(cnt < need, cand, cut)
    cut = lax.fori_loop(0, nbits, body, jnp.zeros((Bd, 1), I32))
    cut = jnp.where(n_ge > topk, cut, jnp.int32(2 ** 30))
    sel_p = jnp.where(key_p > thr, 1, jnp.where(key_p == thr, one(col <= cut), 0))
    sel_n = jnp.where(key_n > thr, 1, jnp.where(key_n == thr, one(P <= cut), 0))
    rank = sel_p
    sh = 1
    while sh < P:
        rank = rank + jnp.where(col >= sh, pltpu.roll(rank, sh, axis=1), 0)
        sh *= 2
    rs_sc[...] = sel_p * rank
    n_past = jnp.sum(sel_p, axis=1, keepdims=True)
    nb_ref[...] = jnp.where(sel_n > 0, rel_ref[0:1, :], NEG)

    slot = lax.broadcasted_iota(I32, (topk, 1), 0)
    ccol = lax.broadcasted_iota(I32, (topk, chunk), 1)
    for b in range(Bd):
        idx = jnp.zeros((topk, 1), I32)
        for c in range(P // chunk):
            rs = rs_sc[b:b + 1, c * chunk:(c + 1) * chunk]
            idx = idx + jnp.sum(jnp.where(rs == slot + 1, ccol + c * chunk, 0), axis=1, keepdims=True)
        idx_ref[b] = idx
        bucket = _t5_bucket(P - idx)
        bias = jnp.zeros((topk, N_HEADS), F32)
        for k in range(N_BUCKETS):
            bias = bias + jnp.where(bucket == k, rel_ref[k:k + 1, :], 0.0)
        sb_ref[b] = jnp.where(slot < n_past[b:b + 1, :], bias, NEG)


def _dsa_sample_select(sc, sc_new, rel_table, topk, chunk=2048):
    Bd, P = sc.shape
    return pl.pallas_call(
        functools.partial(_dsa_sample_select_body, topk=topk, chunk=min(chunk, P)),
        out_shape=[jax.ShapeDtypeStruct((Bd, topk, 1), I32), jax.ShapeDtypeStruct((Bd, topk, N_HEADS), F32),
                   jax.ShapeDtypeStruct((Bd, N_HEADS), F32)],
        scratch_shapes=[pltpu.VMEM((Bd, P), I32)],
        compiler_params=pltpu.CompilerParams(vmem_limit_bytes=V7X_VMEM_LIMIT_BYTES),
    )(sc, sc_new, rel_table)


def _dsa_sample_attend_body(idx_ref, pt_ref, q_ref, kn_ref, vn_ref, sb_ref, nb_ref, kc_hbm, vc_hbm, o_ref,
                            kbuf, vbuf, sems, *, layer, topk, n_pages, scale):
    b = pl.program_id(0)

    def row_copies(j):
        i = idx_ref[b * topk + j]
        pg = pt_ref[b * n_pages + lax.shift_right_logical(i, 7)]
        off = i & (PAGE - 1)
        return (pltpu.make_async_copy(kc_hbm.at[layer, pg, off], kbuf.at[j], sems.at[0]),
                pltpu.make_async_copy(vc_hbm.at[layer, pg, off], vbuf.at[j], sems.at[1]))

    def start(j, c):
        ck, cv = row_copies(j)
        ck.start()
        cv.start()
        return c

    def wait(j, c):
        ck, cv = row_copies(j)
        ck.wait()
        cv.wait()
        return c

    lax.fori_loop(0, topk, start, 0)
    lax.fori_loop(0, topk, wait, 0)

    n_lanes = topk * N_HEADS
    hsel, _ = _head_lane_select(n_lanes)
    q = q_ref[0]
    kb = kbuf[...].reshape(n_lanes, HEAD_DIM).astype(BF16)
    s = lax.dot_general(q, kb, (((1,), (1,)), ((), ())), preferred_element_type=F32) * scale
    s = jnp.where(hsel, s + sb_ref[0], NEG)
    kn = kn_ref[0].astype(BF16).astype(F32)
    vn = vn_ref[0].astype(BF16).astype(F32)
    s_n = jnp.sum(q.astype(F32) * kn, axis=1, keepdims=True) * scale + nb_ref[0]
    m = jnp.maximum(jnp.max(s, axis=1, keepdims=True), s_n)
    p = jnp.exp(s - m)
    p_n = jnp.exp(s_n - m)
    l = jnp.sum(p, axis=1, keepdims=True) + p_n
    vb = vbuf[...].reshape(n_lanes, HEAD_DIM).astype(BF16)
    acc = jnp.dot(p.astype(BF16), vb, preferred_element_type=F32) + p_n.astype(BF16).astype(F32) * vn
    o_ref[0] = acc / l


def _dsa_sample_attend(layer, idx_flat, pt_flat, q, k_new, v_new, slot_bias, new_bias, cache_k, cache_v, topk):
    Bd = q.shape[0]
    n_pages = pt_flat.shape[0] // Bd
    tok = lambda w: pl.BlockSpec((1, N_HEADS, w), lambda b, idx, pt: (b, 0, 0))
    grid_spec = pltpu.PrefetchScalarGridSpec(
        num_scalar_prefetch=2, grid=(Bd,),
        in_specs=[tok(HEAD_DIM), tok(HEAD_DIM), tok(HEAD_DIM),
                  pl.BlockSpec((1, 1, topk * N_HEADS), lambda b, idx, pt: (b, 0, 0)),
                  tok(1),
                  pl.BlockSpec(memory_space=pl.ANY), pl.BlockSpec(memory_space=pl.ANY)],
        out_specs=pl.BlockSpec((1, N_HEADS, HEAD_DIM), lambda b, idx, pt: (b, 0, 0)),
        scratch_shapes=[pltpu.VMEM((topk, N_HEADS, HEAD_DIM), F32), pltpu.VMEM((topk, N_HEADS, HEAD_DIM), F32),
                        pltpu.SemaphoreType.DMA((2,))])
    return pl.pallas_call(
        functools.partial(_dsa_sample_attend_body, layer=layer, topk=topk, n_pages=n_pages,
                          scale=HEAD_DIM ** -0.5),
        grid_spec=grid_spec,
        out_shape=jax.ShapeDtypeStruct((Bd, N_HEADS, HEAD_DIM), F32),
        compiler_params=_params("arbitrary"),
    )(idx_flat, pt_flat, q, k_new, v_new, slot_bias, new_bias, cache_k, cache_v)


W_ATT = N_HEADS * HEAD_DIM
_PROJ_NAMES = ("fq", "fk", "fv", "ff", "bq", "bk", "bv", "iq", "ik", "iw", "su", "gl")
_FF_LANES = slice(0, N_HEADS)
_IW_LANES = slice(N_HEADS, N_HEADS + H_IDX)
_IK_LANES = slice(N_HEADS + H_IDX, N_HEADS + H_IDX + D_IDX)


def _stacked_weights(p):
    D = p["w_in"].shape[1]
    w_c = p["w_glu"].shape[1]
    sizes = (W_ATT, W_ATT, W_ATT, N_HEADS, W_ATT, W_ATT, W_ATT, H_IDX * D_IDX, D_IDX, H_IDX, w_c, 3 * D)
    start, off = {}, 0
    for name, n in zip(_PROJ_NAMES, sizes):
        start[name] = (off, n)
        off += n
    w_in = p["w_in"]
    grab = lambda names: w_in[:, :, start[names[0]][0]:start[names[-1]][0] + start[names[-1]][1]]
    pad = jnp.zeros(w_in.shape[:2] + (LANES - (N_HEADS + H_IDX + D_IDX),), w_in.dtype)
    runs = (("fq", "fk", "fv"), ("bq", "bk", "bv", "iq"), ("su", "gl"))
    sw = {"proj": {}}
    for names in runs:
        arr = grab(names).astype(BF16)
        for n in names:
            sw["proj"][n] = (arr, (start[n][0] - start[names[0]][0], start[n][1]))
    small = jnp.concatenate([grab(("ff",)), grab(("iw",)), grab(("ik",)), pad], axis=2).astype(BF16)
    sw["proj"]["small"] = (small, (0, LANES))
    sw["u"] = p["peer_u"].astype(BF16)
    sw["v_t"] = jnp.swapaxes(p["peer_v"], 1, 2).astype(BF16)
    return sw


def _layer_weights(l, p, sw):
    w_c = p["w_glu"].shape[1]
    lw = {"layer": l, "proj": sw["proj"], "u": sw["u"], "v_t": sw["v_t"]}
    w_glu, w_br = p["w_glu"][l], p["w_br"][l]
    lw["glu_a"], lw["glu_b"] = w_glu[:, :w_c].astype(BF16), w_glu[:, w_c:].astype(BF16)
    lw["wa"] = w_br[:W_ATT].astype(BF16)
    lw["wb"] = w_br[W_ATT:2 * W_ATT].astype(BF16)
    lw["wc"] = w_br[2 * W_ATT:].astype(BF16)
    lw["w_out"] = p["w_out"][l].astype(BF16)
    lw["wq"] = p["peer_wq"][l].astype(BF16)
    lw["keys"] = p["peer_keys"][l].reshape(2 * P_HEADS, N_KEYS, -1).astype(BF16)
    lw["s5"] = _s5_prepare(p["s5_lam_re"][l], p["s5_lam_im"][l], p["s5_log_dt"][l], p["s5_b_re"][l],
                           p["s5_b_im"][l], p["s5_c_re"][l], p["s5_c_im"][l])
    lw["s5_d"] = p["s5_d"][l]
    lw["norm1_g"], lw["norm2_g"] = p["norm1_g"][l], p["norm2_g"][l]
    return lw


def _layer(x, mods, rows_per_group, T, lw, s5_re0, s5_im0, attend):
    M, D = x.shape
    sh1, sc1, gt1, sh2, sc2, gt2 = mods
    tm = min(512, M)
    h = _norm_mod(x, lw["norm1_g"], sc1, sh1, rows_per_group)
    proj = lambda name, dts: _mm(h, lw["proj"][name][0], dts, tm=tm, tn=1024, order="nm", name="proj_" + name,
                                 cols=lw["proj"][name][1], layer=lw["layer"])
    (fq,) = proj("fq", [BF16])
    fk, fk16 = proj("fk", [F32, BF16])
    fv, fv16 = proj("fv", [F32, BF16])
    (bq,) = proj("bq", [BF16])
    bk, bk16 = proj("bk", [F32, BF16])
    bv, bv16 = proj("bv", [F32, BF16])
    (iq,) = proj("iq", [BF16])
    (su,) = proj("su", [F32])
    (small,) = proj("small", [F32])
    (gl,) = proj("gl", [F32])
    o_a, o_b, logf = attend(fq, fk, fk16, fv, fv16, bq, bk, bk16, bv, bv16, iq, small)
    yg, s5_re, s5_im = _s5(su, T, s5_re0, s5_im0, lw["s5"], lw["s5_d"])
    o_c = _glu(yg, lw["glu_a"], lw["glu_b"])
    merged = _merge(o_a, o_b, o_c, lw["wa"], lw["wb"], lw["wc"], gl)
    tn = 512
    (x1,) = _mm(merged, lw["w_out"], [F32], tm=tm, tn=tn,
                epilogue=lambda acc, x_, g_: (x_ + g_ * acc,),
                extras=[(x, (tm, tn), lambda i, j: (i, j)), _mod_extra(gt1, rows_per_group, tm, tn)],
                name="out_proj")
    h2 = _norm_mod(x1, lw["norm2_g"], sc2, sh2, rows_per_group)
    if M < LANES:
        padr = lambda a: jnp.pad(a, ((0, LANES - M), (0, 0)))
        gt2p = jnp.pad(gt2, ((0, 0), (0, LANES - M), (0, 0)))
        x2 = _peer(padr(h2), padr(x1), gt2p, LANES, lw["wq"], lw["keys"], lw["u"], lw["v_t"], lw["layer"])[:M]
    else:
        x2 = _peer(h2, x1, gt2, rows_per_group, lw["wq"], lw["keys"], lw["u"], lw["v_t"], lw["layer"])
    return x2, (fk, fv, logf, bk, bv, small[:, _IK_LANES], s5_re, s5_im)


def kernel(x_prompt, x_sample, cache_fox_k, cache_fox_v, cache_fox_logf, cache_dsa_k, cache_dsa_v,
           cache_dsa_idx_k, state_s5_re, state_s5_im, page_table, c_prompt, c_sample,
           w_ada, b_ada, norm1_g, norm2_g, w_in, b_f, rel_table, s5_lam_re, s5_lam_im, s5_log_dt,
           s5_b_re, s5_b_im, s5_c_re, s5_c_im, s5_d, w_glu, w_br, w_out,
           peer_wq, peer_keys, peer_u, peer_v, final_norm_g):
    p = dict(w_in=w_in, w_glu=w_glu, w_br=w_br, w_out=w_out, peer_wq=peer_wq, peer_keys=peer_keys,
             peer_u=peer_u, peer_v=peer_v, s5_lam_re=s5_lam_re, s5_lam_im=s5_lam_im, s5_log_dt=s5_log_dt,
             s5_b_re=s5_b_re, s5_b_im=s5_b_im, s5_c_re=s5_c_re, s5_c_im=s5_c_im, s5_d=s5_d,
             norm1_g=norm1_g, norm2_g=norm2_g)
    depth = w_in.shape[0]
    B, T, D = x_prompt.shape
    Bd = x_sample.shape[0]
    assert x_sample.shape[1] == 1
    n_pool = cache_fox_k.shape[1]
    n_pages = page_table.shape[1]
    past = n_pages * PAGE
    tile = min(512, T)
    topk_p = min(TOPK_MAX, T // 4)
    topk_s = min(TOPK_MAX, (past + 1) // 4)
    pt_flat = page_table.reshape(-1).astype(I32)
    page_pre, page_tot = _page_prefix(cache_fox_logf.reshape(depth * n_pool, PAGE * N_HEADS).astype(F32))
    cache_ki_t = jnp.swapaxes(cache_dsa_idx_k, 2, 3)
    sw = _stacked_weights(p)
    rel = rel_table.astype(F32)
    tz = _t5_tiles(rel)

    n_c = B + Bd
    c_all = jnp.pad(jnp.concatenate([c_prompt, c_sample], axis=0), ((0, (-n_c) % 8), (0, 0)))
    xp = x_prompt.reshape(B * T, D)
    xs = x_sample.reshape(Bd, D)
    rows_p, rows_s = [], []
    for l in range(depth):
        lw = _layer_weights(l, p, sw)
        m = _adaln(c_all, w_ada, b_ada, l)
        mods_p = [a[:B, None, :] for a in jnp.split(m, 6, axis=1)]
        mods_s = [a[None, B:n_c, :] for a in jnp.split(m, 6, axis=1)]
        b_f_l = b_f[l].astype(F32)

        def attend_prompt(fq, fk, fk16, fv, fv16, bq, bk, bk16, bv, bv16, iq, small):
            r3 = lambda a: a.reshape(B, T, a.shape[-1])
            small3 = r3(small)
            f_t = jnp.moveaxis(small3[:, :, _FF_LANES], -1, 1).reshape(B * N_HEADS, T)
            logf_t, cum = _gate_cumsum(f_t, jnp.tile(b_f_l, B).reshape(B * N_HEADS, 1))
            logf = jnp.moveaxis(logf_t.reshape(B, N_HEADS, T), 1, -1)
            o_a = _flash("fox", r3(fq), r3(fk16), r3(fv16), cum.reshape(B, N_HEADS, T) * LOG2E, tile=tile)
            ki_t = jnp.swapaxes(small3[:, :, _IK_LANES], 1, 2).astype(BF16)
            mask = _dsa_select(r3(iq), small3, ki_t, topk_p)
            o_b = _flash("dsa", r3(bq), r3(bk16), r3(bv16), mask, tz, tile=tile)
            return o_a.reshape(B * T, W_ATT), o_b.reshape(B * T, W_ATT), logf

        def attend_sample(fq, fk, fk16, fv, fv16, bq, bk, bk16, bv, bv16, iq, small, l=l):
            h3 = lambda a: a.reshape(Bd, N_HEADS, HEAD_DIM)
            logf = _logsig(small[:, _FF_LANES], b_f_l.reshape(1, N_HEADS))
            o_a = _fox_sample(l, pt_flat, h3(fq), h3(fk), h3(fv), logf.reshape(Bd, N_HEADS, 1),
                              cache_fox_k, cache_fox_v, page_pre, page_tot)
            qi3 = iq.reshape(Bd, H_IDX, D_IDX)
            w3 = small[:, _IW_LANES].reshape(Bd, H_IDX, 1)
            sc = _idx_scores_paged(l, pt_flat, qi3, w3, cache_ki_t)
            ki_new = jnp.pad(small[:, _IK_LANES].reshape(Bd, 1, D_IDX), ((0, 0), (0, LANES - 1), (0, 0)))
            sc_new = _idx_scores_new(qi3, w3, ki_new)
            idx, sb, nb = _dsa_sample_select(sc.reshape(Bd, past), sc_new.reshape(Bd, LANES), rel, topk_s)
            o_b = _dsa_sample_attend(l, idx.reshape(-1), pt_flat, h3(bq), h3(bk), h3(bv),
                                     sb.reshape(Bd, 1, topk_s * N_HEADS), nb.reshape(Bd, N_HEADS, 1),
                                     cache_dsa_k, cache_dsa_v, topk_s)
            return (o_a.reshape(Bd, W_ATT).astype(BF16), o_b.reshape(Bd, W_ATT).astype(BF16), logf)

        zero_state = jnp.zeros((B,) + state_s5_re.shape[2:], F32)
        xp, rp = _layer(xp, mods_p, T, T, lw, zero_state, zero_state, attend_prompt)
        xs, rs = _layer(xs, mods_s, Bd, 1, lw, state_s5_re[l], state_s5_im[l], attend_sample)
        rows_p.append(rp)
        rows_s.append(rs)

    y_prompt = _norm(xp, final_norm_g).reshape(B, T, D)
    y_sample = _norm(xs, final_norm_g).reshape(Bd, 1, D)

    def leaves(rows, nb, nt):
        fk, fv, fl, bk, bv, ik, sr, si = [jnp.stack(a) for a in zip(*rows)]
        hd = (depth, nb, nt, N_HEADS, HEAD_DIM)
        return (fk.reshape(hd), fv.reshape(hd), fl.reshape(depth, nb, nt, N_HEADS), bk.reshape(hd),
                bv.reshape(hd), ik.reshape(depth, nb, nt, D_IDX), sr, si)

    return (y_prompt, y_sample) + leaves(rows_p, B, T) + leaves(rows_s, Bd, 1)
```

```python
import functools
import math

import jax
import jax.numpy as jnp
from jax import lax
from jax.experimental import pallas as pl
from jax.experimental.pallas import tpu as pltpu

F32 = jnp.float32
BF16 = jnp.bfloat16
I32 = jnp.int32

N_HEADS = 8
HEAD_DIM = 128
H_IDX = 16
D_IDX = 64
TOPK_MAX = 256
GROUP = 16
N_STATE = 64
N_BUCKETS = 32
MAX_DIST = 128
N_KEYS = 128
P_HEADS = 8
P_TOPK = 16
PAGE = 128
EPS = 1e-6

V7X_VMEM_LIMIT_BYTES = 56 * 1024 * 1024
LANES = 128
NEG = -1e30
LOG2E = math.log2(math.e)
INT_MIN = -2 ** 31

_PAIRS = sorted([(i, j) for i in range(P_TOPK) for j in range(P_TOPK) if (i + 1) * (j + 1) <= P_TOPK],
                key=lambda p: p[0] * P_TOPK + p[1])
_N_CAND = 56


def _params(*sem):
    return pltpu.CompilerParams(dimension_semantics=sem, vmem_limit_bytes=V7X_VMEM_LIMIT_BYTES)


def _gelu(x):
    return 0.5 * x * (1.0 + jnp.tanh(math.sqrt(2.0 / math.pi) * (x + 0.044715 * (x * x * x))))


def _sigmoid(x):
    return 1.0 / (1.0 + jnp.exp(-x))


def _mm_body(*refs, n_extra, epilogue):
    a_ref, b_ref = refs[0], refs[1]
    extra = refs[2:2 + n_extra]
    outs = refs[2 + n_extra:]
    acc = jnp.dot(a_ref[...].astype(BF16), b_ref[...].astype(BF16), preferred_element_type=F32)
    vals = epilogue(acc, *[e[...] for e in extra]) if epilogue is not None else (acc,) * len(outs)
    for o, v in zip(outs, vals):
        o[...] = v.astype(o.dtype)


def _mm(a, b, out_dtypes, *, tm, tn, order="mn", epilogue=None, extras=(), name="mm", cols=None, layer=None):
    M, K = a.shape
    col0, N = cols if cols is not None else (0, b.shape[-1])
    tm, tn = min(tm, M), min(tn, N)
    assert M % tm == 0 and N % tn == 0 and col0 % tn == 0, (M, N, tm, tn, col0)
    jb0 = col0 // tn
    if order == "mn":
        grid = (M // tm, N // tn)
        ij = lambda g0, g1: (g0, g1)
    else:
        grid = (N // tn, M // tm)
        ij = lambda g0, g1: (g1, g0)
    if layer is None:
        b_spec = pl.BlockSpec((K, tn), lambda g0, g1: (0, jb0 + ij(g0, g1)[1]))
    else:
        b_spec = pl.BlockSpec((None, K, tn), lambda g0, g1: (layer, 0, jb0 + ij(g0, g1)[1]))
    in_specs = [pl.BlockSpec((tm, K), lambda g0, g1: (ij(g0, g1)[0], 0)), b_spec]
    args = [a, b]
    for arr, bshape, imap in extras:
        in_specs.append(pl.BlockSpec(bshape, lambda g0, g1, imap=imap: imap(*ij(g0, g1))))
        args.append(arr)
    out_specs = [pl.BlockSpec((tm, tn), lambda g0, g1: ij(g0, g1)) for _ in out_dtypes]
    out_shape = [jax.ShapeDtypeStruct((M, N), dt) for dt in out_dtypes]
    return pl.pallas_call(
        functools.partial(_mm_body, n_extra=len(extras), epilogue=epilogue),
        name=name, grid=grid, in_specs=in_specs, out_specs=out_specs, out_shape=out_shape,
        compiler_params=_params("parallel", "parallel"),
    )(*args)


def _mod_extra(mod, rows_per_group, tm, tn):
    r = mod.shape[1]
    return (mod, (None, r, tn), lambda i, j: ((i * tm) // rows_per_group, 0, j))


def _adaln_body(c_ref, w_ref, b_ref, o_ref):
    c = c_ref[...]
    s = c * _sigmoid(c)
    o_ref[...] = jnp.dot(s.astype(BF16), w_ref[...].astype(BF16), preferred_element_type=F32) + b_ref[...]


def _adaln(c, w_ada, b_ada, layer, tn=1024):
    R, D = c.shape
    depth, _, N = w_ada.shape
    return pl.pallas_call(
        _adaln_body, name="adaln", grid=(N // tn,),
        in_specs=[pl.BlockSpec((R, D), lambda j: (0, 0)),
                  pl.BlockSpec((None, D, tn), lambda j: (layer, 0, j)),
                  pl.BlockSpec((None, 1, tn), lambda j: (layer, 0, j))],
        out_specs=pl.BlockSpec((R, tn), lambda j: (0, j)),
        out_shape=jax.ShapeDtypeStruct((R, N), F32),
        compiler_params=_params("parallel"),
    )(c, w_ada, b_ada.reshape(depth, 1, N))


def _norm_mod_body(x_ref, g_ref, sc_ref, sh_ref, o_ref):
    x = x_ref[...]
    y = x * lax.rsqrt(jnp.mean(x * x, axis=-1, keepdims=True) + EPS) * g_ref[...]
    o_ref[...] = (y * (1.0 + sc_ref[...]) + sh_ref[...]).astype(o_ref.dtype)


def _norm_mod(x, g, sc, sh, rows_per_group, tm=512):
    M, D = x.shape
    tm = min(tm, M)
    r = sc.shape[1]
    mod_spec = pl.BlockSpec((None, r, D), lambda i: ((i * tm) // rows_per_group, 0, 0))
    return pl.pallas_call(
        _norm_mod_body, name="norm_mod", grid=(M // tm,),
        in_specs=[pl.BlockSpec((tm, D), lambda i: (i, 0)), pl.BlockSpec((1, D), lambda i: (0, 0)),
                  mod_spec, mod_spec],
        out_specs=pl.BlockSpec((tm, D), lambda i: (i, 0)),
        out_shape=jax.ShapeDtypeStruct((M, D), BF16),
        compiler_params=_params("parallel"),
    )(x, g.reshape(1, D), sc, sh)


def _norm_body(x_ref, g_ref, o_ref):
    x = x_ref[...]
    o_ref[...] = x * lax.rsqrt(jnp.mean(x * x, axis=-1, keepdims=True) + EPS) * g_ref[...]


def _norm(x, g, tm=512):
    M, D = x.shape
    tm = min(tm, M)
    return pl.pallas_call(
        _norm_body, name="final_norm", grid=(M // tm,),
        in_specs=[pl.BlockSpec((tm, D), lambda i: (i, 0)), pl.BlockSpec((1, D), lambda i: (0, 0))],
        out_specs=pl.BlockSpec((tm, D), lambda i: (i, 0)),
        out_shape=jax.ShapeDtypeStruct((M, D), F32),
        compiler_params=_params("parallel"),
    )(x, g.reshape(1, D))


def _log_sigmoid(z):
    return jnp.minimum(z, 0.0) - jnp.log(1.0 + jnp.exp(-jnp.abs(z)))


def _gate_body(f_ref, b_ref, logf_ref, cum_ref):
    logf = _log_sigmoid(f_ref[...] + b_ref[...])
    logf_ref[...] = logf
    T = logf.shape[1]
    lane = lax.broadcasted_iota(I32, logf.shape, 1)
    x = logf
    sh = 1
    while sh < T:
        x = x + jnp.where(lane >= sh, pltpu.roll(x, sh, axis=1), 0.0)
        sh *= 2
    cum_ref[...] = x


def _gate_cumsum(f_t, b_col):
    R, T = f_t.shape
    return pl.pallas_call(
        _gate_body, name="fox_gate",
        out_shape=[jax.ShapeDtypeStruct((R, T), F32), jax.ShapeDtypeStruct((R, T), F32)],
        compiler_params=pltpu.CompilerParams(vmem_limit_bytes=V7X_VMEM_LIMIT_BYTES),
    )(f_t, b_col)


def _flash_body(*refs, mode, tile, scale):
    if mode == "fox":
        q_ref, k_ref, v_ref, fk_ref, o_ref, m_sc, acc_sc = refs
    else:
        q_ref, k_ref, v_ref, mask_ref, tz_ref, o_ref, m_sc, acc_sc = refs
    qi = pl.program_id(1)
    ki = pl.program_id(2)

    @pl.when(ki == 0)
    def _():
        m_sc[...] = jnp.full(m_sc.shape, -jnp.inf, F32)
        acc_sc[...] = jnp.zeros(acc_sc.shape, F32)

    nsub = tile // LANES

    def t5_bias(h, near):
        zero = jnp.zeros((LANES, LANES), F32)
        if near == "diag":
            pick = lambda a, b: tz_ref[h, 0] if a == b else (tz_ref[h, 1] if a == b + 1 else zero)
        else:
            pick = lambda a, b: tz_ref[h, 1] if (a == 0 and b == nsub - 1) else zero
        rows = [jnp.concatenate([pick(a, b) for b in range(nsub)], axis=1) if nsub > 1 else pick(a, 0)
                for a in range(nsub)]
        return jnp.concatenate(rows, axis=0) if nsub > 1 else rows[0]

    def step(near):
        if mode == "fox":
            if near == "diag":
                row = lax.broadcasted_iota(I32, (tile, tile), 0)
                col = lax.broadcasted_iota(I32, (tile, tile), 1)
                keep = row >= col
        else:
            shared = mask_ref[0]
        ones = jnp.ones((tile, HEAD_DIM), BF16)
        for h in range(N_HEADS):
            sl = slice(h * HEAD_DIM, (h + 1) * HEAD_DIM)
            s = lax.dot_general(q_ref[0, :, sl], k_ref[0, :, sl], (((1,), (1,)), ((), ())),
                                preferred_element_type=F32) * (scale * LOG2E)
            if mode == "fox":
                s = s - fk_ref[0, h:h + 1, :]
                if near == "diag":
                    s = jnp.where(keep, s, NEG)
            else:
                s = s + shared
                if near != "far":
                    s = s + t5_bias(h, near)
            m_prev = m_sc[h]
            m_new = jnp.maximum(m_prev, jnp.max(s, axis=1, keepdims=True))
            alpha = jnp.exp2(m_prev - m_new)
            p = jnp.exp2(s - jnp.concatenate([m_new] * nsub, axis=1))
            pv = jnp.dot(p.astype(BF16), jnp.concatenate([v_ref[0, :, sl], ones], axis=1),
                         preferred_element_type=F32)
            acc_sc[h] = jnp.concatenate([alpha] * (2 * HEAD_DIM // LANES), axis=1) * acc_sc[h] + pv
            m_sc[h] = m_new

    if mode == "fox":
        pl.when(ki < qi)(lambda: step("far"))
        pl.when(ki == qi)(lambda: step("diag"))
    else:
        pl.when(ki < qi - 1)(lambda: step("far"))
        pl.when(ki == qi - 1)(lambda: step("next"))
        pl.when(ki == qi)(lambda: step("diag"))

    @pl.when(ki == qi)
    def _():
        for h in range(N_HEADS):
            sl = slice(h * HEAD_DIM, (h + 1) * HEAD_DIM)
            o_ref[0, :, sl] = (acc_sc[h, :, :HEAD_DIM] / acc_sc[h, :, HEAD_DIM:]).astype(o_ref.dtype)


def _flash(mode, q, k, v, *side, tile):
    B, T, W = q.shape
    nt = T // tile
    qspec = pl.BlockSpec((1, tile, W), lambda b, qi, ki: (b, qi, 0))
    kspec = pl.BlockSpec((1, tile, W), lambda b, qi, ki: (b, jnp.minimum(ki, qi), 0))
    if mode == "fox":
        side_specs = [pl.BlockSpec((1, N_HEADS, tile), lambda b, qi, ki: (b, 0, jnp.minimum(ki, qi)))]
    else:
        side_specs = [pl.BlockSpec((1, tile, tile), lambda b, qi, ki: (b, qi, jnp.minimum(ki, qi))),
                      pl.BlockSpec((N_HEADS, 2, LANES, LANES), lambda b, qi, ki: (0, 0, 0, 0))]
    return pl.pallas_call(
        functools.partial(_flash_body, mode=mode, tile=tile, scale=HEAD_DIM ** -0.5),
        name="flash_" + mode, grid=(B, nt, nt),
        in_specs=[qspec, kspec, kspec] + side_specs,
        out_specs=pl.BlockSpec((1, tile, W), lambda b, qi, ki: (b, qi, 0)),
        out_shape=jax.ShapeDtypeStruct((B, T, W), BF16),
        scratch_shapes=[pltpu.VMEM((N_HEADS, tile, LANES), F32), pltpu.VMEM((N_HEADS, tile, 2 * HEAD_DIM), F32)],
        compiler_params=_params("parallel", "parallel", "arbitrary"),
    )(q, k, v, *side)


def _sortable_key(x):
    bits = pltpu.bitcast(x, I32)
    return jnp.where(bits < 0, bits ^ jnp.int32(0x7FFFFFFF), bits)


def _kth_largest_key(count_ge, shape, k):
    def body(b, cur):
        cand = cur | lax.shift_left(jnp.int32(1), 31 - b)
        cnt = count_ge(cand ^ jnp.int32(INT_MIN))
        return jnp.where(cnt >= k, cand, cur)
    cur = lax.fori_loop(0, 32, body, jnp.zeros(shape, I32))
    return cur ^ jnp.int32(INT_MIN)


def _dsa_select_body(qi_ref, small_ref, kit_ref, mask_ref, key_sc, cut_sc, wrep_sc, *, tq, cw, bw, T, topk):
    q0 = pl.program_id(1) * tq
    w = small_ref[0][:, N_HEADS:N_HEADS + H_IDX] * (H_IDX ** -0.5 * D_IDX ** -0.5)
    bucket = (q0 + tq + bw - 1) // bw
    n_valid = bucket * (bw // cw)
    row_c = q0 + lax.broadcasted_iota(I32, (tq, cw), 0)
    lane_c = lax.broadcasted_iota(I32, (tq, cw), 1)

    for h in range(H_IDX):
        wrep_sc[:, h * LANES:(h + 1) * LANES] = jnp.broadcast_to(w[:, h:h + 1], (tq, LANES))

    def score_chunk(c, carry):
        c0 = pl.multiple_of(c * cw, cw)
        kt = kit_ref[0, :, pl.ds(c0, cw)]
        acc = jnp.zeros((tq, cw), F32)
        for h in range(H_IDX):
            d = jnp.dot(qi_ref[0, :, h * D_IDX:(h + 1) * D_IDX], kt, preferred_element_type=F32)
            wh = jnp.concatenate([wrep_sc[:, h * LANES:(h + 1) * LANES]] * (cw // LANES), axis=1)
            acc = acc + wh * jnp.maximum(d, 0.0)
        key_sc[:, pl.ds(c0, cw)] = jnp.where(c0 + lane_c <= row_c, _sortable_key(acc), jnp.int32(INT_MIN))
        return carry

    lax.fori_loop(0, n_valid, score_chunk, 0)

    lane = lax.broadcasted_iota(I32, (tq, LANES), 1)
    row = q0 + lax.broadcasted_iota(I32, (tq, LANES), 0)
    one = lambda pred: jnp.where(pred, 1, 0)

    def select(n_lt):
        def count(ind):
            part = jnp.zeros((tq, LANES), I32)
            for c in range(n_lt):
                part = part + ind(key_sc[:, c * LANES:(c + 1) * LANES], c * LANES + lane)
            return jnp.sum(part, axis=1, keepdims=True)

        thr = _kth_largest_key(lambda t: count(lambda k, col: one(k >= t)), (tq, 1), topk)
        n_gt = count(lambda k, col: one(k > thr))
        n_ge = count(lambda k, col: one(k >= thr))
        excess = jnp.where(thr > jnp.int32(INT_MIN), n_ge - topk, 0)
        cut_sc[...] = jnp.full((tq, 1), T, I32)

        @pl.when(jnp.max(excess) > 0)
        def _():
            need = topk - n_gt
            nbits = max(1, (T - 1).bit_length())

            def body(b, cut):
                cand = cut | lax.shift_left(jnp.int32(1), nbits - 1 - b)
                cnt = count(lambda k, col: jnp.where(k == thr, one(col < cand), 0))
                return jnp.where(cnt < need, cand, cut)
            cut = lax.fori_loop(0, nbits, body, jnp.zeros((tq, 1), I32))
            cut_sc[...] = jnp.where(excess > 0, cut, T)

        cut = cut_sc[...]
        for c in range(n_lt):
            k = key_sc[:, c * LANES:(c + 1) * LANES]
            col = c * LANES + lane
            val = jnp.where(k > thr, 0.0, jnp.where(k == thr, jnp.where(col <= cut, 0.0, NEG), NEG))
            mask_ref[0, :, c * LANES:(c + 1) * LANES] = jnp.where(col <= row, val, NEG)
        if n_lt * LANES < T:
            mask_ref[0, :, n_lt * LANES:] = jnp.full((tq, T - n_lt * LANES), NEG, F32)

    for k in range(1, T // bw + 1):
        pl.when(bucket == k)(functools.partial(select, k * bw // LANES))


def _dsa_select(qi, small, ki_t, topk, tq=256, cw=256, bw=512):
    B, T, _ = qi.shape
    tq, cw, bw = min(tq, T), min(cw, T), min(bw, T)
    assert bw % cw == 0 and T % bw == 0
    return pl.pallas_call(
        functools.partial(_dsa_select_body, tq=tq, cw=cw, bw=bw, T=T, topk=topk),
        name="dsa_select", grid=(B, T // tq),
        in_specs=[pl.BlockSpec((1, tq, H_IDX * D_IDX), lambda b, i: (b, i, 0)),
                  pl.BlockSpec((1, tq, LANES), lambda b, i: (b, i, 0)),
                  pl.BlockSpec((1, D_IDX, T), lambda b, i: (b, 0, 0))],
        out_specs=pl.BlockSpec((1, tq, T), lambda b, i: (b, i, 0)),
        out_shape=jax.ShapeDtypeStruct((B, T, T), F32),
        scratch_shapes=[pltpu.VMEM((tq, T), I32), pltpu.VMEM((tq, 1), I32), pltpu.VMEM((tq, H_IDX * LANES), F32)],
        compiler_params=_params("parallel", "parallel"),
    )(qi, small, ki_t)


def _t5_bucket(dist):
    n = jnp.maximum(dist, 0)
    max_exact = N_BUCKETS // 2
    nf = jnp.maximum(n, 1).astype(F32)
    large = max_exact + (jnp.log(nf / max_exact) / math.log(MAX_DIST / max_exact)
                         * (N_BUCKETS - max_exact)).astype(I32)
    large = jnp.minimum(large, N_BUCKETS - 1)
    return jnp.where(n < max_exact, n, large)


def _t5_tiles(rel_table):
    assert LANES >= MAX_DIST
    i = jnp.arange(LANES)
    bucket = _t5_bucket((jnp.arange(2) * LANES)[:, None, None] + i[None, :, None] - i[None, None, :])
    rel = rel_table.astype(F32)
    tz = jnp.zeros((rel.shape[1],) + bucket.shape, F32)
    for k in range(N_BUCKETS - 1):
        tz = tz + jnp.where(bucket[None] == k, (rel[k] - rel[N_BUCKETS - 1])[:, None, None, None], 0.0)
    return tz * LOG2E


def _s5_disc_body(lr_ref, li_ref, ldt_ref, br_ref, bi_ref, abr_ref, abi_ref, bbr_ref, bbi_ref):
    lr, li = lr_ref[...], li_ref[...]
    dt = jnp.exp(ldt_ref[...])
    mag = jnp.exp(lr * dt)
    ab_re, ab_im = mag * jnp.cos(li * dt), mag * jnp.sin(li * dt)
    den = lr * lr + li * li
    nr = ab_re - 1.0
    k_re = (nr * lr + ab_im * li) / den
    k_im = (ab_im * lr - nr * li) / den
    br, bi = br_ref[...], bi_ref[...]
    abr_ref[...] = ab_re
    abi_ref[...] = ab_im
    bbr_ref[...] = k_re * br - k_im * bi
    bbi_ref[...] = k_re * bi + k_im * br


def _s5_discretise(lam_re, lam_im, log_dt, b_re, b_im):
    G, N, P = b_re.shape
    rep = lambda a: jnp.broadcast_to(a.reshape(G * N, 1), (G * N, P))
    ldt = jnp.broadcast_to(log_dt.reshape(G, 1, 1), (G, N, P)).reshape(G * N, P)
    shp = jax.ShapeDtypeStruct((G * N, P), F32)
    abr, abi, bbr, bbi = pl.pallas_call(
        _s5_disc_body, name="s5_discretise", out_shape=[shp] * 4,
        compiler_params=pltpu.CompilerParams(vmem_limit_bytes=V7X_VMEM_LIMIT_BYTES),
    )(rep(lam_re), rep(lam_im), ldt, b_re.reshape(G * N, P), b_im.reshape(G * N, P))
    return (abr[:, 0].reshape(G, N), abi[:, 0].reshape(G, N),
            bbr.reshape(G, N, P), bbi.reshape(G, N, P))


def _s5_body(u_ref, bc_ref, cc_ref, a_ref, d_ref, x0_ref, y_ref, xT_ref, xs_sc, carry_sc, *, tc, S, lc):
    c = pl.program_id(1)

    @pl.when(c == 0)
    def _():
        carry_sc[...] = x0_ref[0]

    n_ct = u_ref.shape[2] // LANES
    ks = S // n_ct
    ch = lambda t: slice(t * LANES, (t + 1) * LANES)
    st = lambda j: slice(j * ks, (j + 1) * ks)
    for j in range(2 * n_ct):
        xs_sc[:, st(j)] = jnp.dot(u_ref[0, :, ch(j % n_ct)].astype(BF16), bc_ref[j], preferred_element_type=F32)

    for j in range(S // lc):
        slr = slice(j * lc, (j + 1) * lc)
        sli = slice(S + j * lc, S + (j + 1) * lc)
        ar, ai = a_ref[:, slr], a_ref[:, sli]

        def step(t, carry, slr=slr, sli=sli, ar=ar, ai=ai):
            xr, xi = carry
            nr = ar * xr - ai * xi + xs_sc[pl.ds(t, 1), slr]
            ni = ar * xi + ai * xr + xs_sc[pl.ds(t, 1), sli]
            xs_sc[pl.ds(t, 1), slr] = nr
            xs_sc[pl.ds(t, 1), sli] = ni
            return nr, ni

        xr, xi = lax.fori_loop(0, tc, step, (carry_sc[:, slr], carry_sc[:, sli]), unroll=min(8, tc))
        carry_sc[:, slr] = xr
        carry_sc[:, sli] = xi

    for t in range(n_ct):
        y = (jnp.dot(xs_sc[:, st(t)].astype(BF16), cc_ref[t], preferred_element_type=F32)
             + jnp.dot(xs_sc[:, st(n_ct + t)].astype(BF16), cc_ref[n_ct + t], preferred_element_type=F32)
             + d_ref[:, ch(t)] * u_ref[0, :, ch(t)])
        y_ref[0, :, ch(t)] = _gelu(y).astype(y_ref.dtype)
    xT_ref[0] = carry_sc[...]


def _glu_body(y_ref, wa_ref, wb_ref, o_ref):
    y = y_ref[...]
    a = jnp.dot(y, wa_ref[...], preferred_element_type=F32)
    b = jnp.dot(y, wb_ref[...], preferred_element_type=F32)
    o_ref[...] = (a * _sigmoid(b)).astype(o_ref.dtype)


def _glu(y, wa, wb, tm=512):
    M, K = y.shape
    N = wa.shape[1]
    tm = min(tm, M)
    return pl.pallas_call(
        _glu_body, name="s5_glu", grid=(M // tm,),
        in_specs=[pl.BlockSpec((tm, K), lambda i: (i, 0)), pl.BlockSpec((K, N), lambda i: (0, 0)),
                  pl.BlockSpec((K, N), lambda i: (0, 0))],
        out_specs=pl.BlockSpec((tm, N), lambda i: (i, 0)),
        out_shape=jax.ShapeDtypeStruct((M, N), BF16),
        compiler_params=_params("parallel"),
    )(y, wa, wb)


def _merge_body(oa_ref, ob_ref, oc_ref, wa_ref, wb_ref, wc_ref, ga_ref, gb_ref, gc_ref, o_ref):
    dot = lambda x, w: jnp.dot(x[...], w[...], preferred_element_type=F32)
    m = (_sigmoid(ga_ref[...]) * dot(oa_ref, wa_ref) + _sigmoid(gb_ref[...]) * dot(ob_ref, wb_ref)
         + _sigmoid(gc_ref[...]) * dot(oc_ref, wc_ref))
    o_ref[...] = m.astype(o_ref.dtype)


def _merge(o_a, o_b, o_c, wa, wb, wc, gl, tm=512, tn=512):
    M, K = o_a.shape
    D = wa.shape[1]
    tm = min(tm, M)
    nd = D // tn
    ospec = pl.BlockSpec((tm, K), lambda i, j: (i, 0))
    wspec = pl.BlockSpec((K, tn), lambda i, j: (0, j))
    gspec = lambda g: pl.BlockSpec((tm, tn), lambda i, j, g=g: (i, g * nd + j))
    return pl.pallas_call(
        _merge_body, name="branch_merge", grid=(M // tm, nd),
        in_specs=[ospec, ospec, ospec, wspec, wspec, wspec, gspec(0), gspec(1), gspec(2)],
        out_specs=pl.BlockSpec((tm, tn), lambda i, j: (i, j)),
        out_shape=jax.ShapeDtypeStruct((M, D), BF16),
        compiler_params=_params("parallel", "parallel"),
    )(o_a, o_b, o_c, wa, wb, wc, gl, gl, gl)


def _bf16_pair(x):
    hi = pltpu.bitcast(x.astype(BF16).astype(F32), I32)
    return hi | lax.shift_right_logical(hi, 16)


def _top16_rows(s, n_rows):
    iota = lax.broadcasted_iota(I32, s.shape, 0)
    rank = jnp.full(s.shape, P_TOPK, I32)
    vals = []
    work = s
    for k in range(P_TOPK):
        m = jnp.max(work, axis=0, keepdims=True)
        idx = jnp.min(jnp.where(work == m, iota, n_rows), axis=0, keepdims=True)
        hit = iota == idx
        rank = jnp.where(hit, k, rank)
        work = jnp.where(hit, -jnp.inf, work)
        vals.append(m)
    return rank, vals


def _peer_select_body(q_ref, keys_ref, a1_ref, n1_ref, a2_ref, r2_ref, cand_sc, sel_sc):
    tt = q_ref.shape[0]
    nt = (((1,), (1,)), ((), ()))
    for h in range(P_HEADS):
        s1 = lax.dot_general(keys_ref[2 * h], q_ref[:, (2 * h) * N_KEYS:(2 * h + 1) * N_KEYS], nt,
                             preferred_element_type=F32)
        s2 = lax.dot_general(keys_ref[2 * h + 1], q_ref[:, (2 * h + 1) * N_KEYS:(2 * h + 2) * N_KEYS], nt,
                             preferred_element_type=F32)
        r1, v1 = _top16_rows(s1, N_KEYS)
        r2, v2 = _top16_rows(s2, N_KEYS)
        for r, (i, j) in enumerate(_PAIRS):
            cand_sc[r:r + 1, :] = v1[i] + v2[j]
        cand_sc[len(_PAIRS):, :] = jnp.full((_N_CAND - len(_PAIRS), tt), -jnp.inf, F32)
        rc, cv = _top16_rows(cand_sc[...], _N_CAND)
        z = jnp.zeros((1, tt), F32)
        for k in range(P_TOPK):
            z = z + jnp.exp(cv[k] - cv[0])
        sel_sc[...] = jnp.where(rc < P_TOPK, 1, 0)
        cnt = [jnp.zeros((1, tt), I32) for _ in range(P_TOPK)]
        for r, (i, j) in enumerate(_PAIRS):
            cnt[i] = cnt[i] + sel_sc[r:r + 1, :]
        n1 = jnp.zeros((N_KEYS, tt), I32)
        for i in range(P_TOPK):
            n1 = jnp.where(r1 == i, cnt[i], n1)
        a1_ref[h] = _bf16_pair(jnp.exp(s1 - v1[0]) / z)
        n1_ref[h] = _bf16_pair(n1.astype(F32))
        a2_ref[h] = jnp.exp(s2 - v2[0]).astype(a2_ref.dtype)
        r2_ref[h] = r2.astype(F32).astype(r2_ref.dtype)


def _peer_select(q, keys, tt=256):
    M = q.shape[0]
    tt = min(tt, M)
    tab = pl.BlockSpec((P_HEADS, N_KEYS, tt), lambda i: (0, 0, i))
    shp = lambda dt: jax.ShapeDtypeStruct((P_HEADS, N_KEYS, M), dt)
    return pl.pallas_call(
        _peer_select_body, name="peer_select", grid=(M // tt,),
        in_specs=[pl.BlockSpec((tt, q.shape[1]), lambda i: (i, 0)),
                  pl.BlockSpec(keys.shape, lambda i: (0, 0, 0))],
        out_specs=[tab, tab, tab, tab],
        out_shape=[shp(I32), shp(I32), shp(BF16), shp(BF16)],
        scratch_shapes=[pltpu.VMEM((_N_CAND, tt), F32), pltpu.VMEM((_N_CAND, tt), I32)],
        compiler_params=_params("parallel"),
    )(q, keys)


def _peer_dense_body(h_ref, u_ref, vt_ref, a1_ref, n1_ref, a2_ref, r2_ref, x_ref, gt_ref, o_ref,
                     acc_sc, act_sc, ga_sc, *, te):
    j = pl.program_id(1)
    tt = h_ref.shape[0]

    @pl.when(j == 0)
    def _():
        acc_sc[...] = jnp.zeros(acc_sc.shape, F32)
        act_sc[1] = jnp.zeros(act_sc.shape[1:], F32)

    def step(slot):
        act_sc[slot] = lax.dot_general(u_ref[...], h_ref[...], (((1,), (1,)), ((), ())),
                                       preferred_element_type=F32)
        for r in range(te // N_KEYS):
            rows = slice(r * N_KEYS, (r + 1) * N_KEYS)
            wt = None
            for h in range(P_HEADS):
                tile_rows = lambda ref: pltpu.bitcast(jnp.broadcast_to(ref[h, r:r + 1, :], (8, tt)), BF16)
                a1 = jnp.concatenate([tile_rows(a1_ref)] * (N_KEYS // 16), axis=0)
                n1 = jnp.concatenate([tile_rows(n1_ref)] * (N_KEYS // 16), axis=0)
                term = jnp.where(r2_ref[h] < n1, a1 * a2_ref[h], jnp.zeros((), BF16))
                wt = term if wt is None else wt + term
            ga_sc[rows, :] = wt * _gelu(act_sc[1 - slot, rows, :].astype(BF16))
        acc_sc[...] += jnp.dot(vt_ref[...], ga_sc[...], preferred_element_type=F32)

    pl.when(lax.rem(j, 2) == 0)(functools.partial(step, 0))
    pl.when(lax.rem(j, 2) == 1)(functools.partial(step, 1))

    @pl.when(j == pl.num_programs(1) - 1)
    def _():
        o_ref[...] = x_ref[...] + gt_ref[...] * acc_sc[...].T


def _peer_dense(h2, u, v_t, layer, tabs, x, gt, rows_per_group, tt=512, te=1024):
    M, D = h2.shape
    E = u.shape[1]
    tt = min(tt, M)
    n_e1 = te // N_KEYS
    nc = E // te
    chunk = lambda j: jnp.clip(j, 0, nc - 1)
    tab1 = pl.BlockSpec((P_HEADS, n_e1, tt), lambda i, j: (0, chunk(j - 1), i))
    tab2 = pl.BlockSpec((P_HEADS, N_KEYS, tt), lambda i, j: (0, 0, i))
    r = gt.shape[1]
    return pl.pallas_call(
        functools.partial(_peer_dense_body, te=te),
        name="peer_dense", grid=(M // tt, nc + 1),
        in_specs=[pl.BlockSpec((tt, D), lambda i, j: (i, 0)),
                  pl.BlockSpec((None, te, D), lambda i, j: (layer, chunk(j), 0)),
                  pl.BlockSpec((None, D, te), lambda i, j: (layer, 0, chunk(j - 1))),
                  tab1, tab1, tab2, tab2,
                  pl.BlockSpec((tt, D), lambda i, j: (i, 0)),
                  pl.BlockSpec((None, r, D), lambda i, j: ((i * tt) // rows_per_group, 0, 0))],
        out_specs=pl.BlockSpec((tt, D), lambda i, j: (i, 0)),
        out_shape=jax.ShapeDtypeStruct((M, D), F32),
        scratch_shapes=[pltpu.VMEM((D, tt), F32), pltpu.VMEM((2, te, tt), F32), pltpu.VMEM((te, tt), BF16)],
        compiler_params=_params("parallel", "arbitrary"),
    )(h2, u, v_t, *tabs, x, gt)


def _peer(h2, x, gt, rows_per_group, wq, keys, u, v_t, layer):
    (q,) = _mm(h2, wq, [BF16], tm=512, tn=1024, name="peer_query")
    tabs = _peer_select(q, keys)
    return _peer_dense(h2, u, v_t, layer, tabs, x, gt, rows_per_group)


def _s5_prepare(lam_re, lam_im, log_dt, b_re, b_im, c_re, c_im):
    G, N, P = b_re.shape
    S, W = G * N, G * P
    ab_re, ab_im, bb_re, bb_im = _s5_discretise(lam_re, lam_im, log_dt, b_re, b_im)
    cg = LANES // P
    n_ct = G // cg
    eye = jnp.eye(cg, dtype=F32)

    def tiles(m, rows, cols):
        t = jnp.swapaxes(m, 1, 2).reshape(n_ct, cg, m.shape[2], m.shape[1])
        return (t[:, :, :, None, :] * eye[None, :, None, :, None]).reshape(n_ct, rows, cols)

    bc = jnp.concatenate([tiles(bb_re, LANES, cg * N), tiles(bb_im, LANES, cg * N)], axis=0).astype(BF16)
    cc = jnp.concatenate([tiles(c_re.astype(F32), cg * N, LANES), -tiles(c_im.astype(F32), cg * N, LANES)],
                         axis=0).astype(BF16)
    a_row = jnp.concatenate([ab_re.reshape(1, S), ab_im.reshape(1, S)], axis=1)
    return a_row, bc, cc


def _s5(u, T, x0_re, x0_im, prep, d_skip, tc=128, lc=1024):
    M, W = u.shape
    B = M // T
    G, N = x0_re.shape[1:]
    S = G * N
    a_row, bc, cc = prep
    tc = min(tc, T)
    x0 = jnp.concatenate([x0_re.reshape(B, 1, S), x0_im.reshape(B, 1, S)], axis=2).astype(F32)
    yg, x_last = pl.pallas_call(
        functools.partial(_s5_body, tc=tc, S=S, lc=lc),
        name="s5", grid=(B, T // tc),
        in_specs=[pl.BlockSpec((1, tc, W), lambda b, c: (b, c, 0)),
                  pl.BlockSpec(bc.shape, lambda b, c: (0, 0, 0)),
                  pl.BlockSpec(cc.shape, lambda b, c: (0, 0, 0)),
                  pl.BlockSpec((1, 2 * S), lambda b, c: (0, 0)),
                  pl.BlockSpec((1, W), lambda b, c: (0, 0)),
                  pl.BlockSpec((1, 1, 2 * S), lambda b, c: (b, 0, 0))],
        out_specs=[pl.BlockSpec((1, tc, W), lambda b, c: (b, c, 0)),
                   pl.BlockSpec((1, 1, 2 * S), lambda b, c: (b, 0, 0))],
        out_shape=[jax.ShapeDtypeStruct((B, T, W), BF16), jax.ShapeDtypeStruct((B, 1, 2 * S), F32)],
        scratch_shapes=[pltpu.VMEM((tc, 2 * S), F32), pltpu.VMEM((1, 2 * S), F32)],
        compiler_params=_params("parallel", "arbitrary"),
    )(u.reshape(B, T, W), bc, cc, a_row, d_skip.reshape(1, W).astype(F32), x0)
    return yg.reshape(M, W), x_last[:, 0, :S].reshape(B, G, N), x_last[:, 0, S:].reshape(B, G, N)


def _logsig_body(f_ref, b_ref, o_ref):
    o_ref[...] = _log_sigmoid(f_ref[...] + b_ref[...])


def _logsig(f, b_row):
    return pl.pallas_call(_logsig_body, out_shape=jax.ShapeDtypeStruct(f.shape, F32))(f, b_row)


def _head_lane_select(n_lanes):
    lane = lax.broadcasted_iota(I32, (N_HEADS, n_lanes), 1)
    head = lax.broadcasted_iota(I32, (N_HEADS, n_lanes), 0)
    return (lane & (N_HEADS - 1)) == head, lane


def _page_prefix_body(lf_ref, pre_ref, tot_ref):
    lf = lf_ref[...]
    n_lanes = lf.shape[1]
    lane = lax.broadcasted_iota(I32, lf.shape, 1)
    pre, tot = lf, lf
    sh = N_HEADS
    while sh < n_lanes:
        pre = pre + jnp.where(lane >= sh, pltpu.roll(pre, sh, axis=1), 0.0)
        tot = tot + pltpu.roll(tot, sh, axis=1)
        sh *= 2
    pre_ref[...] = pre
    tot_ref[...] = tot


def _page_prefix(logf_rows):
    R, n_lanes = logf_rows.shape
    tr = math.gcd(R, 256)
    spec = pl.BlockSpec((tr, n_lanes), lambda i: (i, 0))
    return pl.pallas_call(
        _page_prefix_body, name="fox_page_prefix", grid=(R // tr,), in_specs=[spec], out_specs=[spec, spec],
        out_shape=[jax.ShapeDtypeStruct((R, n_lanes), F32)] * 2,
        compiler_params=_params("parallel"),
    )(logf_rows)


def _fox_sample_body(pt_ref, q_ref, kn_ref, vn_ref, lfn_ref, k_hbm, v_hbm, pre_hbm, tot_hbm, o_ref,
                     kbuf, vbuf, pbuf, tbuf, sems, m_sc, l_sc, acc_sc, carry_sc,
                     *, layer, n_pool, n_pages, pp, scale):
    g = pl.program_id(1)
    ng = pl.num_programs(1)
    step = pl.program_id(0) * ng + g
    n_steps = pl.num_programs(0) * ng
    slot = lax.rem(step, 2)
    page_lanes = PAGE * N_HEADS
    n_lanes = pp * page_lanes
    nt = (((1,), (1,)), ((), ()))

    def copies(st, sl):
        first = (st // ng) * n_pages + lax.rem(st, ng) * pp
        out = []
        for k in range(pp):
            pg = pt_ref[first + k]
            row = layer * n_pool + pg
            out += [pltpu.make_async_copy(k_hbm.at[layer, pg], kbuf.at[sl, pl.ds(k * PAGE, PAGE)], sems.at[0, sl]),
                    pltpu.make_async_copy(v_hbm.at[layer, pg], vbuf.at[sl, pl.ds(k * PAGE, PAGE)], sems.at[1, sl]),
                    pltpu.make_async_copy(pre_hbm.at[pl.ds(row, 1)], pbuf.at[sl, pl.ds(k, 1)], sems.at[2, sl]),
                    pltpu.make_async_copy(tot_hbm.at[pl.ds(row, 1)], tbuf.at[sl, pl.ds(k, 1)], sems.at[3, sl])]
        return out

    @pl.when(step == 0)
    def _():
        for c in copies(step, slot):
            c.start()

    @pl.when(step + 1 < n_steps)
    def _():
        for c in copies(step + 1, 1 - slot):
            c.start()

    @pl.when(g == 0)
    def _():
        m_sc[...] = jnp.full(m_sc.shape, -jnp.inf, F32)
        l_sc[...] = jnp.zeros(l_sc.shape, F32)
        acc_sc[...] = jnp.zeros(acc_sc.shape, F32)
        carry_sc[...] = jnp.zeros(carry_sc.shape, F32)

    for c in copies(step, slot):
        c.wait()

    hsel, lane8 = _head_lane_select(n_lanes)
    q = q_ref[0]
    kp = kbuf[slot].reshape(n_lanes, HEAD_DIM).astype(BF16)
    s = lax.dot_general(q, kp, nt, preferred_element_type=F32) * scale
    carry = carry_sc[...]
    f_pages = []
    for k in range(pp):
        f_pages.append(carry + pbuf[slot, k:k + 1, :])
        carry = carry + tbuf[slot, k:k + 1, :]
    carry_sc[...] = carry
    f_k = jnp.concatenate(f_pages, axis=1) if pp > 1 else f_pages[0]
    s = jnp.where(hsel, s - f_k, NEG)
    m_prev = m_sc[...]
    m_new = jnp.maximum(m_prev, jnp.max(s, axis=1, keepdims=True))
    alpha = jnp.exp(m_prev - m_new)
    pr = jnp.exp(s - m_new)
    l_sc[...] = alpha * l_sc[...] + jnp.sum(pr, axis=1, keepdims=True)
    vp = vbuf[slot].reshape(n_lanes, HEAD_DIM).astype(BF16)
    acc_sc[...] = alpha * acc_sc[...] + jnp.dot(pr.astype(BF16), vp, preferred_element_type=F32)
    m_sc[...] = m_new

    @pl.when(g == ng - 1)
    def _():
        hsel_p, lane_p = _head_lane_select(page_lanes)
        f_col = jnp.sum(jnp.where(hsel_p, jnp.where(lane_p < N_HEADS, carry_sc[...], 0.0), 0.0),
                        axis=1, keepdims=True)
        kn = kn_ref[0].astype(BF16).astype(F32)
        vn = vn_ref[0].astype(BF16).astype(F32)
        s_n = jnp.sum(q.astype(F32) * kn, axis=1, keepdims=True) * scale - (f_col + lfn_ref[0])
        m_prev = m_sc[...]
        m_new = jnp.maximum(m_prev, s_n)
        alpha = jnp.exp(m_prev - m_new)
        p_n = jnp.exp(s_n - m_new)
        l = alpha * l_sc[...] + p_n
        acc = alpha * acc_sc[...] + p_n.astype(BF16).astype(F32) * vn
        o_ref[0] = acc / l


def _fox_sample(layer, pt_flat, q, k_new, v_new, logf_new, cache_k, cache_v, page_pre, page_tot, pp=8):
    Bd = q.shape[0]
    n_pool = cache_k.shape[1]
    n_pages = pt_flat.shape[0] // Bd
    pp = math.gcd(pp, n_pages)
    tok = lambda w: pl.BlockSpec((1, N_HEADS, w), lambda b, g, pt: (b, 0, 0))
    hbm = pl.BlockSpec(memory_space=pl.ANY)
    grid_spec = pltpu.PrefetchScalarGridSpec(
        num_scalar_prefetch=1, grid=(Bd, n_pages // pp),
        in_specs=[tok(HEAD_DIM), tok(HEAD_DIM), tok(HEAD_DIM), tok(1), hbm, hbm, hbm, hbm],
        out_specs=pl.BlockSpec((1, N_HEADS, HEAD_DIM), lambda b, g, pt: (b, 0, 0)),
        scratch_shapes=[pltpu.VMEM((2, pp * PAGE, N_HEADS, HEAD_DIM), F32),
                        pltpu.VMEM((2, pp * PAGE, N_HEADS, HEAD_DIM), F32),
                        pltpu.VMEM((2, pp, PAGE * N_HEADS), F32), pltpu.VMEM((2, pp, PAGE * N_HEADS), F32),
                        pltpu.SemaphoreType.DMA((4, 2)),
                        pltpu.VMEM((N_HEADS, 1), F32), pltpu.VMEM((N_HEADS, 1), F32),
                        pltpu.VMEM((N_HEADS, HEAD_DIM), F32), pltpu.VMEM((1, PAGE * N_HEADS), F32)])
    return pl.pallas_call(
        functools.partial(_fox_sample_body, layer=layer, n_pool=n_pool, n_pages=n_pages, pp=pp,
                          scale=HEAD_DIM ** -0.5),
        name="fox_sample", grid_spec=grid_spec,
        out_shape=jax.ShapeDtypeStruct((Bd, N_HEADS, HEAD_DIM), F32),
        compiler_params=_params("arbitrary", "arbitrary"),
    )(pt_flat, q, k_new, v_new, logf_new, cache_k, cache_v, page_pre, page_tot)


def _idx_scores(qi, w, ki, keys_on_lanes=False):
    contract = (((1,), (0,)), ((), ())) if keys_on_lanes else (((1,), (1,)), ((), ()))
    d = lax.dot_general(qi, ki.astype(BF16), contract, preferred_element_type=F32)
    return jnp.sum(w * (H_IDX ** -0.5 * D_IDX ** -0.5) * jnp.maximum(d, 0.0), axis=0, keepdims=True)


def _idx_scores_body(qi_ref, w_ref, ki_ref, o_ref):
    o_ref[0] = _idx_scores(qi_ref[0], w_ref[0], ki_ref[...])


def _idx_scores_paged_body(pt_ref, qi_ref, w_ref, ki_hbm, o_ref, buf, sem, *, layer, n_pages, chunk):
    b = pl.program_id(0)

    def page_copy(p):
        return pltpu.make_async_copy(ki_hbm.at[layer, pt_ref[b * n_pages + p]],
                                     buf.at[:, pl.ds(pl.multiple_of(p * PAGE, PAGE), PAGE)], sem.at[0])

    def start(p, c):
        page_copy(p).start()
        return c

    def wait(p, c):
        page_copy(p).wait()
        return c

    lax.fori_loop(0, n_pages, start, 0)
    lax.fori_loop(0, n_pages, wait, 0)
    for c in range(n_pages * PAGE // chunk):
        o_ref[0, :, c * chunk:(c + 1) * chunk] = _idx_scores(qi_ref[0], w_ref[0], buf[:, c * chunk:(c + 1) * chunk],
                                                             keys_on_lanes=True)


def _idx_scores_paged(layer, pt_flat, qi, w, cache_ki_t, chunk=2048):
    Bd = qi.shape[0]
    n_pages = pt_flat.shape[0] // Bd
    chunk = math.gcd(chunk, n_pages * PAGE)
    grid_spec = pltpu.PrefetchScalarGridSpec(
        num_scalar_prefetch=1, grid=(Bd,),
        in_specs=[pl.BlockSpec((1, H_IDX, D_IDX), lambda b, pt: (b, 0, 0)),
                  pl.BlockSpec((1, H_IDX, 1), lambda b, pt: (b, 0, 0)),
                  pl.BlockSpec(memory_space=pl.ANY)],
        out_specs=pl.BlockSpec((1, 1, n_pages * PAGE), lambda b, pt: (b, 0, 0)),
        scratch_shapes=[pltpu.VMEM((D_IDX, n_pages * PAGE), F32), pltpu.SemaphoreType.DMA((1,))])
    return pl.pallas_call(
        functools.partial(_idx_scores_paged_body, layer=layer, n_pages=n_pages, chunk=chunk),
        name="dsa_sample_scores", grid_spec=grid_spec,
        out_shape=jax.ShapeDtypeStruct((Bd, 1, n_pages * PAGE), F32),
        compiler_params=_params("arbitrary"),
    )(pt_flat, qi, w, cache_ki_t)


def _idx_scores_new(qi, w, ki_rows):
    Bd, R, _ = ki_rows.shape
    return pl.pallas_call(
        _idx_scores_body, grid=(Bd,),
        in_specs=[pl.BlockSpec((1, H_IDX, D_IDX), lambda b: (b, 0, 0)),
                  pl.BlockSpec((1, H_IDX, 1), lambda b: (b, 0, 0)),
                  pl.BlockSpec((None, R, D_IDX), lambda b: (b, 0, 0))],
        out_specs=pl.BlockSpec((1, 1, R), lambda b: (b, 0, 0)),
        out_shape=jax.ShapeDtypeStruct((Bd, 1, R), F32),
        compiler_params=_params("parallel"),
    )(qi, w, ki_rows)


def _dsa_sample_select_body(sc_ref, scn_ref, rel_ref, idx_ref, sb_ref, nb_ref, rs_sc, *, topk, chunk):
    Bd, P = sc_ref.shape
    key_p = _sortable_key(sc_ref[...])
    key_n = _sortable_key(scn_ref[:, 0:1])
    col = lax.broadcasted_iota(I32, (Bd, P), 1)

    one = lambda pred: jnp.where(pred, 1, 0)

    def count(ind_p, ind_n):
        return jnp.sum(ind_p, axis=1, keepdims=True) + ind_n

    thr = _kth_largest_key(lambda t: count(one(key_p >= t), one(key_n >= t)), (Bd, 1), topk)
    n_gt = count(one(key_p > thr), one(key_n > thr))
    n_ge = count(one(key_p >= thr), one(key_n >= thr))
    need = topk - n_gt
    nbits = (P + 1).bit_length()

    def body(b, cut):
        cand = cut | lax.shift_left(jnp.int32(1), nbits - 1 - b)
        cnt = count(jnp.where(key_p == thr, one(col < cand), 0), jnp.where(key_n == thr, one(P < cand), 0))
        return jnp.where(cnt < need, cand, cut)
    cut = lax.fori_loop(0, nbits, body, jnp.zeros((Bd, 1), I32))
    cut = jnp.where(n_ge > topk, cut, jnp.int32(2 ** 30))
    sel_p = jnp.where(key_p > thr, 1, jnp.where(key_p == thr, one(col <= cut), 0))
    sel_n = jnp.where(key_n > thr, 1, jnp.where(key_n == thr, one(P <= cut), 0))
    rank = sel_p
    sh = 1
    while sh < P:
        rank = rank + jnp.where(col >= sh, pltpu.roll(rank, sh, axis=1), 0)
        sh *= 2
    rs_sc[...] = sel_p * rank
    n_past = jnp.sum(sel_p, axis=1, keepdims=True)
    nb_ref[...] = jnp.where(sel_n > 0, rel_ref[0:1, :], NEG)

    slot = lax.broadcasted_iota(I32, (topk, 1), 0)
    ccol = lax.broadcasted_iota(I32, (topk, chunk), 1)
    for b in range(Bd):
        idx = jnp.zeros((topk, 1), I32)
        for c in range(P // chunk):
            rs = rs_sc[b:b + 1, c * chunk:(c + 1) * chunk]
            idx = idx + jnp.sum(jnp.where(rs == slot + 1, ccol + c * chunk, 0), axis=1, keepdims=True)
        idx_ref[b] = idx
        bucket = _t5_bucket(P - idx)
        bias = jnp.zeros((topk, N_HEADS), F32)
        for k in range(N_BUCKETS):
            bias = bias + jnp.where(bucket == k, rel_ref[k:k + 1, :], 0.0)
        sb_ref[b] = jnp.where(slot < n_past[b:b + 1, :], bias, NEG)


def _dsa_sample_select(sc, sc_new, rel_table, topk, chunk=2048):
    Bd, P = sc.shape
    return pl.pallas_call(
        functools.partial(_dsa_sample_select_body, topk=topk, chunk=min(chunk, P)),
        out_shape=[jax.ShapeDtypeStruct((Bd, topk, 1), I32), jax.ShapeDtypeStruct((Bd, topk, N_HEADS), F32),
                   jax.ShapeDtypeStruct((Bd, N_HEADS), F32)],
        scratch_shapes=[pltpu.VMEM((Bd, P), I32)],
        compiler_params=pltpu.CompilerParams(vmem_limit_bytes=V7X_VMEM_LIMIT_BYTES),
    )(sc, sc_new, rel_table)


def _dsa_sample_attend_body(idx_ref, pt_ref, q_ref, kn_ref, vn_ref, sb_ref, nb_ref, kc_hbm, vc_hbm, o_ref,
                            kbuf, vbuf, sems, *, layer, topk, n_pages, scale):
    b = pl.program_id(0)

    def row_copies(j):
        i = idx_ref[b * topk + j]
        pg = pt_ref[b * n_pages + lax.shift_right_logical(i, 7)]
        off = i & (PAGE - 1)
        return (pltpu.make_async_copy(kc_hbm.at[layer, pg, off], kbuf.at[j], sems.at[0]),
                pltpu.make_async_copy(vc_hbm.at[layer, pg, off], vbuf.at[j], sems.at[1]))

    def start(j, c):
        ck, cv = row_copies(j)
        ck.start()
        cv.start()
        return c

    def wait(j, c):
        ck, cv = row_copies(j)
        ck.wait()
        cv.wait()
        return c

    lax.fori_loop(0, topk, start, 0)
    lax.fori_loop(0, topk, wait, 0)

    n_lanes = topk * N_HEADS
    hsel, _ = _head_lane_select(n_lanes)
    q = q_ref[0]
    kb = kbuf[...].reshape(n_lanes, HEAD_DIM).astype(BF16)
    s = lax.dot_general(q, kb, (((1,), (1,)), ((), ())), preferred_element_type=F32) * scale
    s = jnp.where(hsel, s + sb_ref[0], NEG)
    kn = kn_ref[0].astype(BF16).astype(F32)
    vn = vn_ref[0].astype(BF16).astype(F32)
    s_n = jnp.sum(q.astype(F32) * kn, axis=1, keepdims=True) * scale + nb_ref[0]
    m = jnp.maximum(jnp.max(s, axis=1, keepdims=True), s_n)
    p = jnp.exp(s - m)
    p_n = jnp.exp(s_n - m)
    l = jnp.sum(p, axis=1, keepdims=True) + p_n
    vb = vbuf[...].reshape(n_lanes, HEAD_DIM).astype(BF16)
    acc = jnp.dot(p.astype(BF16), vb, preferred_element_type=F32) + p_n.astype(BF16).astype(F32) * vn
    o_ref[0] = acc / l


def _dsa_sample_attend(layer, idx_flat, pt_flat, q, k_new, v_new, slot_bias, new_bias, cache_k, cache_v, topk):
    Bd = q.shape[0]
    n_pages = pt_flat.shape[0] // Bd
    tok = lambda w: pl.BlockSpec((1, N_HEADS, w), lambda b, idx, pt: (b, 0, 0))
    grid_spec = pltpu.PrefetchScalarGridSpec(
        num_scalar_prefetch=2, grid=(Bd,),
        in_specs=[tok(HEAD_DIM), tok(HEAD_DIM), tok(HEAD_DIM),
                  pl.BlockSpec((1, 1, topk * N_HEADS), lambda b, idx, pt: (b, 0, 0)),
                  tok(1),
                  pl.BlockSpec(memory_space=pl.ANY), pl.BlockSpec(memory_space=pl.ANY)],
        out_specs=pl.BlockSpec((1, N_HEADS, HEAD_DIM), lambda b, idx, pt: (b, 0, 0)),
        scratch_shapes=[pltpu.VMEM((topk, N_HEADS, HEAD_DIM), F32), pltpu.VMEM((topk, N_HEADS, HEAD_DIM), F32),
                        pltpu.SemaphoreType.DMA((2,))])
    return pl.pallas_call(
        functools.partial(_dsa_sample_attend_body, layer=layer, topk=topk, n_pages=n_pages,
                          scale=HEAD_DIM ** -0.5),
        grid_spec=grid_spec,
        out_shape=jax.ShapeDtypeStruct((Bd, N_HEADS, HEAD_DIM), F32),
        compiler_params=_params("arbitrary"),
    )(idx_flat, pt_flat, q, k_new, v_new, slot_bias, new_bias, cache_k, cache_v)


W_ATT = N_HEADS * HEAD_DIM
_PROJ_NAMES = ("fq", "fk", "fv", "ff", "bq", "bk", "bv", "iq", "ik", "iw", "su", "gl")
_FF_LANES = slice(0, N_HEADS)
_IW_LANES = slice(N_HEADS, N_HEADS + H_IDX)
_IK_LANES = slice(N_HEADS + H_IDX, N_HEADS + H_IDX + D_IDX)


def _stacked_weights(p):
    D = p["w_in"].shape[1]
    w_c = p["w_glu"].shape[1]
    sizes = (W_ATT, W_ATT, W_ATT, N_HEADS, W_ATT, W_ATT, W_ATT, H_IDX * D_IDX, D_IDX, H_IDX, w_c, 3 * D)
    start, off = {}, 0
    for name, n in zip(_PROJ_NAMES, sizes):
        start[name] = (off, n)
        off += n
    w_in = p["w_in"]
    grab = lambda names: w_in[:, :, start[names[0]][0]:start[names[-1]][0] + start[names[-1]][1]]
    pad = jnp.zeros(w_in.shape[:2] + (LANES - (N_HEADS + H_IDX + D_IDX),), w_in.dtype)
    runs = (("fq", "fk", "fv"), ("bq", "bk", "bv", "iq"), ("su", "gl"))
    sw = {"proj": {}}
    for names in runs:
        arr = grab(names).astype(BF16)
        for n in names:
            sw["proj"][n] = (arr, (start[n][0] - start[names[0]][0], start[n][1]))
    small = jnp.concatenate([grab(("ff",)), grab(("iw",)), grab(("ik",)), pad], axis=2).astype(BF16)
    sw["proj"]["small"] = (small, (0, LANES))
    sw["u"] = p["peer_u"].astype(BF16)
    sw["v_t"] = jnp.swapaxes(p["peer_v"], 1, 2).astype(BF16)
    return sw


def _layer_weights(l, p, sw):
    w_c = p["w_glu"].shape[1]
    lw = {"layer": l, "proj": sw["proj"], "u": sw["u"], "v_t": sw["v_t"]}
    w_glu, w_br = p["w_glu"][l], p["w_br"][l]
    lw["glu_a"], lw["glu_b"] = w_glu[:, :w_c].astype(BF16), w_glu[:, w_c:].astype(BF16)
    lw["wa"] = w_br[:W_ATT].astype(BF16)
    lw["wb"] = w_br[W_ATT:2 * W_ATT].astype(BF16)
    lw["wc"] = w_br[2 * W_ATT:].astype(BF16)
    lw["w_out"] = p["w_out"][l].astype(BF16)
    lw["wq"] = p["peer_wq"][l].astype(BF16)
    lw["keys"] = p["peer_keys"][l].reshape(2 * P_HEADS, N_KEYS, -1).astype(BF16)
    lw["s5"] = _s5_prepare(p["s5_lam_re"][l], p["s5_lam_im"][l], p["s5_log_dt"][l], p["s5_b_re"][l],
                           p["s5_b_im"][l], p["s5_c_re"][l], p["s5_c_im"][l])
    lw["s5_d"] = p["s5_d"][l]
    lw["norm1_g"], lw["norm2_g"] = p["norm1_g"][l], p["norm2_g"][l]
    return lw


def _layer(x, mods, rows_per_group, T, lw, s5_re0, s5_im0, attend):
    M, D = x.shape
    sh1, sc1, gt1, sh2, sc2, gt2 = mods
    tm = min(512, M)
    h = _norm_mod(x, lw["norm1_g"], sc1, sh1, rows_per_group)
    proj = lambda name, dts: _mm(h, lw["proj"][name][0], dts, tm=tm, tn=1024, order="nm", name="proj_" + name,
                                 cols=lw["proj"][name][1], layer=lw["layer"])
    (fq,) = proj("fq", [BF16])
    fk, fk16 = proj("fk", [F32, BF16])
    fv, fv16 = proj("fv", [F32, BF16])
    (bq,) = proj("bq", [BF16])
    bk, bk16 = proj("bk", [F32, BF16])
    bv, bv16 = proj("bv", [F32, BF16])
    (iq,) = proj("iq", [BF16])
    (su,) = proj("su", [F32])
    (small,) = proj("small", [F32])
    (gl,) = proj("gl", [F32])
    o_a, o_b, logf = attend(fq, fk, fk16, fv, fv16, bq, bk, bk16, bv, bv16, iq, small)
    yg, s5_re, s5_im = _s5(su, T, s5_re0, s5_im0, lw["s5"], lw["s5_d"])
    o_c = _glu(yg, lw["glu_a"], lw["glu_b"])
    merged = _merge(o_a, o_b, o_c, lw["wa"], lw["wb"], lw["wc"], gl)
    tn = 512
    (x1,) = _mm(merged, lw["w_out"], [F32], tm=tm, tn=tn,
                epilogue=lambda acc, x_, g_: (x_ + g_ * acc,),
                extras=[(x, (tm, tn), lambda i, j: (i, j)), _mod_extra(gt1, rows_per_group, tm, tn)],
                name="out_proj")
    h2 = _norm_mod(x1, lw["norm2_g"], sc2, sh2, rows_per_group)
    if M < LANES:
        padr = lambda a: jnp.pad(a, ((0, LANES - M), (0, 0)))
        gt2p = jnp.pad(gt2, ((0, 0), (0, LANES - M), (0, 0)))
        x2 = _peer(padr(h2), padr(x1), gt2p, LANES, lw["wq"], lw["keys"], lw["u"], lw["v_t"], lw["layer"])[:M]
    else:
        x2 = _peer(h2, x1, gt2, rows_per_group, lw["wq"], lw["keys"], lw["u"], lw["v_t"], lw["layer"])
    return x2, (fk, fv, logf, bk, bv, small[:, _IK_LANES], s5_re, s5_im)


def kernel(x_prompt, x_sample, cache_fox_k, cache_fox_v, cache_fox_logf, cache_dsa_k, cache_dsa_v,
           cache_dsa_idx_k, state_s5_re, state_s5_im, page_table, c_prompt, c_sample,
           w_ada, b_ada, norm1_g, norm2_g, w_in, b_f, rel_table, s5_lam_re, s5_lam_im, s5_log_dt,
           s5_b_re, s5_b_im, s5_c_re, s5_c_im, s5_d, w_glu, w_br, w_out,
           peer_wq, peer_keys, peer_u, peer_v, final_norm_g):
    p = dict(w_in=w_in, w_glu=w_glu, w_br=w_br, w_out=w_out, peer_wq=peer_wq, peer_keys=peer_keys,
             peer_u=peer_u, peer_v=peer_v, s5_lam_re=s5_lam_re, s5_lam_im=s5_lam_im, s5_log_dt=s5_log_dt,
             s5_b_re=s5_b_re, s5_b_im=s5_b_im, s5_c_re=s5_c_re, s5_c_im=s5_c_im, s5_d=s5_d,
             norm1_g=norm1_g, norm2_g=norm2_g)
    depth = w_in.shape[0]
    B, T, D = x_prompt.shape
    Bd = x_sample.shape[0]
    assert x_sample.shape[1] == 1
    n_pool = cache_fox_k.shape[1]
    n_pages = page_table.shape[1]
    past = n_pages * PAGE
    tile = min(512, T)
    topk_p = min(TOPK_MAX, T // 4)
    topk_s = min(TOPK_MAX, (past + 1) // 4)
    pt_flat = page_table.reshape(-1).astype(I32)
    page_pre, page_tot = _page_prefix(cache_fox_logf.reshape(depth * n_pool, PAGE * N_HEADS).astype(F32))
    cache_ki_t = jnp.swapaxes(cache_dsa_idx_k, 2, 3)
    sw = _stacked_weights(p)
    rel = rel_table.astype(F32)
    tz = _t5_tiles(rel)

    n_c = B + Bd
    c_all = jnp.pad(jnp.concatenate([c_prompt, c_sample], axis=0), ((0, (-n_c) % 8), (0, 0)))
    xp = x_prompt.reshape(B * T, D)
    xs = x_sample.reshape(Bd, D)
    rows_p, rows_s = [], []
    for l in range(depth):
        lw = _layer_weights(l, p, sw)
        m = _adaln(c_all, w_ada, b_ada, l)
        mods_p = [a[:B, None, :] for a in jnp.split(m, 6, axis=1)]
        mods_s = [a[None, B:n_c, :] for a in jnp.split(m, 6, axis=1)]
        b_f_l = b_f[l].astype(F32)

        def attend_prompt(fq, fk, fk16, fv, fv16, bq, bk, bk16, bv, bv16, iq, small):
            r3 = lambda a: a.reshape(B, T, a.shape[-1])
            small3 = r3(small)
            f_t = jnp.moveaxis(small3[:, :, _FF_LANES], -1, 1).reshape(B * N_HEADS, T)
            logf_t, cum = _gate_cumsum(f_t, jnp.tile(b_f_l, B).reshape(B * N_HEADS, 1))
            logf = jnp.moveaxis(logf_t.reshape(B, N_HEADS, T), 1, -1)
            o_a = _flash("fox", r3(fq), r3(fk16), r3(fv16), cum.reshape(B, N_HEADS, T) * LOG2E, tile=tile)
            ki_t = jnp.swapaxes(small3[:, :, _IK_LANES], 1, 2).astype(BF16)
            mask = _dsa_select(r3(iq), small3, ki_t, topk_p)
            o_b = _flash("dsa", r3(bq), r3(bk16), r3(bv16), mask, tz, tile=tile)
            return o_a.reshape(B * T, W_ATT), o_b.reshape(B * T, W_ATT), logf

        def attend_sample(fq, fk, fk16, fv, fv16, bq, bk, bk16, bv, bv16, iq, small, l=l):
            h3 = lambda a: a.reshape(Bd, N_HEADS, HEAD_DIM)
            logf = _logsig(small[:, _FF_LANES], b_f_l.reshape(1, N_HEADS))
            o_a = _fox_sample(l, pt_flat, h3(fq), h3(fk), h3(fv), logf.reshape(Bd, N_HEADS, 1),
                              cache_fox_k, cache_fox_v, page_pre, page_tot)
            qi3 = iq.reshape(Bd, H_IDX, D_IDX)
            w3 = small[:, _IW_LANES].reshape(Bd, H_IDX, 1)
            sc = _idx_scores_paged(l, pt_flat, qi3, w3, cache_ki_t)
            ki_new = jnp.pad(small[:, _IK_LANES].reshape(Bd, 1, D_IDX), ((0, 0), (0, LANES - 1), (0, 0)))
            sc_new = _idx_scores_new(qi3, w3, ki_new)
            idx, sb, nb = _dsa_sample_select(sc.reshape(Bd, past), sc_new.reshape(Bd, LANES), rel, topk_s)
            o_b = _dsa_sample_attend(l, idx.reshape(-1), pt_flat, h3(bq), h3(bk), h3(bv),
                                     sb.reshape(Bd, 1, topk_s * N_HEADS), nb.reshape(Bd, N_HEADS, 1),
                                     cache_dsa_k, cache_dsa_v, topk_s)
            return (o_a.reshape(Bd, W_ATT).astype(BF16), o_b.reshape(Bd, W_ATT).astype(BF16), logf)

        zero_state = jnp.zeros((B,) + state_s5_re.shape[2:], F32)
        xp, rp = _layer(xp, mods_p, T, T, lw, zero_state, zero_state, attend_prompt)
        xs, rs = _layer(xs, mods_s, Bd, 1, lw, state_s5_re[l], state_s5_im[l], attend_sample)
        rows_p.append(rp)
        rows_s.append(rs)

    y_prompt = _norm(xp, final_norm_g).reshape(B, T, D)
    y_sample = _norm(xs, final_norm_g).reshape(Bd, 1, D)

    def leaves(rows, nb, nt):
        fk, fv, fl, bk, bv, ik, sr, si = [jnp.stack(a) for a in zip(*rows)]
        hd = (depth, nb, nt, N_HEADS, HEAD_DIM)
        return (fk.reshape(hd), fv.reshape(hd), fl.reshape(depth, nb, nt, N_HEADS), bk.reshape(hd),
                bv.reshape(hd), ik.reshape(depth, nb, nt, D_IDX), sr, si)

    return (y_prompt, y_sample) + leaves(rows_p, B, T) + leaves(rows_s, Bd, 1)
```

```python
import functools
import math

import jax
import jax.numpy as jnp
from jax import lax
from jax.experimental import pallas as pl
from jax.experimental.pallas import tpu as pltpu

F32 = jnp.float32
BF16 = jnp.bfloat16
I32 = jnp.int32

N_HEADS = 8
HEAD_DIM = 128
H_IDX = 16
D_IDX = 64
TOPK_MAX = 256
GROUP = 16
N_STATE = 64
N_BUCKETS = 32
MAX_DIST = 128
N_KEYS = 128
P_HEADS = 8
P_TOPK = 16
PAGE = 128
EPS = 1e-6

V7X_VMEM_LIMIT_BYTES = 56 * 1024 * 1024
LANES = 128
NEG = -1e30
LOG2E = math.log2(math.e)
INT_MIN = -2 ** 31

_PAIRS = sorted([(i, j) for i in range(P_TOPK) for j in range(P_TOPK) if (i + 1) * (j + 1) <= P_TOPK],
                key=lambda p: p[0] * P_TOPK + p[1])
_N_CAND = 56


def _params(*sem):
    return pltpu.CompilerParams(dimension_semantics=sem, vmem_limit_bytes=V7X_VMEM_LIMIT_BYTES)


def _gelu(x):
    return 0.5 * x * (1.0 + jnp.tanh(math.sqrt(2.0 / math.pi) * (x + 0.044715 * (x * x * x))))


def _sigmoid(x):
    return 1.0 / (1.0 + jnp.exp(-x))


def _mm_body(*refs, n_extra, epilogue):
    a_ref, b_ref = refs[0], refs[1]
    extra = refs[2:2 + n_extra]
    outs = refs[2 + n_extra:]
    acc = jnp.dot(a_ref[...].astype(BF16), b_ref[...].astype(BF16), preferred_element_type=F32)
    vals = epilogue(acc, *[e[...] for e in extra]) if epilogue is not None else (acc,) * len(outs)
    for o, v in zip(outs, vals):
        o[...] = v.astype(o.dtype)


def _mm(a, b, out_dtypes, *, tm, tn, order="mn", epilogue=None, extras=(), name="mm", cols=None, layer=None):
    M, K = a.shape
    col0, N = cols if cols is not None else (0, b.shape[-1])
    tm, tn = min(tm, M), min(tn, N)
    assert M % tm == 0 and N % tn == 0 and col0 % tn == 0, (M, N, tm, tn, col0)
    jb0 = col0 // tn
    if order == "mn":
        grid = (M // tm, N // tn)
        ij = lambda g0, g1: (g0, g1)
    else:
        grid = (N // tn, M // tm)
        ij = lambda g0, g1: (g1, g0)
    if layer is None:
        b_spec = pl.BlockSpec((K, tn), lambda g0, g1: (0, jb0 + ij(g0, g1)[1]))
    else:
        b_spec = pl.BlockSpec((None, K, tn), lambda g0, g1: (layer, 0, jb0 + ij(g0, g1)[1]))
    in_specs = [pl.BlockSpec((tm, K), lambda g0, g1: (ij(g0, g1)[0], 0)), b_spec]
    args = [a, b]
    for arr, bshape, imap in extras:
        in_specs.append(pl.BlockSpec(bshape, lambda g0, g1, imap=imap: imap(*ij(g0, g1))))
        args.append(arr)
    out_specs = [pl.BlockSpec((tm, tn), lambda g0, g1: ij(g0, g1)) for _ in out_dtypes]
    out_shape = [jax.ShapeDtypeStruct((M, N), dt) for dt in out_dtypes]
    return pl.pallas_call(
        functools.partial(_mm_body, n_extra=len(extras), epilogue=epilogue),
        name=name, grid=grid, in_specs=in_specs, out_specs=out_specs, out_shape=out_shape,
        compiler_params=_params("parallel", "parallel"),
    )(*args)


def _mod_extra(mod, rows_per_group, tm, tn):
    r = mod.shape[1]
    return (mod, (None, r, tn), lambda i, j: ((i * tm) // rows_per_group, 0, j))


def _adaln_body(c_ref, w_ref, b_ref, o_ref):
    c = c_ref[...]
    s = c * _sigmoid(c)
    o_ref[...] = jnp.dot(s.astype(BF16), w_ref[...].astype(BF16), preferred_element_type=F32) + b_ref[...]


def _adaln(c, w_ada, b_ada, layer, tn=1024):
    R, D = c.shape
    depth, _, N = w_ada.shape
    return pl.pallas_call(
        _adaln_body, name="adaln", grid=(N // tn,),
        in_specs=[pl.BlockSpec((R, D), lambda j: (0, 0)),
                  pl.BlockSpec((None, D, tn), lambda j: (layer, 0, j)),
                  pl.BlockSpec((None, 1, tn), lambda j: (layer, 0, j))],
        out_specs=pl.BlockSpec((R, tn), lambda j: (0, j)),
        out_shape=jax.ShapeDtypeStruct((R, N), F32),
        compiler_params=_params("parallel"),
    )(c, w_ada, b_ada.reshape(depth, 1, N))


def _norm_mod_body(x_ref, g_ref, sc_ref, sh_ref, o_ref):
    x = x_ref[...]
    y = x * lax.rsqrt(jnp.mean(x * x, axis=-1, keepdims=True) + EPS) * g_ref[...]
    o_ref[...] = (y * (1.0 + sc_ref[...]) + sh_ref[...]).astype(o_ref.dtype)


def _norm_mod(x, g, sc, sh, rows_per_group, tm=512):
    M, D = x.shape
    tm = min(tm, M)
    r = sc.shape[1]
    mod_spec = pl.BlockSpec((None, r, D), lambda i: ((i * tm) // rows_per_group, 0, 0))
    return pl.pallas_call(
        _norm_mod_body, name="norm_mod", grid=(M // tm,),
        in_specs=[pl.BlockSpec((tm, D), lambda i: (i, 0)), pl.BlockSpec((1, D), lambda i: (0, 0)),
                  mod_spec, mod_spec],
        out_specs=pl.BlockSpec((tm, D), lambda i: (i, 0)),
        out_shape=jax.ShapeDtypeStruct((M, D), BF16),
        compiler_params=_params("parallel"),
    )(x, g.reshape(1, D), sc, sh)


def _norm_body(x_ref, g_ref, o_ref):
    x = x_ref[...]
    o_ref[...] = x * lax.rsqrt(jnp.mean(x * x, axis=-1, keepdims=True) + EPS) * g_ref[...]


def _norm(x, g, tm=512):
    M, D = x.shape
    tm = min(tm, M)
    return pl.pallas_call(
        _norm_body, name="final_norm", grid=(M // tm,),
        in_specs=[pl.BlockSpec((tm, D), lambda i: (i, 0)), pl.BlockSpec((1, D), lambda i: (0, 0))],
        out_specs=pl.BlockSpec((tm, D), lambda i: (i, 0)),
        out_shape=jax.ShapeDtypeStruct((M, D), F32),
        compiler_params=_params("parallel"),
    )(x, g.reshape(1, D))


def _log_sigmoid(z):
    return jnp.minimum(z, 0.0) - jnp.log(1.0 + jnp.exp(-jnp.abs(z)))


def _gate_body(f_ref, b_ref, logf_ref, cum_ref):
    logf = _log_sigmoid(f_ref[...] + b_ref[...])
    logf_ref[...] = logf
    T = logf.shape[1]
    lane = lax.broadcasted_iota(I32, logf.shape, 1)
    x = logf
    sh = 1
    while sh < T:
        x = x + jnp.where(lane >= sh, pltpu.roll(x, sh, axis=1), 0.0)
        sh *= 2
    cum_ref[...] = x


def _gate_cumsum(f_t, b_col):
    R, T = f_t.shape
    return pl.pallas_call(
        _gate_body, name="fox_gate",
        out_shape=[jax.ShapeDtypeStruct((R, T), F32), jax.ShapeDtypeStruct((R, T), F32)],
        compiler_params=pltpu.CompilerParams(vmem_limit_bytes=V7X_VMEM_LIMIT_BYTES),
    )(f_t, b_col)


def _flash_body(*refs, mode, tile, scale):
    if mode == "fox":
        q_ref, k_ref, v_ref, fk_ref, o_ref, m_sc, acc_sc = refs
    else:
        q_ref, k_ref, v_ref, mask_ref, tz_ref, o_ref, m_sc, acc_sc = refs
    qi = pl.program_id(1)
    ki = pl.program_id(2)

    @pl.when(ki == 0)
    def _():
        m_sc[...] = jnp.full(m_sc.shape, -jnp.inf, F32)
        acc_sc[...] = jnp.zeros(acc_sc.shape, F32)

    nsub = tile // LANES

    def t5_bias(h, near):
        zero = jnp.zeros((LANES, LANES), F32)
        if near == "diag":
            pick = lambda a, b: tz_ref[h, 0] if a == b else (tz_ref[h, 1] if a == b + 1 else zero)
        else:
            pick = lambda a, b: tz_ref[h, 1] if (a == 0 and b == nsub - 1) else zero
        rows = [jnp.concatenate([pick(a, b) for b in range(nsub)], axis=1) if nsub > 1 else pick(a, 0)
                for a in range(nsub)]
        return jnp.concatenate(rows, axis=0) if nsub > 1 else rows[0]

    def step(near):
        if mode == "fox":
            if near == "diag":
                row = lax.broadcasted_iota(I32, (tile, tile), 0)
                col = lax.broadcasted_iota(I32, (tile, tile), 1)
                keep = row >= col
        else:
            shared = mask_ref[0]
        ones = jnp.ones((tile, HEAD_DIM), BF16)
        for h in range(N_HEADS):
            sl = slice(h * HEAD_DIM, (h + 1) * HEAD_DIM)
            s = lax.dot_general(q_ref[0, :, sl], k_ref[0, :, sl], (((1,), (1,)), ((), ())),
                                preferred_element_type=F32) * (scale * LOG2E)
            if mode == "fox":
                s = s - fk_ref[0, h:h + 1, :]
                if near == "diag":
                    s = jnp.where(keep, s, NEG)
            else:
                s = s + shared
                if near != "far":
                    s = s + t5_bias(h, near)
            m_prev = m_sc[h]
            m_new = jnp.maximum(m_prev, jnp.max(s, axis=1, keepdims=True))
            alpha = jnp.exp2(m_prev - m_new)
            p = jnp.exp2(s - jnp.concatenate([m_new] * nsub, axis=1))
            pv = jnp.dot(p.astype(BF16), jnp.concatenate([v_ref[0, :, sl], ones], axis=1),
                         preferred_element_type=F32)
            acc_sc[h] = jnp.concatenate([alpha] * (2 * HEAD_DIM // LANES), axis=1) * acc_sc[h] + pv
            m_sc[h] = m_new

    if mode == "fox":
        pl.when(ki < qi)(lambda: step("far"))
        pl.when(ki == qi)(lambda: step("diag"))
    else:
        pl.when(ki < qi - 1)(lambda: step("far"))
        pl.when(ki == qi - 1)(lambda: step("next"))
        pl.when(ki == qi)(lambda: step("diag"))

    @pl.when(ki == qi)
    def _():
        for h in range(N_HEADS):
            sl = slice(h * HEAD_DIM, (h + 1) * HEAD_DIM)
            o_ref[0, :, sl] = (acc_sc[h, :, :HEAD_DIM] / acc_sc[h, :, HEAD_DIM:]).astype(o_ref.dtype)


def _flash(mode, q, k, v, *side, tile):
    B, T, W = q.shape
    nt = T // tile
    qspec = pl.BlockSpec((1, tile, W), lambda b, qi, ki: (b, qi, 0))
    kspec = pl.BlockSpec((1, tile, W), lambda b, qi, ki: (b, jnp.minimum(ki, qi), 0))
    if mode == "fox":
        side_specs = [pl.BlockSpec((1, N_HEADS, tile), lambda b, qi, ki: (b, 0, jnp.minimum(ki, qi)))]
    else:
        side_specs = [pl.BlockSpec((1, tile, tile), lambda b, qi, ki: (b, qi, jnp.minimum(ki, qi))),
                      pl.BlockSpec((N_HEADS, 2, LANES, LANES), lambda b, qi, ki: (0, 0, 0, 0))]
    return pl.pallas_call(
        functools.partial(_flash_body, mode=mode, tile=tile, scale=HEAD_DIM ** -0.5),
        name="flash_" + mode, grid=(B, nt, nt),
        in_specs=[qspec, kspec, kspec] + side_specs,
        out_specs=pl.BlockSpec((1, tile, W), lambda b, qi, ki: (b, qi, 0)),
        out_shape=jax.ShapeDtypeStruct((B, T, W), BF16),
        scratch_shapes=[pltpu.VMEM((N_HEADS, tile, LANES), F32), pltpu.VMEM((N_HEADS, tile, 2 * HEAD_DIM), F32)],
        compiler_params=_params("parallel", "parallel", "arbitrary"),
    )(q, k, v, *side)


def _sortable_key(x):
    bits = pltpu.bitcast(x, I32)
    return jnp.where(bits < 0, bits ^ jnp.int32(0x7FFFFFFF), bits)


def _kth_largest_key(count_ge, shape, k):
    def body(b, cur):
        cand = cur | lax.shift_left(jnp.int32(1), 31 - b)
        cnt = count_ge(cand ^ jnp.int32(INT_MIN))
        return jnp.where(cnt >= k, cand, cur)
    cur = lax.fori_loop(0, 32, body, jnp.zeros(shape, I32))
    return cur ^ jnp.int32(INT_MIN)


def _dsa_select_body(qi_ref, small_ref, kit_ref, mask_ref, key_sc, cut_sc, wrep_sc, *, tq, cw, bw, T, topk):
    q0 = pl.program_id(1) * tq
    w = small_ref[0][:, N_HEADS:N_HEADS + H_IDX] * (H_IDX ** -0.5 * D_IDX ** -0.5)
    bucket = (q0 + tq + bw - 1) // bw
    n_valid = bucket * (bw // cw)
    row_c = q0 + lax.broadcasted_iota(I32, (tq, cw), 0)
    lane_c = lax.broadcasted_iota(I32, (tq, cw), 1)

    for h in range(H_IDX):
        wrep_sc[:, h * LANES:(h + 1) * LANES] = jnp.broadcast_to(w[:, h:h + 1], (tq, LANES))

    def score_chunk(c, carry):
        c0 = pl.multiple_of(c * cw, cw)
        kt = kit_ref[0, :, pl.ds(c0, cw)]
        acc = jnp.zeros((tq, cw), F32)
        for h in range(H_IDX):
            d = jnp.dot(qi_ref[0, :, h * D_IDX:(h + 1) * D_IDX], kt, preferred_element_type=F32)
            wh = jnp.concatenate([wrep_sc[:, h * LANES:(h + 1) * LANES]] * (cw // LANES), axis=1)
            acc = acc + wh * jnp.maximum(d, 0.0)
        key_sc[:, pl.ds(c0, cw)] = jnp.where(c0 + lane_c <= row_c, _sortable_key(acc), jnp.int32(INT_MIN))
        return carry

    lax.fori_loop(0, n_valid, score_chunk, 0)

    lane = lax.broadcasted_iota(I32, (tq, LANES), 1)
    row = q0 + lax.broadcasted_iota(I32, (tq, LANES), 0)
    one = lambda pred: jnp.where(pred, 1, 0)

    def select(n_lt):
        def count(ind):
            part = jnp.zeros((tq, LANES), I32)
            for c in range(n_lt):
                part = part + ind(key_sc[:, c * LANES:(c + 1) * LANES], c * LANES + lane)
            return jnp.sum(part, axis=1, keepdims=True)

        thr = _kth_largest_key(lambda t: count(lambda k, col: one(k >= t)), (tq, 1), topk)
        n_gt = count(lambda k, col: one(k > thr))
        n_ge = count(lambda k, col: one(k >= thr))
        excess = jnp.where(thr > jnp.int32(INT_MIN), n_ge - topk, 0)
        cut_sc[...] = jnp.full((tq, 1), T, I32)

        @pl.when(jnp.max(excess) > 0)
        def _():
            need = topk - n_gt
            nbits = max(1, (T - 1).bit_length())

            def body(b, cut):
                cand = cut | lax.shift_left(jnp.int32(1), nbits - 1 - b)
                cnt = count(lambda k, col: jnp.where(k == thr, one(col < cand), 0))
                return jnp.where(cnt < need, cand, cut)
            cut = lax.fori_loop(0, nbits, body, jnp.zeros((tq, 1), I32))
            cut_sc[...] = jnp.where(excess > 0, cut, T)

        cut = cut_sc[...]
        for c in range(n_lt):
            k = key_sc[:, c * LANES:(c + 1) * LANES]
            col = c * LANES + lane
            val = jnp.where(k > thr, 0.0, jnp.where(k == thr, jnp.where(col <= cut, 0.0, NEG), NEG))
            mask_ref[0, :, c * LANES:(c + 1) * LANES] = jnp.where(col <= row, val, NEG)
        if n_lt * LANES < T:
            mask_ref[0, :, n_lt * LANES:] = jnp.full((tq, T - n_lt * LANES), NEG, F32)

    for k in range(1, T // bw + 1):
        pl.when(bucket == k)(functools.partial(select, k * bw // LANES))


def _dsa_select(qi, small, ki_t, topk, tq=256, cw=256, bw=512):
    B, T, _ = qi.shape
    tq, cw, bw = min(tq, T), min(cw, T), min(bw, T)
    assert bw % cw == 0 and T % bw == 0
    return pl.pallas_call(
        functools.partial(_dsa_select_body, tq=tq, cw=cw, bw=bw, T=T, topk=topk),
        name="dsa_select", grid=(B, T // tq),
        in_specs=[pl.BlockSpec((1, tq, H_IDX * D_IDX), lambda b, i: (b, i, 0)),
                  pl.BlockSpec((1, tq, LANES), lambda b, i: (b, i, 0)),
                  pl.BlockSpec((1, D_IDX, T), lambda b, i: (b, 0, 0))],
        out_specs=pl.BlockSpec((1, tq, T), lambda b, i: (b, i, 0)),
        out_shape=jax.ShapeDtypeStruct((B, T, T), F32),
        scratch_shapes=[pltpu.VMEM((tq, T), I32), pltpu.VMEM((tq, 1), I32), pltpu.VMEM((tq, H_IDX * LANES), F32)],
        compiler_params=_params("parallel", "parallel"),
    )(qi, small, ki_t)


def _t5_bucket(dist):
    n = jnp.maximum(dist, 0)
    max_exact = N_BUCKETS // 2
    nf = jnp.maximum(n, 1).astype(F32)
    large = max_exact + (jnp.log(nf / max_exact) / math.log(MAX_DIST / max_exact)
                         * (N_BUCKETS - max_exact)).astype(I32)
    large = jnp.minimum(large, N_BUCKETS - 1)
    return jnp.where(n < max_exact, n, large)


def _t5_tiles(rel_table):
    assert LANES >= MAX_DIST
    i = jnp.arange(LANES)
    bucket = _t5_bucket((jnp.arange(2) * LANES)[:, None, None] + i[None, :, None] - i[None, None, :])
    rel = rel_table.astype(F32)
    tz = jnp.zeros((rel.shape[1],) + bucket.shape, F32)
    for k in range(N_BUCKETS - 1):
        tz = tz + jnp.where(bucket[None] == k, (rel[k] - rel[N_BUCKETS - 1])[:, None, None, None], 0.0)
    return tz * LOG2E


def _s5_disc_body(lr_ref, li_ref, ldt_ref, br_ref, bi_ref, abr_ref, abi_ref, bbr_ref, bbi_ref):
    lr, li = lr_ref[...], li_ref[...]
    dt = jnp.exp(ldt_ref[...])
    mag = jnp.exp(lr * dt)
    ab_re, ab_im = mag * jnp.cos(li * dt), mag * jnp.sin(li * dt)
    den = lr * lr + li * li
    nr = ab_re - 1.0
    k_re = (nr * lr + ab_im * li) / den
    k_im = (ab_im * lr - nr * li) / den
    br, bi = br_ref[...], bi_ref[...]
    abr_ref[...] = ab_re
    abi_ref[...] = ab_im
    bbr_ref[...] = k_re * br - k_im * bi
    bbi_ref[...] = k_re * bi + k_im * br


def _s5_discretise(lam_re, lam_im, log_dt, b_re, b_im):
    G, N, P = b_re.shape
    rep = lambda a: jnp.broadcast_to(a.reshape(G * N, 1), (G * N, P))
    ldt = jnp.broadcast_to(log_dt.reshape(G, 1, 1), (G, N, P)).reshape(G * N, P)
    shp = jax.ShapeDtypeStruct((G * N, P), F32)
    abr, abi, bbr, bbi = pl.pallas_call(
        _s5_disc_body, name="s5_discretise", out_shape=[shp] * 4,
        compiler_params=pltpu.CompilerParams(vmem_limit_bytes=V7X_VMEM_LIMIT_BYTES),
    )(rep(lam_re), rep(lam_im), ldt, b_re.reshape(G * N, P), b_im.reshape(G * N, P))
    return (abr[:, 0].reshape(G, N), abi[:, 0].reshape(G, N),
            bbr.reshape(G, N, P), bbi.reshape(G, N, P))


def _s5_body(u_ref, bc_ref, cc_ref, a_ref, d_ref, x0_ref, y_ref, xT_ref, xs_sc, carry_sc, *, tc, S, lc):
    c = pl.program_id(1)

    @pl.when(c == 0)
    def _():
        carry_sc[...] = x0_ref[0]

    n_ct = u_ref.shape[2] // LANES
    ks = S // n_ct
    ch = lambda t: slice(t * LANES, (t + 1) * LANES)
    st = lambda j: slice(j * ks, (j + 1) * ks)
    for j in range(2 * n_ct):
        xs_sc[:, st(j)] = jnp.dot(u_ref[0, :, ch(j % n_ct)].astype(BF16), bc_ref[j], preferred_element_type=F32)

    for j in range(S // lc):
        slr = slice(j * lc, (j + 1) * lc)
        sli = slice(S + j * lc, S + (j + 1) * lc)
        ar, ai = a_ref[:, slr], a_ref[:, sli]

        def step(t, carry, slr=slr, sli=sli, ar=ar, ai=ai):
            xr, xi = carry
            nr = ar * xr - ai * xi + xs_sc[pl.ds(t, 1), slr]
            ni = ar * xi + ai * xr + xs_sc[pl.ds(t, 1), sli]
            xs_sc[pl.ds(t, 1), slr] = nr
            xs_sc[pl.ds(t, 1), sli] = ni
            return nr, ni

        xr, xi = lax.fori_loop(0, tc, step, (carry_sc[:, slr], carry_sc[:, sli]), unroll=min(8, tc))
        carry_sc[:, slr] = xr
        carry_sc[:, sli] = xi

    for t in range(n_ct):
        y = (jnp.dot(xs_sc[:, st(t)].astype(BF16), cc_ref[t], preferred_element_type=F32)
             + jnp.dot(xs_sc[:, st(n_ct + t)].astype(BF16), cc_ref[n_ct + t], preferred_element_type=F32)
             + d_ref[:, ch(t)] * u_ref[0, :, ch(t)])
        y_ref[0, :, ch(t)] = _gelu(y).astype(y_ref.dtype)
    xT_ref[0] = carry_sc[...]


def _glu_body(y_ref, wa_ref, wb_ref, o_ref):
    y = y_ref[...]
    a = jnp.dot(y, wa_ref[...], preferred_element_type=F32)
    b = jnp.dot(y, wb_ref[...], preferred_element_type=F32)
    o_ref[...] = (a * _sigmoid(b)).astype(o_ref.dtype)


def _glu(y, wa, wb, tm=512):
    M, K = y.shape
    N = wa.shape[1]
    tm = min(tm, M)
    return pl.pallas_call(
        _glu_body, name="s5_glu", grid=(M // tm,),
        in_specs=[pl.BlockSpec((tm, K), lambda i: (i, 0)), pl.BlockSpec((K, N), lambda i: (0, 0)),
                  pl.BlockSpec((K, N), lambda i: (0, 0))],
        out_specs=pl.BlockSpec((tm, N), lambda i: (i, 0)),
        out_shape=jax.ShapeDtypeStruct((M, N), BF16),
        compiler_params=_params("parallel"),
    )(y, wa, wb)


def _merge_body(oa_ref, ob_ref, oc_ref, wa_ref, wb_ref, wc_ref, ga_ref, gb_ref, gc_ref, o_ref):
    dot = lambda x, w: jnp.dot(x[...], w[...], preferred_element_type=F32)
    m = (_sigmoid(ga_ref[...]) * dot(oa_ref, wa_ref) + _sigmoid(gb_ref[...]) * dot(ob_ref, wb_ref)
         + _sigmoid(gc_ref[...]) * dot(oc_ref, wc_ref))
    o_ref[...] = m.astype(o_ref.dtype)


def _merge(o_a, o_b, o_c, wa, wb, wc, gl, tm=512, tn=1024):
    M, K = o_a.shape
    D = wa.shape[1]
    tm = min(tm, M)
    nd = D // tn
    ospec = pl.BlockSpec((tm, K), lambda i, j: (i, 0))
    wspec = pl.BlockSpec((K, tn), lambda i, j: (0, j))
    gspec = lambda g: pl.BlockSpec((tm, tn), lambda i, j, g=g: (i, g * nd + j))
    return pl.pallas_call(
        _merge_body, name="branch_merge", grid=(M // tm, nd),
        in_specs=[ospec, ospec, ospec, wspec, wspec, wspec, gspec(0), gspec(1), gspec(2)],
        out_specs=pl.BlockSpec((tm, tn), lambda i, j: (i, j)),
        out_shape=jax.ShapeDtypeStruct((M, D), BF16),
        compiler_params=_params("parallel", "parallel"),
    )(o_a, o_b, o_c, wa, wb, wc, gl, gl, gl)


def _bf16_pair(x):
    hi = pltpu.bitcast(x.astype(BF16).astype(F32), I32)
    return hi | lax.shift_right_logical(hi, 16)


def _top16_rows(s, n_rows):
    iota = lax.broadcasted_iota(I32, s.shape, 0)
    rank = jnp.full(s.shape, P_TOPK, I32)
    vals = []
    work = s
    for k in range(P_TOPK):
        m = jnp.max(work, axis=0, keepdims=True)
        idx = jnp.min(jnp.where(work == m, iota, n_rows), axis=0, keepdims=True)
        hit = iota == idx
        rank = jnp.where(hit, k, rank)
        work = jnp.where(hit, -jnp.inf, work)
        vals.append(m)
    return rank, vals


def _peer_select_body(q_ref, keys_ref, a1_ref, n1_ref, a2_ref, r2_ref, cand_sc, sel_sc):
    tt = q_ref.shape[0]
    nt = (((1,), (1,)), ((), ()))
    for h in range(P_HEADS):
        s1 = lax.dot_general(keys_ref[2 * h], q_ref[:, (2 * h) * N_KEYS:(2 * h + 1) * N_KEYS], nt,
                             preferred_element_type=F32)
        s2 = lax.dot_general(keys_ref[2 * h + 1], q_ref[:, (2 * h + 1) * N_KEYS:(2 * h + 2) * N_KEYS], nt,
                             preferred_element_type=F32)
        r1, v1 = _top16_rows(s1, N_KEYS)
        r2, v2 = _top16_rows(s2, N_KEYS)
        for r, (i, j) in enumerate(_PAIRS):
            cand_sc[r:r + 1, :] = v1[i] + v2[j]
        cand_sc[len(_PAIRS):, :] = jnp.full((_N_CAND - len(_PAIRS), tt), -jnp.inf, F32)
        rc, cv = _top16_rows(cand_sc[...], _N_CAND)
        z = jnp.zeros((1, tt), F32)
        for k in range(P_TOPK):
            z = z + jnp.exp(cv[k] - cv[0])
        sel_sc[...] = jnp.where(rc < P_TOPK, 1, 0)
        cnt = [jnp.zeros((1, tt), I32) for _ in range(P_TOPK)]
        for r, (i, j) in enumerate(_PAIRS):
            cnt[i] = cnt[i] + sel_sc[r:r + 1, :]
        n1 = jnp.zeros((N_KEYS, tt), I32)
        for i in range(P_TOPK):
            n1 = jnp.where(r1 == i, cnt[i], n1)
        a1_ref[h] = _bf16_pair(jnp.exp(s1 - v1[0]) / z)
        n1_ref[h] = _bf16_pair(n1.astype(F32))
        a2_ref[h] = jnp.exp(s2 - v2[0]).astype(a2_ref.dtype)
        r2_ref[h] = r2.astype(F32).astype(r2_ref.dtype)


def _peer_select(q, keys, tt=256):
    M = q.shape[0]
    tt = min(tt, M)
    tab = pl.BlockSpec((P_HEADS, N_KEYS, tt), lambda i: (0, 0, i))
    shp = lambda dt: jax.ShapeDtypeStruct((P_HEADS, N_KEYS, M), dt)
    return pl.pallas_call(
        _peer_select_body, name="peer_select", grid=(M // tt,),
        in_specs=[pl.BlockSpec((tt, q.shape[1]), lambda i: (i, 0)),
                  pl.BlockSpec(keys.shape, lambda i: (0, 0, 0))],
        out_specs=[tab, tab, tab, tab],
        out_shape=[shp(I32), shp(I32), shp(BF16), shp(BF16)],
        scratch_shapes=[pltpu.VMEM((_N_CAND, tt), F32), pltpu.VMEM((_N_CAND, tt), I32)],
        compiler_params=_params("parallel"),
    )(q, keys)


def _peer_dense_body(h_ref, u_ref, vt_ref, a1_ref, n1_ref, a2_ref, r2_ref, x_ref, gt_ref, o_ref,
                     acc_sc, act_sc, ga_sc, *, te):
    j = pl.program_id(1)
    tt = h_ref.shape[0]

    @pl.when(j == 0)
    def _():
        acc_sc[...] = jnp.zeros(acc_sc.shape, F32)
        act_sc[1] = jnp.zeros(act_sc.shape[1:], F32)

    def step(slot):
        act_sc[slot] = lax.dot_general(u_ref[...], h_ref[...], (((1,), (1,)), ((), ())),
                                       preferred_element_type=F32)
        for r in range(te // N_KEYS):
            rows = slice(r * N_KEYS, (r + 1) * N_KEYS)
            wt = None
            for h in range(P_HEADS):
                tile_rows = lambda ref: pltpu.bitcast(jnp.broadcast_to(ref[h, r:r + 1, :], (8, tt)), BF16)
                a1 = jnp.concatenate([tile_rows(a1_ref)] * (N_KEYS // 16), axis=0)
                n1 = jnp.concatenate([tile_rows(n1_ref)] * (N_KEYS // 16), axis=0)
                term = jnp.where(r2_ref[h] < n1, a1 * a2_ref[h], jnp.zeros((), BF16))
                wt = term if wt is None else wt + term
            ga_sc[rows, :] = wt * _gelu(act_sc[1 - slot, rows, :].astype(BF16))
        acc_sc[...] += jnp.dot(vt_ref[...], ga_sc[...], preferred_element_type=F32)

    pl.when(lax.rem(j, 2) == 0)(functools.partial(step, 0))
    pl.when(lax.rem(j, 2) == 1)(functools.partial(step, 1))

    @pl.when(j == pl.num_programs(1) - 1)
    def _():
        o_ref[...] = x_ref[...] + gt_ref[...] * acc_sc[...].T


def _peer_dense(h2, u, v_t, layer, tabs, x, gt, rows_per_group, tt=512, te=1024):
    M, D = h2.shape
    E = u.shape[1]
    tt = min(tt, M)
    n_e1 = te // N_KEYS
    nc = E // te
    chunk = lambda j: jnp.clip(j, 0, nc - 1)
    tab1 = pl.BlockSpec((P_HEADS, n_e1, tt), lambda i, j: (0, chunk(j - 1), i))
    tab2 = pl.BlockSpec((P_HEADS, N_KEYS, tt), lambda i, j: (0, 0, i))
    r = gt.shape[1]
    return pl.pallas_call(
        functools.partial(_peer_dense_body, te=te),
        name="peer_dense", grid=(M // tt, nc + 1),
        in_specs=[pl.BlockSpec((tt, D), lambda i, j: (i, 0)),
                  pl.BlockSpec((None, te, D), lambda i, j: (layer, chunk(j), 0)),
                  pl.BlockSpec((None, D, te), lambda i, j: (layer, 0, chunk(j - 1))),
                  tab1, tab1, tab2, tab2,
                  pl.BlockSpec((tt, D), lambda i, j: (i, 0)),
                  pl.BlockSpec((None, r, D), lambda i, j: ((i * tt) // rows_per_group, 0, 0))],
        out_specs=pl.BlockSpec((tt, D), lambda i, j: (i, 0)),
        out_shape=jax.ShapeDtypeStruct((M, D), F32),
        scratch_shapes=[pltpu.VMEM((D, tt), F32), pltpu.VMEM((2, te, tt), F32), pltpu.VMEM((te, tt), BF16)],
        compiler_params=_params("parallel", "arbitrary"),
    )(h2, u, v_t, *tabs, x, gt)


def _peer(h2, x, gt, rows_per_group, wq, keys, u, v_t, layer):
    (q,) = _mm(h2, wq, [BF16], tm=512, tn=1024, name="peer_query")
    tabs = _peer_select(q, keys)
    return _peer_dense(h2, u, v_t, layer, tabs, x, gt, rows_per_group)


def _s5_prepare(lam_re, lam_im, log_dt, b_re, b_im, c_re, c_im):
    G, N, P = b_re.shape
    S, W = G * N, G * P
    ab_re, ab_im, bb_re, bb_im = _s5_discretise(lam_re, lam_im, log_dt, b_re, b_im)
    cg = LANES // P
    n_ct = G // cg
    eye = jnp.eye(cg, dtype=F32)

    def tiles(m, rows, cols):
        t = jnp.swapaxes(m, 1, 2).reshape(n_ct, cg, m.shape[2], m.shape[1])
        return (t[:, :, :, None, :] * eye[None, :, None, :, None]).reshape(n_ct, rows, cols)

    bc = jnp.concatenate([tiles(bb_re, LANES, cg * N), tiles(bb_im, LANES, cg * N)], axis=0).astype(BF16)
    cc = jnp.concatenate([tiles(c_re.astype(F32), cg * N, LANES), -tiles(c_im.astype(F32), cg * N, LANES)],
                         axis=0).astype(BF16)
    a_row = jnp.concatenate([ab_re.reshape(1, S), ab_im.reshape(1, S)], axis=1)
    return a_row, bc, cc


def _s5(u, T, x0_re, x0_im, prep, d_skip, tc=128, lc=1024):
    M, W = u.shape
    B = M // T
    G, N = x0_re.shape[1:]
    S = G * N
    a_row, bc, cc = prep
    tc = min(tc, T)
    x0 = jnp.concatenate([x0_re.reshape(B, 1, S), x0_im.reshape(B, 1, S)], axis=2).astype(F32)
    yg, x_last = pl.pallas_call(
        functools.partial(_s5_body, tc=tc, S=S, lc=lc),
        name="s5", grid=(B, T // tc),
        in_specs=[pl.BlockSpec((1, tc, W), lambda b, c: (b, c, 0)),
                  pl.BlockSpec(bc.shape, lambda b, c: (0, 0, 0)),
                  pl.BlockSpec(cc.shape, lambda b, c: (0, 0, 0)),
                  pl.BlockSpec((1, 2 * S), lambda b, c: (0, 0)),
                  pl.BlockSpec((1, W), lambda b, c: (0, 0)),
                  pl.BlockSpec((1, 1, 2 * S), lambda b, c: (b, 0, 0))],
        out_specs=[pl.BlockSpec((1, tc, W), lambda b, c: (b, c, 0)),
                   pl.BlockSpec((1, 1, 2 * S), lambda b, c: (b, 0, 0))],
        out_shape=[jax.ShapeDtypeStruct((B, T, W), BF16), jax.ShapeDtypeStruct((B, 1, 2 * S), F32)],
        scratch_shapes=[pltpu.VMEM((tc, 2 * S), F32), pltpu.VMEM((1, 2 * S), F32)],
        compiler_params=_params("parallel", "arbitrary"),
    )(u.reshape(B, T, W), bc, cc, a_row, d_skip.reshape(1, W).astype(F32), x0)
    return yg.reshape(M, W), x_last[:, 0, :S].reshape(B, G, N), x_last[:, 0, S:].reshape(B, G, N)


def _logsig_body(f_ref, b_ref, o_ref):
    o_ref[...] = _log_sigmoid(f_ref[...] + b_ref[...])


def _logsig(f, b_row):
    return pl.pallas_call(_logsig_body, out_shape=jax.ShapeDtypeStruct(f.shape, F32))(f, b_row)


def _head_lane_select(n_lanes):
    lane = lax.broadcasted_iota(I32, (N_HEADS, n_lanes), 1)
    head = lax.broadcasted_iota(I32, (N_HEADS, n_lanes), 0)
    return (lane & (N_HEADS - 1)) == head, lane


def _page_prefix_body(lf_ref, pre_ref, tot_ref):
    lf = lf_ref[...]
    n_lanes = lf.shape[1]
    lane = lax.broadcasted_iota(I32, lf.shape, 1)
    pre, tot = lf, lf
    sh = N_HEADS
    while sh < n_lanes:
        pre = pre + jnp.where(lane >= sh, pltpu.roll(pre, sh, axis=1), 0.0)
        tot = tot + pltpu.roll(tot, sh, axis=1)
        sh *= 2
    pre_ref[...] = pre
    tot_ref[...] = tot


def _page_prefix(logf_rows):
    R, n_lanes = logf_rows.shape
    tr = math.gcd(R, 256)
    spec = pl.BlockSpec((tr, n_lanes), lambda i: (i, 0))
    return pl.pallas_call(
        _page_prefix_body, name="fox_page_prefix", grid=(R // tr,), in_specs=[spec], out_specs=[spec, spec],
        out_shape=[jax.ShapeDtypeStruct((R, n_lanes), F32)] * 2,
        compiler_params=_params("parallel"),
    )(logf_rows)


def _fox_sample_body(pt_ref, q_ref, kn_ref, vn_ref, lfn_ref, k_hbm, v_hbm, pre_hbm, tot_hbm, o_ref,
                     kbuf, vbuf, pbuf, tbuf, sems, m_sc, l_sc, acc_sc, carry_sc,
                     *, layer, n_pool, n_pages, pp, scale):
    g = pl.program_id(1)
    ng = pl.num_programs(1)
    step = pl.program_id(0) * ng + g
    n_steps = pl.num_programs(0) * ng
    slot = lax.rem(step, 2)
    page_lanes = PAGE * N_HEADS
    n_lanes = pp * page_lanes
    nt = (((1,), (1,)), ((), ()))

    def copies(st, sl):
        first = (st // ng) * n_pages + lax.rem(st, ng) * pp
        out = []
        for k in range(pp):
            pg = pt_ref[first + k]
            row = layer * n_pool + pg
            out += [pltpu.make_async_copy(k_hbm.at[layer, pg], kbuf.at[sl, pl.ds(k * PAGE, PAGE)], sems.at[0, sl]),
                    pltpu.make_async_copy(v_hbm.at[layer, pg], vbuf.at[sl, pl.ds(k * PAGE, PAGE)], sems.at[1, sl]),
                    pltpu.make_async_copy(pre_hbm.at[pl.ds(row, 1)], pbuf.at[sl, pl.ds(k, 1)], sems.at[2, sl]),
                    pltpu.make_async_copy(tot_hbm.at[pl.ds(row, 1)], tbuf.at[sl, pl.ds(k, 1)], sems.at[3, sl])]
        return out

    @pl.when(step == 0)
    def _():
        for c in copies(step, slot):
            c.start()

    @pl.when(step + 1 < n_steps)
    def _():
        for c in copies(step + 1, 1 - slot):
            c.start()

    @pl.when(g == 0)
    def _():
        m_sc[...] = jnp.full(m_sc.shape, -jnp.inf, F32)
        l_sc[...] = jnp.zeros(l_sc.shape, F32)
        acc_sc[...] = jnp.zeros(acc_sc.shape, F32)
        carry_sc[...] = jnp.zeros(carry_sc.shape, F32)

    for c in copies(step, slot):
        c.wait()

    hsel, lane8 = _head_lane_select(n_lanes)
    q = q_ref[0]
    kp = kbuf[slot].reshape(n_lanes, HEAD_DIM).astype(BF16)
    s = lax.dot_general(q, kp, nt, preferred_element_type=F32) * scale
    carry = carry_sc[...]
    f_pages = []
    for k in range(pp):
        f_pages.append(carry + pbuf[slot, k:k + 1, :])
        carry = carry + tbuf[slot, k:k + 1, :]
    carry_sc[...] = carry
    f_k = jnp.concatenate(f_pages, axis=1) if pp > 1 else f_pages[0]
    s = jnp.where(hsel, s - f_k, NEG)
    m_prev = m_sc[...]
    m_new = jnp.maximum(m_prev, jnp.max(s, axis=1, keepdims=True))
    alpha = jnp.exp(m_prev - m_new)
    pr = jnp.exp(s - m_new)
    l_sc[...] = alpha * l_sc[...] + jnp.sum(pr, axis=1, keepdims=True)
    vp = vbuf[slot].reshape(n_lanes, HEAD_DIM).astype(BF16)
    acc_sc[...] = alpha * acc_sc[...] + jnp.dot(pr.astype(BF16), vp, preferred_element_type=F32)
    m_sc[...] = m_new

    @pl.when(g == ng - 1)
    def _():
        hsel_p, lane_p = _head_lane_select(page_lanes)
        f_col = jnp.sum(jnp.where(hsel_p, jnp.where(lane_p < N_HEADS, carry_sc[...], 0.0), 0.0),
                        axis=1, keepdims=True)
        kn = kn_ref[0].astype(BF16).astype(F32)
        vn = vn_ref[0].astype(BF16).astype(F32)
        s_n = jnp.sum(q.astype(F32) * kn, axis=1, keepdims=True) * scale - (f_col + lfn_ref[0])
        m_prev = m_sc[...]
        m_new = jnp.maximum(m_prev, s_n)
        alpha = jnp.exp(m_prev - m_new)
        p_n = jnp.exp(s_n - m_new)
        l = alpha * l_sc[...] + p_n
        acc = alpha * acc_sc[...] + p_n.astype(BF16).astype(F32) * vn
        o_ref[0] = acc / l


def _fox_sample(layer, pt_flat, q, k_new, v_new, logf_new, cache_k, cache_v, page_pre, page_tot, pp=16):
    Bd = q.shape[0]
    n_pool = cache_k.shape[1]
    n_pages = pt_flat.shape[0] // Bd
    pp = math.gcd(pp, n_pages)
    tok = lambda w: pl.BlockSpec((1, N_HEADS, w), lambda b, g, pt: (b, 0, 0))
    hbm = pl.BlockSpec(memory_space=pl.ANY)
    grid_spec = pltpu.PrefetchScalarGridSpec(
        num_scalar_prefetch=1, grid=(Bd, n_pages // pp),
        in_specs=[tok(HEAD_DIM), tok(HEAD_DIM), tok(HEAD_DIM), tok(1), hbm, hbm, hbm, hbm],
        out_specs=pl.BlockSpec((1, N_HEADS, HEAD_DIM), lambda b, g, pt: (b, 0, 0)),
        scratch_shapes=[pltpu.VMEM((2, pp * PAGE, N_HEADS, HEAD_DIM), F32),
                        pltpu.VMEM((2, pp * PAGE, N_HEADS, HEAD_DIM), F32),
                        pltpu.VMEM((2, pp, PAGE * N_HEADS), F32), pltpu.VMEM((2, pp, PAGE * N_HEADS), F32),
                        pltpu.SemaphoreType.DMA((4, 2)),
                        pltpu.VMEM((N_HEADS, 1), F32), pltpu.VMEM((N_HEADS, 1), F32),
                        pltpu.VMEM((N_HEADS, HEAD_DIM), F32), pltpu.VMEM((1, PAGE * N_HEADS), F32)])
    return pl.pallas_call(
        functools.partial(_fox_sample_body, layer=layer, n_pool=n_pool, n_pages=n_pages, pp=pp,
                          scale=HEAD_DIM ** -0.5),
        name="fox_sample", grid_spec=grid_spec,
        out_shape=jax.ShapeDtypeStruct((Bd, N_HEADS, HEAD_DIM), F32),
        compiler_params=_params("arbitrary", "arbitrary"),
    )(pt_flat, q, k_new, v_new, logf_new, cache_k, cache_v, page_pre, page_tot)


def _idx_scores(qi, w, ki, keys_on_lanes=False):
    contract = (((1,), (0,)), ((), ())) if keys_on_lanes else (((1,), (1,)), ((), ()))
    d = lax.dot_general(qi, ki.astype(BF16), contract, preferred_element_type=F32)
    return jnp.sum(w * (H_IDX ** -0.5 * D_IDX ** -0.5) * jnp.maximum(d, 0.0), axis=0, keepdims=True)


def _idx_scores_body(qi_ref, w_ref, ki_ref, o_ref):
    o_ref[0] = _idx_scores(qi_ref[0], w_ref[0], ki_ref[...])


def _idx_scores_paged_body(pt_ref, qi_ref, w_ref, ki_hbm, o_ref, buf, sem, *, layer, n_pages, chunk):
    b = pl.program_id(0)

    def page_copy(p):
        return pltpu.make_async_copy(ki_hbm.at[layer, pt_ref[b * n_pages + p]],
                                     buf.at[:, pl.ds(pl.multiple_of(p * PAGE, PAGE), PAGE)], sem.at[0])

    def start(p, c):
        page_copy(p).start()
        return c

    def wait(p, c):
        page_copy(p).wait()
        return c

    lax.fori_loop(0, n_pages, start, 0)
    lax.fori_loop(0, n_pages, wait, 0)
    for c in range(n_pages * PAGE // chunk):
        o_ref[0, :, c * chunk:(c + 1) * chunk] = _idx_scores(qi_ref[0], w_ref[0], buf[:, c * chunk:(c + 1) * chunk],
                                                             keys_on_lanes=True)


def _idx_scores_paged(layer, pt_flat, qi, w, cache_ki_t, chunk=2048):
    Bd = qi.shape[0]
    n_pages = pt_flat.shape[0] // Bd
    chunk = math.gcd(chunk, n_pages * PAGE)
    grid_spec = pltpu.PrefetchScalarGridSpec(
        num_scalar_prefetch=1, grid=(Bd,),
        in_specs=[pl.BlockSpec((1, H_IDX, D_IDX), lambda b, pt: (b, 0, 0)),
                  pl.BlockSpec((1, H_IDX, 1), lambda b, pt: (b, 0, 0)),
                  pl.BlockSpec(memory_space=pl.ANY)],
        out_specs=pl.BlockSpec((1, 1, n_pages * PAGE), lambda b, pt: (b, 0, 0)),
        scratch_shapes=[pltpu.VMEM((D_IDX, n_pages * PAGE), F32), pltpu.SemaphoreType.DMA((1,))])
    return pl.pallas_call(
        functools.partial(_idx_scores_paged_body, layer=layer, n_pages=n_pages, chunk=chunk),
        name="dsa_sample_scores", grid_spec=grid_spec,
        out_shape=jax.ShapeDtypeStruct((Bd, 1, n_pages * PAGE), F32),
        compiler_params=_params("arbitrary"),
    )(pt_flat, qi, w, cache_ki_t)


def _idx_scores_new(qi, w, ki_rows):
    Bd, R, _ = ki_rows.shape
    return pl.pallas_call(
        _idx_scores_body, grid=(Bd,),
        in_specs=[pl.BlockSpec((1, H_IDX, D_IDX), lambda b: (b, 0, 0)),
                  pl.BlockSpec((1, H_IDX, 1), lambda b: (b, 0, 0)),
                  pl.BlockSpec((None, R, D_IDX), lambda b: (b, 0, 0))],
        out_specs=pl.BlockSpec((1, 1, R), lambda b: (b, 0, 0)),
        out_shape=jax.ShapeDtypeStruct((Bd, 1, R), F32),
        compiler_params=_params("parallel"),
    )(qi, w, ki_rows)


def _dsa_sample_select_body(sc_ref, scn_ref, rel_ref, idx_ref, sb_ref, nb_ref, rs_sc, *, topk, chunk):
    Bd, P = sc_ref.shape
    key_p = _sortable_key(sc_ref[...])
    key_n = _sortable_key(scn_ref[:, 0:1])
    col = lax.broadcasted_iota(I32, (Bd, P), 1)

    one = lambda pred: jnp.where(pred, 1, 0)

    def count(ind_p, ind_n):
        return jnp.sum(ind_p, axis=1, keepdims=True) + ind_n

    thr = _kth_largest_key(lambda t: count(one(key_p >= t), one(key_n >= t)), (Bd, 1), topk)
    n_gt = count(one(key_p > thr), one(key_n > thr))
    n_ge = count(one(key_p >= thr), one(key_n >= thr))
    need = topk - n_gt
    nbits = (P + 1).bit_length()

    def body(b, cut):
        cand = cut | lax.shift_left(jnp.int32(1), nbits - 1 - b)
        cnt = count(jnp.where(key_p == thr, one(col < cand), 0), jnp.where(key_n == thr, one(P < cand), 0))
        return jnp.where(cnt < need, cand, cut)
    cut = lax.fori_loop(0, nbits, body, jnp.zeros((Bd, 1), I32))
    cut = jnp.where(n_ge > topk, cut, jnp.int32(2 ** 30))
    sel_p = jnp.where(key_p > thr, 1, jnp.where(key_p == thr, one(col <= cut), 0))
    sel_n = jnp.where(key_n > thr, 1, jnp.where(key_n == thr, one(P <= cut), 0))
    rank = sel_p
    sh = 1
    while sh < P:
        rank = rank + jnp.where(col >= sh, pltpu.roll(rank, sh, axis=1), 0)
        sh *= 2
    rs_sc[...] = sel_p * rank
    n_past = jnp.sum(sel_p, axis=1, keepdims=True)
    nb_ref[...] = jnp.where(sel_n > 0, rel_ref[0:1, :], NEG)

    slot = lax.broadcasted_iota(I32, (topk, 1), 0)
    ccol = lax.broadcasted_iota(I32, (topk, chunk), 1)
    for b in range(Bd):
        idx = jnp.zeros((topk, 1), I32)
        for c in range(P // chunk):
            rs = rs_sc[b:b + 1, c * chunk:(c + 1) * chunk]
            idx = idx + jnp.sum(jnp.where(rs == slot + 1, ccol + c * chunk, 0), axis=1, keepdims=True)
        idx_ref[b] = idx
        bucket = _t5_bucket(P - idx)
        bias = jnp.zeros((topk, N_HEADS), F32)
        for k in range(N_BUCKETS):
            bias = bias + jnp.where(bucket == k, rel_ref[k:k + 1, :], 0.0)
        sb_ref[b] = jnp.where(slot < n_past[b:b + 1, :], bias, NEG)


def _dsa_sample_select(sc, sc_new, rel_table, topk, chunk=2048):
    Bd, P = sc.shape
    return pl.pallas_call(
        functools.partial(_dsa_sample_select_body, topk=topk, chunk=min(chunk, P)),
        out_shape=[jax.ShapeDtypeStruct((Bd, topk, 1), I32), jax.ShapeDtypeStruct((Bd, topk, N_HEADS), F32),
                   jax.ShapeDtypeStruct((Bd, N_HEADS), F32)],
        scratch_shapes=[pltpu.VMEM((Bd, P), I32)],
        compiler_params=pltpu.CompilerParams(vmem_limit_bytes=V7X_VMEM_LIMIT_BYTES),
    )(sc, sc_new, rel_table)


def _dsa_sample_attend_body(idx_ref, pt_ref, q_ref, kn_ref, vn_ref, sb_ref, nb_ref, kc_hbm, vc_hbm, o_ref,
                            kbuf, vbuf, sems, *, layer, topk, n_pages, scale):
    b = pl.program_id(0)

    def row_copies(j):
        i = idx_ref[b * topk + j]
        pg = pt_ref[b * n_pages + lax.shift_right_logical(i, 7)]
        off = i & (PAGE - 1)
        return (pltpu.make_async_copy(kc_hbm.at[layer, pg, off], kbuf.at[j], sems.at[0]),
                pltpu.make_async_copy(vc_hbm.at[layer, pg, off], vbuf.at[j], sems.at[1]))

    def start(j, c):
        ck, cv = row_copies(j)
        ck.start()
        cv.start()
        return c

    def wait(j, c):
        ck, cv = row_copies(j)
        ck.wait()
        cv.wait()
        return c

    lax.fori_loop(0, topk, start, 0)
    lax.fori_loop(0, topk, wait, 0)

    n_lanes = topk * N_HEADS
    hsel, _ = _head_lane_select(n_lanes)
    q = q_ref[0]
    kb = kbuf[...].reshape(n_lanes, HEAD_DIM).astype(BF16)
    s = lax.dot_general(q, kb, (((1,), (1,)), ((), ())), preferred_element_type=F32) * scale
    s = jnp.where(hsel, s + sb_ref[0], NEG)
    kn = kn_ref[0].astype(BF16).astype(F32)
    vn = vn_ref[0].astype(BF16).astype(F32)
    s_n = jnp.sum(q.astype(F32) * kn, axis=1, keepdims=True) * scale + nb_ref[0]
    m = jnp.maximum(jnp.max(s, axis=1, keepdims=True), s_n)
    p = jnp.exp(s - m)
    p_n = jnp.exp(s_n - m)
    l = jnp.sum(p, axis=1, keepdims=True) + p_n
    vb = vbuf[...].reshape(n_lanes, HEAD_DIM).astype(BF16)
    acc = jnp.dot(p.astype(BF16), vb, preferred_element_type=F32) + p_n.astype(BF16).astype(F32) * vn
    o_ref[0] = acc / l


def _dsa_sample_attend(layer, idx_flat, pt_flat, q, k_new, v_new, slot_bias, new_bias, cache_k, cache_v, topk):
    Bd = q.shape[0]
    n_pages = pt_flat.shape[0] // Bd
    tok = lambda w: pl.BlockSpec((1, N_HEADS, w), lambda b, idx, pt: (b, 0, 0))
    grid_spec = pltpu.PrefetchScalarGridSpec(
        num_scalar_prefetch=2, grid=(Bd,),
        in_specs=[tok(HEAD_DIM), tok(HEAD_DIM), tok(HEAD_DIM),
                  pl.BlockSpec((1, 1, topk * N_HEADS), lambda b, idx, pt: (b, 0, 0)),
                  tok(1),
                  pl.BlockSpec(memory_space=pl.ANY), pl.BlockSpec(memory_space=pl.ANY)],
        out_specs=pl.BlockSpec((1, N_HEADS, HEAD_DIM), lambda b, idx, pt: (b, 0, 0)),
        scratch_shapes=[pltpu.VMEM((topk, N_HEADS, HEAD_DIM), F32), pltpu.VMEM((topk, N_HEADS, HEAD_DIM), F32),
                        pltpu.SemaphoreType.DMA((2,))])
    return pl.pallas_call(
        functools.partial(_dsa_sample_attend_body, layer=layer, topk=topk, n_pages=n_pages,
                          scale=HEAD_DIM ** -0.5),
        grid_spec=grid_spec,
        out_shape=jax.ShapeDtypeStruct((Bd, N_HEADS, HEAD_DIM), F32),
        compiler_params=_params("arbitrary"),
    )(idx_flat, pt_flat, q, k_new, v_new, slot_bias, new_bias, cache_k, cache_v)


W_ATT = N_HEADS * HEAD_DIM
_PROJ_NAMES = ("fq", "fk", "fv", "ff", "bq", "bk", "bv", "iq", "ik", "iw", "su", "gl")
_FF_LANES = slice(0, N_HEADS)
_IW_LANES = slice(N_HEADS, N_HEADS + H_IDX)
_IK_LANES = slice(N_HEADS + H_IDX, N_HEADS + H_IDX + D_IDX)


def _stacked_weights(p):
    D = p["w_in"].shape[1]
    w_c = p["w_glu"].shape[1]
    sizes = (W_ATT, W_ATT, W_ATT, N_HEADS, W_ATT, W_ATT, W_ATT, H_IDX * D_IDX, D_IDX, H_IDX, w_c, 3 * D)
    start, off = {}, 0
    for name, n in zip(_PROJ_NAMES, sizes):
        start[name] = (off, n)
        off += n
    w_in = p["w_in"]
    grab = lambda names: w_in[:, :, start[names[0]][0]:start[names[-1]][0] + start[names[-1]][1]]
    pad = jnp.zeros(w_in.shape[:2] + (LANES - (N_HEADS + H_IDX + D_IDX),), w_in.dtype)
    runs = (("fq", "fk", "fv"), ("bq", "bk", "bv", "iq"), ("su", "gl"))
    sw = {"proj": {}}
    for names in runs:
        arr = grab(names).astype(BF16)
        for n in names:
            sw["proj"][n] = (arr, (start[n][0] - start[names[0]][0], start[n][1]))
    small = jnp.concatenate([grab(("ff",)), grab(("iw",)), grab(("ik",)), pad], axis=2).astype(BF16)
    sw["proj"]["small"] = (small, (0, LANES))
    sw["u"] = p["peer_u"].astype(BF16)
    sw["v_t"] = jnp.swapaxes(p["peer_v"], 1, 2).astype(BF16)
    return sw


def _layer_weights(l, p, sw):
    w_c = p["w_glu"].shape[1]
    lw = {"layer": l, "proj": sw["proj"], "u": sw["u"], "v_t": sw["v_t"]}
    w_glu, w_br = p["w_glu"][l], p["w_br"][l]
    lw["glu_a"], lw["glu_b"] = w_glu[:, :w_c].astype(BF16), w_glu[:, w_c:].astype(BF16)
    lw["wa"] = w_br[:W_ATT].astype(BF16)
    lw["wb"] = w_br[W_ATT:2 * W_ATT].astype(BF16)
    lw["wc"] = w_br[2 * W_ATT:].astype(BF16)
    lw["w_out"] = p["w_out"][l].astype(BF16)
    lw["wq"] = p["peer_wq"][l].astype(BF16)
    lw["keys"] = p["peer_keys"][l].reshape(2 * P_HEADS, N_KEYS, -1).astype(BF16)
    lw["s5"] = _s5_prepare(p["s5_lam_re"][l], p["s5_lam_im"][l], p["s5_log_dt"][l], p["s5_b_re"][l],
                           p["s5_b_im"][l], p["s5_c_re"][l], p["s5_c_im"][l])
    lw["s5_d"] = p["s5_d"][l]
    lw["norm1_g"], lw["norm2_g"] = p["norm1_g"][l], p["norm2_g"][l]
    return lw


def _layer(x, mods, rows_per_group, T, lw, s5_re0, s5_im0, attend):
    M, D = x.shape
    sh1, sc1, gt1, sh2, sc2, gt2 = mods
    tm = min(512, M)
    h = _norm_mod(x, lw["norm1_g"], sc1, sh1, rows_per_group)
    proj = lambda name, dts: _mm(h, lw["proj"][name][0], dts, tm=tm, tn=1024, order="nm", name="proj_" + name,
                                 cols=lw["proj"][name][1], layer=lw["layer"])
    (fq,) = proj("fq", [BF16])
    fk, fk16 = proj("fk", [F32, BF16])
    fv, fv16 = proj("fv", [F32, BF16])
    (bq,) = proj("bq", [BF16])
    bk, bk16 = proj("bk", [F32, BF16])
    bv, bv16 = proj("bv", [F32, BF16])
    (iq,) = proj("iq", [BF16])
    (su,) = proj("su", [F32])
    (small,) = proj("small", [F32])
    (gl,) = proj("gl", [F32])
    o_a, o_b, logf = attend(fq, fk, fk16, fv, fv16, bq, bk, bk16, bv, bv16, iq, small)
    yg, s5_re, s5_im = _s5(su, T, s5_re0, s5_im0, lw["s5"], lw["s5_d"])
    o_c = _glu(yg, lw["glu_a"], lw["glu_b"])
    merged = _merge(o_a, o_b, o_c, lw["wa"], lw["wb"], lw["wc"], gl)
    tn = 1024
    (x1,) = _mm(merged, lw["w_out"], [F32], tm=tm, tn=tn,
                epilogue=lambda acc, x_, g_: (x_ + g_ * acc,),
                extras=[(x, (tm, tn), lambda i, j: (i, j)), _mod_extra(gt1, rows_per_group, tm, tn)],
                name="out_proj")
    h2 = _norm_mod(x1, lw["norm2_g"], sc2, sh2, rows_per_group)
    if M < LANES:
        padr = lambda a: jnp.pad(a, ((0, LANES - M), (0, 0)))
        gt2p = jnp.pad(gt2, ((0, 0), (0, LANES - M), (0, 0)))
        x2 = _peer(padr(h2), padr(x1), gt2p, LANES, lw["wq"], lw["keys"], lw["u"], lw["v_t"], lw["layer"])[:M]
    else:
        x2 = _peer(h2, x1, gt2, rows_per_group, lw["wq"], lw["keys"], lw["u"], lw["v_t"], lw["layer"])
    return x2, (fk, fv, logf, bk, bv, small[:, _IK_LANES], s5_re, s5_im)


def kernel(x_prompt, x_sample, cache_fox_k, cache_fox_v, cache_fox_logf, cache_dsa_k, cache_dsa_v,
           cache_dsa_idx_k, state_s5_re, state_s5_im, page_table, c_prompt, c_sample,
           w_ada, b_ada, norm1_g, norm2_g, w_in, b_f, rel_table, s5_lam_re, s5_lam_im, s5_log_dt,
           s5_b_re, s5_b_im, s5_c_re, s5_c_im, s5_d, w_glu, w_br, w_out,
           peer_wq, peer_keys, peer_u, peer_v, final_norm_g):
    p = dict(w_in=w_in, w_glu=w_glu, w_br=w_br, w_out=w_out, peer_wq=peer_wq, peer_keys=peer_keys,
             peer_u=peer_u, peer_v=peer_v, s5_lam_re=s5_lam_re, s5_lam_im=s5_lam_im, s5_log_dt=s5_log_dt,
             s5_b_re=s5_b_re, s5_b_im=s5_b_im, s5_c_re=s5_c_re, s5_c_im=s5_c_im, s5_d=s5_d,
             norm1_g=norm1_g, norm2_g=norm2_g)
    depth = w_in.shape[0]
    B, T, D = x_prompt.shape
    Bd = x_sample.shape[0]
    assert x_sample.shape[1] == 1
    n_pool = cache_fox_k.shape[1]
    n_pages = page_table.shape[1]
    past = n_pages * PAGE
    tile = min(512, T)
    topk_p = min(TOPK_MAX, T // 4)
    topk_s = min(TOPK_MAX, (past + 1) // 4)
    pt_flat = page_table.reshape(-1).astype(I32)
    page_pre, page_tot = _page_prefix(cache_fox_logf.reshape(depth * n_pool, PAGE * N_HEADS).astype(F32))
    cache_ki_t = jnp.swapaxes(cache_dsa_idx_k, 2, 3)
    sw = _stacked_weights(p)
    rel = rel_table.astype(F32)
    tz = _t5_tiles(rel)

    n_c = B + Bd
    c_all = jnp.pad(jnp.concatenate([c_prompt, c_sample], axis=0), ((0, (-n_c) % 8), (0, 0)))
    xp = x_prompt.reshape(B * T, D)
    xs = x_sample.reshape(Bd, D)
    rows_p, rows_s = [], []
    for l in range(depth):
        lw = _layer_weights(l, p, sw)
        m = _adaln(c_all, w_ada, b_ada, l)
        mods_p = [a[:B, None, :] for a in jnp.split(m, 6, axis=1)]
        mods_s = [a[None, B:n_c, :] for a in jnp.split(m, 6, axis=1)]
        b_f_l = b_f[l].astype(F32)

        def attend_prompt(fq, fk, fk16, fv, fv16, bq, bk, bk16, bv, bv16, iq, small):
            r3 = lambda a: a.reshape(B, T, a.shape[-1])
            small3 = r3(small)
            f_t = jnp.moveaxis(small3[:, :, _FF_LANES], -1, 1).reshape(B * N_HEADS, T)
            logf_t, cum = _gate_cumsum(f_t, jnp.tile(b_f_l, B).reshape(B * N_HEADS, 1))
            logf = jnp.moveaxis(logf_t.reshape(B, N_HEADS, T), 1, -1)
            o_a = _flash("fox", r3(fq), r3(fk16), r3(fv16), cum.reshape(B, N_HEADS, T) * LOG2E, tile=tile)
            ki_t = jnp.swapaxes(small3[:, :, _IK_LANES], 1, 2).astype(BF16)
            mask = _dsa_select(r3(iq), small3, ki_t, topk_p)
            o_b = _flash("dsa", r3(bq), r3(bk16), r3(bv16), mask, tz, tile=tile)
            return o_a.reshape(B * T, W_ATT), o_b.reshape(B * T, W_ATT), logf

        def attend_sample(fq, fk, fk16, fv, fv16, bq, bk, bk16, bv, bv16, iq, small, l=l):
            h3 = lambda a: a.reshape(Bd, N_HEADS, HEAD_DIM)
            logf = _logsig(small[:, _FF_LANES], b_f_l.reshape(1, N_HEADS))
            o_a = _fox_sample(l, pt_flat, h3(fq), h3(fk), h3(fv), logf.reshape(Bd, N_HEADS, 1),
                              cache_fox_k, cache_fox_v, page_pre, page_tot)
            qi3 = iq.reshape(Bd, H_IDX, D_IDX)
            w3 = small[:, _IW_LANES].reshape(Bd, H_IDX, 1)
            sc = _idx_scores_paged(l, pt_flat, qi3, w3, cache_ki_t)
            ki_new = jnp.pad(small[:, _IK_LANES].reshape(Bd, 1, D_IDX), ((0, 0), (0, LANES - 1), (0, 0)))
            sc_new = _idx_scores_new(qi3, w3, ki_new)
            idx, sb, nb = _dsa_sample_select(sc.reshape(Bd, past), sc_new.reshape(Bd, LANES), rel, topk_s)
            o_b = _dsa_sample_attend(l, idx.reshape(-1), pt_flat, h3(bq), h3(bk), h3(bv),
                                     sb.reshape(Bd, 1, topk_s * N_HEADS), nb.reshape(Bd, N_HEADS, 1),
                                     cache_dsa_k, cache_dsa_v, topk_s)
            return (o_a.reshape(Bd, W_ATT).astype(BF16), o_b.reshape(Bd, W_ATT).astype(BF16), logf)

        zero_state = jnp.zeros((B,) + state_s5_re.shape[2:], F32)
        xp, rp = _layer(xp, mods_p, T, T, lw, zero_state, zero_state, attend_prompt)
        xs, rs = _layer(xs, mods_s, Bd, 1, lw, state_s5_re[l], state_s5_im[l], attend_sample)
        rows_p.append(rp)
        rows_s.append(rs)

    y_prompt = _norm(xp, final_norm_g).reshape(B, T, D)
    y_sample = _norm(xs, final_norm_g).reshape(Bd, 1, D)

    def leaves(rows, nb, nt):
        fk, fv, fl, bk, bv, ik, sr, si = [jnp.stack(a) for a in zip(*rows)]
        hd = (depth, nb, nt, N_HEADS, HEAD_DIM)
        return (fk.reshape(hd), fv.reshape(hd), fl.reshape(depth, nb, nt, N_HEADS), bk.reshape(hd),
                bv.reshape(hd), ik.reshape(depth, nb, nt, D_IDX), sr, si)

    return (y_prompt, y_sample) + leaves(rows_p, B, T) + leaves(rows_s, Bd, 1)
```

```python
import functools
import math

import jax
import jax.numpy as jnp
from jax import lax
from jax.experimental import pallas as pl
from jax.experimental.pallas import tpu as pltpu

F32 = jnp.float32
BF16 = jnp.bfloat16
I32 = jnp.int32

N_HEADS = 8
HEAD_DIM = 128
H_IDX = 16
D_IDX = 64
TOPK_MAX = 256
GROUP = 16
N_STATE = 64
N_BUCKETS = 32
MAX_DIST = 128
N_KEYS = 128
P_HEADS = 8
P_TOPK = 16
PAGE = 128
EPS = 1e-6

V7X_VMEM_LIMIT_BYTES = 56 * 1024 * 1024
LANES = 128
NEG = -1e30
LOG2E = math.log2(math.e)
INT_MIN = -2 ** 31

_PAIRS = sorted([(i, j) for i in range(P_TOPK) for j in range(P_TOPK) if (i + 1) * (j + 1) <= P_TOPK],
                key=lambda p: p[0] * P_TOPK + p[1])
_N_CAND = 56


def _params(*sem):
    return pltpu.CompilerParams(dimension_semantics=sem, vmem_limit_bytes=V7X_VMEM_LIMIT_BYTES)


def _gelu(x):
    return 0.5 * x * (1.0 + jnp.tanh(math.sqrt(2.0 / math.pi) * (x + 0.044715 * (x * x * x))))


def _sigmoid(x):
    return 1.0 / (1.0 + jnp.exp(-x))


def _mm_body(*refs, n_extra, epilogue):
    a_ref, b_ref = refs[0], refs[1]
    extra = refs[2:2 + n_extra]
    outs = refs[2 + n_extra:]
    acc = jnp.dot(a_ref[...].astype(BF16), b_ref[...].astype(BF16), preferred_element_type=F32)
    vals = epilogue(acc, *[e[...] for e in extra]) if epilogue is not None else (acc,) * len(outs)
    for o, v in zip(outs, vals):
        o[...] = v.astype(o.dtype)


def _mm(a, b, out_dtypes, *, tm, tn, order="mn", epilogue=None, extras=(), name="mm", cols=None, layer=None):
    M, K = a.shape
    col0, N = cols if cols is not None else (0, b.shape[-1])
    tm, tn = min(tm, M), min(tn, N)
    assert M % tm == 0 and N % tn == 0 and col0 % tn == 0, (M, N, tm, tn, col0)
    jb0 = col0 // tn
    if order == "mn":
        grid = (M // tm, N // tn)
        ij = lambda g0, g1: (g0, g1)
    else:
        grid = (N // tn, M // tm)
        ij = lambda g0, g1: (g1, g0)
    if layer is None:
        b_spec = pl.BlockSpec((K, tn), lambda g0, g1: (0, jb0 + ij(g0, g1)[1]))
    else:
        b_spec = pl.BlockSpec((None, K, tn), lambda g0, g1: (layer, 0, jb0 + ij(g0, g1)[1]))
    in_specs = [pl.BlockSpec((tm, K), lambda g0, g1: (ij(g0, g1)[0], 0)), b_spec]
    args = [a, b]
    for arr, bshape, imap in extras:
        in_specs.append(pl.BlockSpec(bshape, lambda g0, g1, imap=imap: imap(*ij(g0, g1))))
        args.append(arr)
    out_specs = [pl.BlockSpec((tm, tn), lambda g0, g1: ij(g0, g1)) for _ in out_dtypes]
    out_shape = [jax.ShapeDtypeStruct((M, N), dt) for dt in out_dtypes]
    return pl.pallas_call(
        functools.partial(_mm_body, n_extra=len(extras), epilogue=epilogue),
        name=name, grid=grid, in_specs=in_specs, out_specs=out_specs, out_shape=out_shape,
        compiler_params=_params("parallel", "parallel"),
    )(*args)


def _mod_extra(mod, rows_per_group, tm, tn):
    r = mod.shape[1]
    return (mod, (None, r, tn), lambda i, j: ((i * tm) // rows_per_group, 0, j))


def _adaln_body(c_ref, w_ref, b_ref, o_ref):
    c = c_ref[...]
    s = c * _sigmoid(c)
    o_ref[...] = jnp.dot(s.astype(BF16), w_ref[...].astype(BF16), preferred_element_type=F32) + b_ref[...]


def _adaln(c, w_ada, b_ada, layer, tn=1024):
    R, D = c.shape
    depth, _, N = w_ada.shape
    return pl.pallas_call(
        _adaln_body, name="adaln", grid=(N // tn,),
        in_specs=[pl.BlockSpec((R, D), lambda j: (0, 0)),
                  pl.BlockSpec((None, D, tn), lambda j: (layer, 0, j)),
                  pl.BlockSpec((None, 1, tn), lambda j: (layer, 0, j))],
        out_specs=pl.BlockSpec((R, tn), lambda j: (0, j)),
        out_shape=jax.ShapeDtypeStruct((R, N), F32),
        compiler_params=_params("parallel"),
    )(c, w_ada, b_ada.reshape(depth, 1, N))


def _norm_mod_body(x_ref, g_ref, sc_ref, sh_ref, o_ref):
    x = x_ref[...]
    y = x * lax.rsqrt(jnp.mean(x * x, axis=-1, keepdims=True) + EPS) * g_ref[...]
    o_ref[...] = (y * (1.0 + sc_ref[...]) + sh_ref[...]).astype(o_ref.dtype)


def _norm_mod(x, g, sc, sh, rows_per_group, tm=512):
    M, D = x.shape
    tm = min(tm, M)
    r = sc.shape[1]
    mod_spec = pl.BlockSpec((None, r, D), lambda i: ((i * tm) // rows_per_group, 0, 0))
    return pl.pallas_call(
        _norm_mod_body, name="norm_mod", grid=(M // tm,),
        in_specs=[pl.BlockSpec((tm, D), lambda i: (i, 0)), pl.BlockSpec((1, D), lambda i: (0, 0)),
                  mod_spec, mod_spec],
        out_specs=pl.BlockSpec((tm, D), lambda i: (i, 0)),
        out_shape=jax.ShapeDtypeStruct((M, D), BF16),
        compiler_params=_params("parallel"),
    )(x, g.reshape(1, D), sc, sh)


def _norm_body(x_ref, g_ref, o_ref):
    x = x_ref[...]
    o_ref[...] = x * lax.rsqrt(jnp.mean(x * x, axis=-1, keepdims=True) + EPS) * g_ref[...]


def _norm(x, g, tm=512):
    M, D = x.shape
    tm = min(tm, M)
    return pl.pallas_call(
        _norm_body, name="final_norm", grid=(M // tm,),
        in_specs=[pl.BlockSpec((tm, D), lambda i: (i, 0)), pl.BlockSpec((1, D), lambda i: (0, 0))],
        out_specs=pl.BlockSpec((tm, D), lambda i: (i, 0)),
        out_shape=jax.ShapeDtypeStruct((M, D), F32),
        compiler_params=_params("parallel"),
    )(x, g.reshape(1, D))


def _log_sigmoid(z):
    return jnp.minimum(z, 0.0) - jnp.log(1.0 + jnp.exp(-jnp.abs(z)))


def _gate_body(f_ref, b_ref, logf_ref, cum_ref):
    logf = _log_sigmoid(f_ref[...] + b_ref[...])
    logf_ref[...] = logf
    T = logf.shape[1]
    lane = lax.broadcasted_iota(I32, logf.shape, 1)
    x = logf
    sh = 1
    while sh < T:
        x = x + jnp.where(lane >= sh, pltpu.roll(x, sh, axis=1), 0.0)
        sh *= 2
    cum_ref[...] = x


def _gate_cumsum(f_t, b_col):
    R, T = f_t.shape
    return pl.pallas_call(
        _gate_body, name="fox_gate",
        out_shape=[jax.ShapeDtypeStruct((R, T), F32), jax.ShapeDtypeStruct((R, T), F32)],
        compiler_params=pltpu.CompilerParams(vmem_limit_bytes=V7X_VMEM_LIMIT_BYTES),
    )(f_t, b_col)


def _flash_body(*refs, mode, tile, scale):
    if mode == "fox":
        q_ref, k_ref, v_ref, fk_ref, o_ref, m_sc, acc_sc = refs
    else:
        q_ref, k_ref, v_ref, mask_ref, tz_ref, o_ref, m_sc, acc_sc = refs
    qi = pl.program_id(1)
    ki = pl.program_id(2)

    @pl.when(ki == 0)
    def _():
        m_sc[...] = jnp.full(m_sc.shape, -jnp.inf, F32)
        acc_sc[...] = jnp.zeros(acc_sc.shape, F32)

    nsub = tile // LANES

    def t5_bias(h, near):
        zero = jnp.zeros((LANES, LANES), F32)
        if near == "diag":
            pick = lambda a, b: tz_ref[h, 0] if a == b else (tz_ref[h, 1] if a == b + 1 else zero)
        else:
            pick = lambda a, b: tz_ref[h, 1] if (a == 0 and b == nsub - 1) else zero
        rows = [jnp.concatenate([pick(a, b) for b in range(nsub)], axis=1) if nsub > 1 else pick(a, 0)
                for a in range(nsub)]
        return jnp.concatenate(rows, axis=0) if nsub > 1 else rows[0]

    def step(near):
        if mode == "fox":
            if near == "diag":
                row = lax.broadcasted_iota(I32, (tile, tile), 0)
                col = lax.broadcasted_iota(I32, (tile, tile), 1)
                keep = row >= col
        else:
            shared = mask_ref[0]
        ones = jnp.ones((tile, HEAD_DIM), BF16)
        for h in range(N_HEADS):
            sl = slice(h * HEAD_DIM, (h + 1) * HEAD_DIM)
            s = lax.dot_general(q_ref[0, :, sl], k_ref[0, :, sl], (((1,), (1,)), ((), ())),
                                preferred_element_type=F32) * (scale * LOG2E)
            if mode == "fox":
                s = s - fk_ref[0, h:h + 1, :]
                if near == "diag":
                    s = jnp.where(keep, s, NEG)
            else:
                s = s + shared
                if near != "far":
                    s = s + t5_bias(h, near)
            m_prev = m_sc[h]
            m_new = jnp.maximum(m_prev, jnp.max(s, axis=1, keepdims=True))
            alpha = jnp.exp2(m_prev - m_new)
            p = jnp.exp2(s - jnp.concatenate([m_new] * nsub, axis=1))
            pv = jnp.dot(p.astype(BF16), jnp.concatenate([v_ref[0, :, sl], ones], axis=1),
                         preferred_element_type=F32)
            acc_sc[h] = jnp.concatenate([alpha] * (2 * HEAD_DIM // LANES), axis=1) * acc_sc[h] + pv
            m_sc[h] = m_new

    if mode == "fox":
        pl.when(ki < qi)(lambda: step("far"))
        pl.when(ki == qi)(lambda: step("diag"))
    else:
        pl.when(ki < qi - 1)(lambda: step("far"))
        pl.when(ki == qi - 1)(lambda: step("next"))
        pl.when(ki == qi)(lambda: step("diag"))

    @pl.when(ki == qi)
    def _():
        for h in range(N_HEADS):
            sl = slice(h * HEAD_DIM, (h + 1) * HEAD_DIM)
            o_ref[0, :, sl] = (acc_sc[h, :, :HEAD_DIM] / acc_sc[h, :, HEAD_DIM:]).astype(o_ref.dtype)


def _flash(mode, q, k, v, *side, tile):
    B, T, W = q.shape
    nt = T // tile
    qspec = pl.BlockSpec((1, tile, W), lambda b, qi, ki: (b, qi, 0))
    kspec = pl.BlockSpec((1, tile, W), lambda b, qi, ki: (b, jnp.minimum(ki, qi), 0))
    if mode == "fox":
        side_specs = [pl.BlockSpec((1, N_HEADS, tile), lambda b, qi, ki: (b, 0, jnp.minimum(ki, qi)))]
    else:
        side_specs = [pl.BlockSpec((1, tile, tile), lambda b, qi, ki: (b, qi, jnp.minimum(ki, qi))),
                      pl.BlockSpec((N_HEADS, 2, LANES, LANES), lambda b, qi, ki: (0, 0, 0, 0))]
    return pl.pallas_call(
        functools.partial(_flash_body, mode=mode, tile=tile, scale=HEAD_DIM ** -0.5),
        name="flash_" + mode, grid=(B, nt, nt),
        in_specs=[qspec, kspec, kspec] + side_specs,
        out_specs=pl.BlockSpec((1, tile, W), lambda b, qi, ki: (b, qi, 0)),
        out_shape=jax.ShapeDtypeStruct((B, T, W), BF16),
        scratch_shapes=[pltpu.VMEM((N_HEADS, tile, LANES), F32), pltpu.VMEM((N_HEADS, tile, 2 * HEAD_DIM), F32)],
        compiler_params=_params("parallel", "parallel", "arbitrary"),
    )(q, k, v, *side)


def _sortable_key(x):
    bits = pltpu.bitcast(x, I32)
    return jnp.where(bits < 0, bits ^ jnp.int32(0x7FFFFFFF), bits)


def _kth_largest_key(count_ge, shape, k):
    def body(b, cur):
        cand = cur | lax.shift_left(jnp.int32(1), 31 - b)
        cnt = count_ge(cand ^ jnp.int32(INT_MIN))
        return jnp.where(cnt >= k, cand, cur)
    cur = lax.fori_loop(0, 32, body, jnp.zeros(shape, I32))
    return cur ^ jnp.int32(INT_MIN)


def _dsa_select_body(qi_ref, small_ref, kit_ref, mask_ref, key_sc, cut_sc, wrep_sc, *, tq, cw, bw, T, topk):
    q0 = pl.program_id(1) * tq
    w = small_ref[0][:, N_HEADS:N_HEADS + H_IDX] * (H_IDX ** -0.5 * D_IDX ** -0.5)
    bucket = (q0 + tq + bw - 1) // bw
    n_valid = bucket * (bw // cw)
    row_c = q0 + lax.broadcasted_iota(I32, (tq, cw), 0)
    lane_c = lax.broadcasted_iota(I32, (tq, cw), 1)

    for h in range(H_IDX):
        wrep_sc[:, h * LANES:(h + 1) * LANES] = jnp.broadcast_to(w[:, h:h + 1], (tq, LANES))

    def score_chunk(c, carry):
        c0 = pl.multiple_of(c * cw, cw)
        kt = kit_ref[0, :, pl.ds(c0, cw)]
        acc = jnp.zeros((tq, cw), F32)
        for h in range(H_IDX):
            d = jnp.dot(qi_ref[0, :, h * D_IDX:(h + 1) * D_IDX], kt, preferred_element_type=F32)
            wh = jnp.concatenate([wrep_sc[:, h * LANES:(h + 1) * LANES]] * (cw // LANES), axis=1)
            acc = acc + wh * jnp.maximum(d, 0.0)
        key_sc[:, pl.ds(c0, cw)] = jnp.where(c0 + lane_c <= row_c, _sortable_key(acc), jnp.int32(INT_MIN))
        return carry

    lax.fori_loop(0, n_valid, score_chunk, 0)

    lane = lax.broadcasted_iota(I32, (tq, LANES), 1)
    row = q0 + lax.broadcasted_iota(I32, (tq, LANES), 0)
    one = lambda pred: jnp.where(pred, 1, 0)

    def select(n_lt):
        def count(ind):
            part = jnp.zeros((tq, LANES), I32)
            for c in range(n_lt):
                part = part + ind(key_sc[:, c * LANES:(c + 1) * LANES], c * LANES + lane)
            return jnp.sum(part, axis=1, keepdims=True)

        thr = _kth_largest_key(lambda t: count(lambda k, col: one(k >= t)), (tq, 1), topk)
        n_gt = count(lambda k, col: one(k > thr))
        n_ge = count(lambda k, col: one(k >= thr))
        excess = jnp.where(thr > jnp.int32(INT_MIN), n_ge - topk, 0)
        cut_sc[...] = jnp.full((tq, 1), T, I32)

        @pl.when(jnp.max(excess) > 0)
        def _():
            need = topk - n_gt
            nbits = max(1, (T - 1).bit_length())

            def body(b, cut):
                cand = cut | lax.shift_left(jnp.int32(1), nbits - 1 - b)
                cnt = count(lambda k, col: jnp.where(k == thr, one(col < cand), 0))
                return jnp.where(cnt < need, cand, cut)
            cut = lax.fori_loop(0, nbits, body, jnp.zeros((tq, 1), I32))
            cut_sc[...] = jnp.where(excess > 0, cut, T)

        cut = cut_sc[...]
        for c in range(n_lt):
            k = key_sc[:, c * LANES:(c + 1) * LANES]
            col = c * LANES + lane
            val = jnp.where(k > thr, 0.0, jnp.where(k == thr, jnp.where(col <= cut, 0.0, NEG), NEG))
            mask_ref[0, :, c * LANES:(c + 1) * LANES] = jnp.where(col <= row, val, NEG)
        if n_lt * LANES < T:
            mask_ref[0, :, n_lt * LANES:] = jnp.full((tq, T - n_lt * LANES), NEG, F32)

    for k in range(1, T // bw + 1):
        pl.when(bucket == k)(functools.partial(select, k * bw // LANES))


def _dsa_select(qi, small, ki_t, topk, tq=256, cw=256, bw=512):
    B, T, _ = qi.shape
    tq, cw, bw = min(tq, T), min(cw, T), min(bw, T)
    assert bw % cw == 0 and T % bw == 0
    return pl.pallas_call(
        functools.partial(_dsa_select_body, tq=tq, cw=cw, bw=bw, T=T, topk=topk),
        name="dsa_select", grid=(B, T // tq),
        in_specs=[pl.BlockSpec((1, tq, H_IDX * D_IDX), lambda b, i: (b, i, 0)),
                  pl.BlockSpec((1, tq, LANES), lambda b, i: (b, i, 0)),
                  pl.BlockSpec((1, D_IDX, T), lambda b, i: (b, 0, 0))],
        out_specs=pl.BlockSpec((1, tq, T), lambda b, i: (b, i, 0)),
        out_shape=jax.ShapeDtypeStruct((B, T, T), F32),
        scratch_shapes=[pltpu.VMEM((tq, T), I32), pltpu.VMEM((tq, 1), I32), pltpu.VMEM((tq, H_IDX * LANES), F32)],
        compiler_params=_params("parallel", "parallel"),
    )(qi, small, ki_t)


def _t5_bucket(dist):
    n = jnp.maximum(dist, 0)
    max_exact = N_BUCKETS // 2
    nf = jnp.maximum(n, 1).astype(F32)
    large = max_exact + (jnp.log(nf / max_exact) / math.log(MAX_DIST / max_exact)
                         * (N_BUCKETS - max_exact)).astype(I32)
    large = jnp.minimum(large, N_BUCKETS - 1)
    return jnp.where(n < max_exact, n, large)


def _t5_tiles(rel_table):
    assert LANES >= MAX_DIST
    i = jnp.arange(LANES)
    bucket = _t5_bucket((jnp.arange(2) * LANES)[:, None, None] + i[None, :, None] - i[None, None, :])
    rel = rel_table.astype(F32)
    tz = jnp.zeros((rel.shape[1],) + bucket.shape, F32)
    for k in range(N_BUCKETS - 1):
        tz = tz + jnp.where(bucket[None] == k, (rel[k] - rel[N_BUCKETS - 1])[:, None, None, None], 0.0)
    return tz * LOG2E


def _s5_disc_body(lr_ref, li_ref, ldt_ref, br_ref, bi_ref, abr_ref, abi_ref, bbr_ref, bbi_ref):
    lr, li = lr_ref[...], li_ref[...]
    dt = jnp.exp(ldt_ref[...])
    mag = jnp.exp(lr * dt)
    ab_re, ab_im = mag * jnp.cos(li * dt), mag * jnp.sin(li * dt)
    den = lr * lr + li * li
    nr = ab_re - 1.0
    k_re = (nr * lr + ab_im * li) / den
    k_im = (ab_im * lr - nr * li) / den
    br, bi = br_ref[...], bi_ref[...]
    abr_ref[...] = ab_re
    abi_ref[...] = ab_im
    bbr_ref[...] = k_re * br - k_im * bi
    bbi_ref[...] = k_re * bi + k_im * br


def _s5_discretise(lam_re, lam_im, log_dt, b_re, b_im):
    G, N, P = b_re.shape
    rep = lambda a: jnp.broadcast_to(a.reshape(G * N, 1), (G * N, P))
    ldt = jnp.broadcast_to(log_dt.reshape(G, 1, 1), (G, N, P)).reshape(G * N, P)
    shp = jax.ShapeDtypeStruct((G * N, P), F32)
    abr, abi, bbr, bbi = pl.pallas_call(
        _s5_disc_body, name="s5_discretise", out_shape=[shp] * 4,
        compiler_params=pltpu.CompilerParams(vmem_limit_bytes=V7X_VMEM_LIMIT_BYTES),
    )(rep(lam_re), rep(lam_im), ldt, b_re.reshape(G * N, P), b_im.reshape(G * N, P))
    return (abr[:, 0].reshape(G, N), abi[:, 0].reshape(G, N),
            bbr.reshape(G, N, P), bbi.reshape(G, N, P))


def _s5_body(u_ref, bc_ref, cc_ref, a_ref, d_ref, x0_ref, y_ref, xT_ref, xs_sc, carry_sc, *, tc, S, lc):
    c = pl.program_id(1)

    @pl.when(c == 0)
    def _():
        carry_sc[...] = x0_ref[0]

    n_ct = u_ref.shape[2] // LANES
    ks = S // n_ct
    ch = lambda t: slice(t * LANES, (t + 1) * LANES)
    st = lambda j: slice(j * ks, (j + 1) * ks)
    for j in range(2 * n_ct):
        xs_sc[:, st(j)] = jnp.dot(u_ref[0, :, ch(j % n_ct)].astype(BF16), bc_ref[j], preferred_element_type=F32)

    for j in range(S // lc):
        slr = slice(j * lc, (j + 1) * lc)
        sli = slice(S + j * lc, S + (j + 1) * lc)
        ar, ai = a_ref[:, slr], a_ref[:, sli]

        def step(t, carry, slr=slr, sli=sli, ar=ar, ai=ai):
            xr, xi = carry
            nr = ar * xr - ai * xi + xs_sc[pl.ds(t, 1), slr]
            ni = ar * xi + ai * xr + xs_sc[pl.ds(t, 1), sli]
            xs_sc[pl.ds(t, 1), slr] = nr
            xs_sc[pl.ds(t, 1), sli] = ni
            return nr, ni

        xr, xi = lax.fori_loop(0, tc, step, (carry_sc[:, slr], carry_sc[:, sli]), unroll=min(8, tc))
        carry_sc[:, slr] = xr
        carry_sc[:, sli] = xi

    for t in range(n_ct):
        y = (jnp.dot(xs_sc[:, st(t)].astype(BF16), cc_ref[t], preferred_element_type=F32)
             + jnp.dot(xs_sc[:, st(n_ct + t)].astype(BF16), cc_ref[n_ct + t], preferred_element_type=F32)
             + d_ref[:, ch(t)] * u_ref[0, :, ch(t)])
        y_ref[0, :, ch(t)] = _gelu(y).astype(y_ref.dtype)
    xT_ref[0] = carry_sc[...]


def _glu_body(y_ref, wa_ref, wb_ref, o_ref):
    y = y_ref[...]
    a = jnp.dot(y, wa_ref[...], preferred_element_type=F32)
    b = jnp.dot(y, wb_ref[...], preferred_element_type=F32)
    o_ref[...] = (a * _sigmoid(b)).astype(o_ref.dtype)


def _glu(y, wa, wb, tm=512):
    M, K = y.shape
    N = wa.shape[1]
    tm = min(tm, M)
    return pl.pallas_call(
        _glu_body, name="s5_glu", grid=(M // tm,),
        in_specs=[pl.BlockSpec((tm, K), lambda i: (i, 0)), pl.BlockSpec((K, N), lambda i: (0, 0)),
                  pl.BlockSpec((K, N), lambda i: (0, 0))],
        out_specs=pl.BlockSpec((tm, N), lambda i: (i, 0)),
        out_shape=jax.ShapeDtypeStruct((M, N), BF16),
        compiler_params=_params("parallel"),
    )(y, wa, wb)


def _merge_body(oa_ref, ob_ref, oc_ref, wa_ref, wb_ref, wc_ref, ga_ref, gb_ref, gc_ref, o_ref):
    dot = lambda x, w: jnp.dot(x[...], w[...], preferred_element_type=F32)
    m = (_sigmoid(ga_ref[...]) * dot(oa_ref, wa_ref) + _sigmoid(gb_ref[...]) * dot(ob_ref, wb_ref)
         + _sigmoid(gc_ref[...]) * dot(oc_ref, wc_ref))
    o_ref[...] = m.astype(o_ref.dtype)


def _merge(o_a, o_b, o_c, wa, wb, wc, gl, tm=512, tn=1024):
    M, K = o_a.shape
    D = wa.shape[1]
    tm = min(tm, M)
    nd = D // tn
    ospec = pl.BlockSpec((tm, K), lambda i, j: (i, 0))
    wspec = pl.BlockSpec((K, tn), lambda i, j: (0, j))
    gspec = lambda g: pl.BlockSpec((tm, tn), lambda i, j, g=g: (i, g * nd + j))
    return pl.pallas_call(
        _merge_body, name="branch_merge", grid=(M // tm, nd),
        in_specs=[ospec, ospec, ospec, wspec, wspec, wspec, gspec(0), gspec(1), gspec(2)],
        out_specs=pl.BlockSpec((tm, tn), lambda i, j: (i, j)),
        out_shape=jax.ShapeDtypeStruct((M, D), BF16),
        compiler_params=_params("parallel", "parallel"),
    )(o_a, o_b, o_c, wa, wb, wc, gl, gl, gl)


def _bf16_pair(x):
    hi = pltpu.bitcast(x.astype(BF16).astype(F32), I32)
    return hi | lax.shift_right_logical(hi, 16)


def _top16_rows(s, n_rows):
    iota = lax.broadcasted_iota(I32, s.shape, 0)
    rank = jnp.full(s.shape, P_TOPK, I32)
    vals = []
    work = s
    for k in range(P_TOPK):
        m = jnp.max(work, axis=0, keepdims=True)
        idx = jnp.min(jnp.where(work == m, iota, n_rows), axis=0, keepdims=True)
        hit = iota == idx
        rank = jnp.where(hit, k, rank)
        work = jnp.where(hit, -jnp.inf, work)
        vals.append(m)
    return rank, vals


def _peer_select_body(q_ref, keys_ref, a1_ref, n1_ref, a2_ref, r2_ref, cand_sc, sel_sc):
    tt = q_ref.shape[0]
    nt = (((1,), (1,)), ((), ()))
    for h in range(P_HEADS):
        s1 = lax.dot_general(keys_ref[2 * h], q_ref[:, (2 * h) * N_KEYS:(2 * h + 1) * N_KEYS], nt,
                             preferred_element_type=F32)
        s2 = lax.dot_general(keys_ref[2 * h + 1], q_ref[:, (2 * h + 1) * N_KEYS:(2 * h + 2) * N_KEYS], nt,
                             preferred_element_type=F32)
        r1, v1 = _top16_rows(s1, N_KEYS)
        r2, v2 = _top16_rows(s2, N_KEYS)
        for r, (i, j) in enumerate(_PAIRS):
            cand_sc[r:r + 1, :] = v1[i] + v2[j]
        cand_sc[len(_PAIRS):, :] = jnp.full((_N_CAND - len(_PAIRS), tt), -jnp.inf, F32)
        rc, cv = _top16_rows(cand_sc[...], _N_CAND)
        z = jnp.zeros((1, tt), F32)
        for k in range(P_TOPK):
            z = z + jnp.exp(cv[k] - cv[0])
        sel_sc[...] = jnp.where(rc < P_TOPK, 1, 0)
        cnt = [jnp.zeros((1, tt), I32) for _ in range(P_TOPK)]
        for r, (i, j) in enumerate(_PAIRS):
            cnt[i] = cnt[i] + sel_sc[r:r + 1, :]
        n1 = jnp.zeros((N_KEYS, tt), I32)
        for i in range(P_TOPK):
            n1 = jnp.where(r1 == i, cnt[i], n1)
        a1_ref[h] = _bf16_pair(jnp.exp(s1 - v1[0]) / z)
        n1_ref[h] = _bf16_pair(n1.astype(F32))
        a2_ref[h] = jnp.exp(s2 - v2[0]).astype(a2_ref.dtype)
        r2_ref[h] = r2.astype(F32).astype(r2_ref.dtype)


def _peer_select(q, keys, tt=256):
    M = q.shape[0]
    tt = min(tt, M)
    tab = pl.BlockSpec((P_HEADS, N_KEYS, tt), lambda i: (0, 0, i))
    shp = lambda dt: jax.ShapeDtypeStruct((P_HEADS, N_KEYS, M), dt)
    return pl.pallas_call(
        _peer_select_body, name="peer_select", grid=(M // tt,),
        in_specs=[pl.BlockSpec((tt, q.shape[1]), lambda i: (i, 0)),
                  pl.BlockSpec(keys.shape, lambda i: (0, 0, 0))],
        out_specs=[tab, tab, tab, tab],
        out_shape=[shp(I32), shp(I32), shp(BF16), shp(BF16)],
        scratch_shapes=[pltpu.VMEM((_N_CAND, tt), F32), pltpu.VMEM((_N_CAND, tt), I32)],
        compiler_params=_params("parallel"),
    )(q, keys)


def _peer_dense_body(h_ref, u_ref, vt_ref, a1_ref, n1_ref, a2_ref, r2_ref, x_ref, gt_ref, o_ref,
                     acc_sc, act_sc, ga_sc, *, te):
    j = pl.program_id(1)
    tt = h_ref.shape[0]

    @pl.when(j == 0)
    def _():
        acc_sc[...] = jnp.zeros(acc_sc.shape, F32)
        act_sc[1] = jnp.zeros(act_sc.shape[1:], F32)

    def step(slot):
        act_sc[slot] = lax.dot_general(u_ref[...], h_ref[...], (((1,), (1,)), ((), ())),
                                       preferred_element_type=F32)
        for r in range(te // N_KEYS):
            rows = slice(r * N_KEYS, (r + 1) * N_KEYS)
            wt = None
            for h in range(P_HEADS):
                tile_rows = lambda ref: pltpu.bitcast(jnp.broadcast_to(ref[h, r:r + 1, :], (8, tt)), BF16)
                a1 = jnp.concatenate([tile_rows(a1_ref)] * (N_KEYS // 16), axis=0)
                n1 = jnp.concatenate([tile_rows(n1_ref)] * (N_KEYS // 16), axis=0)
                term = jnp.where(r2_ref[h] < n1, a1 * a2_ref[h], jnp.zeros((), BF16))
                wt = term if wt is None else wt + term
            ga_sc[rows, :] = wt * _gelu(act_sc[1 - slot, rows, :].astype(BF16))
        acc_sc[...] += jnp.dot(vt_ref[...], ga_sc[...], preferred_element_type=F32)

    pl.when(lax.rem(j, 2) == 0)(functools.partial(step, 0))
    pl.when(lax.rem(j, 2) == 1)(functools.partial(step, 1))

    @pl.when(j == pl.num_programs(1) - 1)
    def _():
        o_ref[...] = x_ref[...] + gt_ref[...] * acc_sc[...].T


def _peer_dense(h2, u, v_t, layer, tabs, x, gt, rows_per_group, tt=512, te=1024):
    M, D = h2.shape
    E = u.shape[1]
    tt = min(tt, M)
    n_e1 = te // N_KEYS
    nc = E // te
    chunk = lambda j: jnp.clip(j, 0, nc - 1)
    tab1 = pl.BlockSpec((P_HEADS, n_e1, tt), lambda i, j: (0, chunk(j - 1), i))
    tab2 = pl.BlockSpec((P_HEADS, N_KEYS, tt), lambda i, j: (0, 0, i))
    r = gt.shape[1]
    return pl.pallas_call(
        functools.partial(_peer_dense_body, te=te),
        name="peer_dense", grid=(M // tt, nc + 1),
        in_specs=[pl.BlockSpec((tt, D), lambda i, j: (i, 0)),
                  pl.BlockSpec((None, te, D), lambda i, j: (layer, chunk(j), 0)),
                  pl.BlockSpec((None, D, te), lambda i, j: (layer, 0, chunk(j - 1))),
                  tab1, tab1, tab2, tab2,
                  pl.BlockSpec((tt, D), lambda i, j: (i, 0)),
                  pl.BlockSpec((None, r, D), lambda i, j: ((i * tt) // rows_per_group, 0, 0))],
        out_specs=pl.BlockSpec((tt, D), lambda i, j: (i, 0)),
        out_shape=jax.ShapeDtypeStruct((M, D), F32),
        scratch_shapes=[pltpu.VMEM((D, tt), F32), pltpu.VMEM((2, te, tt), F32), pltpu.VMEM((te, tt), BF16)],
        compiler_params=_params("parallel", "arbitrary"),
    )(h2, u, v_t, *tabs, x, gt)


def _peer(h2, x, gt, rows_per_group, wq, keys, u, v_t, layer):
    (q,) = _mm(h2, wq, [BF16], tm=512, tn=1024, name="peer_query")
    tabs = _peer_select(q, keys)
    return _peer_dense(h2, u, v_t, layer, tabs, x, gt, rows_per_group)


def _s5_prepare(lam_re, lam_im, log_dt, b_re, b_im, c_re, c_im):
    G, N, P = b_re.shape
    S, W = G * N, G * P
    ab_re, ab_im, bb_re, bb_im = _s5_discretise(lam_re, lam_im, log_dt, b_re, b_im)
    cg = LANES // P
    n_ct = G // cg
    eye = jnp.eye(cg, dtype=F32)

    def tiles(m, rows, cols):
        t = jnp.swapaxes(m, 1, 2).reshape(n_ct, cg, m.shape[2], m.shape[1])
        return (t[:, :, :, None, :] * eye[None, :, None, :, None]).reshape(n_ct, rows, cols)

    bc = jnp.concatenate([tiles(bb_re, LANES, cg * N), tiles(bb_im, LANES, cg * N)], axis=0).astype(BF16)
    cc = jnp.concatenate([tiles(c_re.astype(F32), cg * N, LANES), -tiles(c_im.astype(F32), cg * N, LANES)],
                         axis=0).astype(BF16)
    a_row = jnp.concatenate([ab_re.reshape(1, S), ab_im.reshape(1, S)], axis=1)
    return a_row, bc, cc


def _s5(u, T, x0_re, x0_im, prep, d_skip, tc=128, lc=1024):
    M, W = u.shape
    B = M // T
    G, N = x0_re.shape[1:]
    S = G * N
    a_row, bc, cc = prep
    tc = min(tc, T)
    x0 = jnp.concatenate([x0_re.reshape(B, 1, S), x0_im.reshape(B, 1, S)], axis=2).astype(F32)
    yg, x_last = pl.pallas_call(
        functools.partial(_s5_body, tc=tc, S=S, lc=lc),
        name="s5", grid=(B, T // tc),
        in_specs=[pl.BlockSpec((1, tc, W), lambda b, c: (b, c, 0)),
                  pl.BlockSpec(bc.shape, lambda b, c: (0, 0, 0)),
                  pl.BlockSpec(cc.shape, lambda b, c: (0, 0, 0)),
                  pl.BlockSpec((1, 2 * S), lambda b, c: (0, 0)),
                  pl.BlockSpec((1, W), lambda b, c: (0, 0)),
                  pl.BlockSpec((1, 1, 2 * S), lambda b, c: (b, 0, 0))],
        out_specs=[pl.BlockSpec((1, tc, W), lambda b, c: (b, c, 0)),
                   pl.BlockSpec((1, 1, 2 * S), lambda b, c: (b, 0, 0))],
        out_shape=[jax.ShapeDtypeStruct((B, T, W), BF16), jax.ShapeDtypeStruct((B, 1, 2 * S), F32)],
        scratch_shapes=[pltpu.VMEM((tc, 2 * S), F32), pltpu.VMEM((1, 2 * S), F32)],
        compiler_params=_params("parallel", "arbitrary"),
    )(u.reshape(B, T, W), bc, cc, a_row, d_skip.reshape(1, W).astype(F32), x0)
    return yg.reshape(M, W), x_last[:, 0, :S].reshape(B, G, N), x_last[:, 0, S:].reshape(B, G, N)


def _logsig_body(f_ref, b_ref, o_ref):
    o_ref[...] = _log_sigmoid(f_ref[...] + b_ref[...])


def _logsig(f, b_row):
    return pl.pallas_call(_logsig_body, out_shape=jax.ShapeDtypeStruct(f.shape, F32))(f, b_row)


def _head_lane_select(n_lanes):
    lane = lax.broadcasted_iota(I32, (N_HEADS, n_lanes), 1)
    head = lax.broadcasted_iota(I32, (N_HEADS, n_lanes), 0)
    return (lane & (N_HEADS - 1)) == head, lane


def _page_prefix_body(lf_ref, pre_ref, tot_ref):
    lf = lf_ref[...]
    n_lanes = lf.shape[1]
    lane = lax.broadcasted_iota(I32, lf.shape, 1)
    pre, tot = lf, lf
    sh = N_HEADS
    while sh < n_lanes:
        pre = pre + jnp.where(lane >= sh, pltpu.roll(pre, sh, axis=1), 0.0)
        tot = tot + pltpu.roll(tot, sh, axis=1)
        sh *= 2
    pre_ref[...] = pre
    tot_ref[...] = tot


def _page_prefix(logf_rows):
    R, n_lanes = logf_rows.shape
    tr = math.gcd(R, 256)
    spec = pl.BlockSpec((tr, n_lanes), lambda i: (i, 0))
    return pl.pallas_call(
        _page_prefix_body, name="fox_page_prefix", grid=(R // tr,), in_specs=[spec], out_specs=[spec, spec],
        out_shape=[jax.ShapeDtypeStruct((R, n_lanes), F32)] * 2,
        compiler_params=_params("parallel"),
    )(logf_rows)


def _fox_sample_body(pt_ref, q_ref, kn_ref, vn_ref, lfn_ref, k_hbm, v_hbm, pre_hbm, tot_hbm, o_ref,
                     kbuf, vbuf, pbuf, tbuf, sems, m_sc, l_sc, acc_sc, carry_sc,
                     *, layer, n_pool, n_pages, pp, scale):
    g = pl.program_id(1)
    ng = pl.num_programs(1)
    step = pl.program_id(0) * ng + g
    n_steps = pl.num_programs(0) * ng
    slot = lax.rem(step, 2)
    page_lanes = PAGE * N_HEADS
    n_lanes = pp * page_lanes
    nt = (((1,), (1,)), ((), ()))

    def copies(st, sl):
        first = (st // ng) * n_pages + lax.rem(st, ng) * pp
        out = []
        for k in range(pp):
            pg = pt_ref[first + k]
            row = layer * n_pool + pg
            out += [pltpu.make_async_copy(k_hbm.at[layer, pg], kbuf.at[sl, pl.ds(k * PAGE, PAGE)], sems.at[0, sl]),
                    pltpu.make_async_copy(v_hbm.at[layer, pg], vbuf.at[sl, pl.ds(k * PAGE, PAGE)], sems.at[1, sl]),
                    pltpu.make_async_copy(pre_hbm.at[pl.ds(row, 1)], pbuf.at[sl, pl.ds(k, 1)], sems.at[2, sl]),
                    pltpu.make_async_copy(tot_hbm.at[pl.ds(row, 1)], tbuf.at[sl, pl.ds(k, 1)], sems.at[3, sl])]
        return out

    @pl.when(step == 0)
    def _():
        for c in copies(step, slot):
            c.start()

    @pl.when(step + 1 < n_steps)
    def _():
        for c in copies(step + 1, 1 - slot):
            c.start()

    @pl.when(g == 0)
    def _():
        m_sc[...] = jnp.full(m_sc.shape, -jnp.inf, F32)
        l_sc[...] = jnp.zeros(l_sc.shape, F32)
        acc_sc[...] = jnp.zeros(acc_sc.shape, F32)
        carry_sc[...] = jnp.zeros(carry_sc.shape, F32)

    for c in copies(step, slot):
        c.wait()

    hsel, lane8 = _head_lane_select(n_lanes)
    q = q_ref[0]
    kp = kbuf[slot].reshape(n_lanes, HEAD_DIM).astype(BF16)
    s = lax.dot_general(q, kp, nt, preferred_element_type=F32) * scale
    carry = carry_sc[...]
    f_pages = []
    for k in range(pp):
        f_pages.append(carry + pbuf[slot, k:k + 1, :])
        carry = carry + tbuf[slot, k:k + 1, :]
    carry_sc[...] = carry
    f_k = jnp.concatenate(f_pages, axis=1) if pp > 1 else f_pages[0]
    s = jnp.where(hsel, s - f_k, NEG)
    m_prev = m_sc[...]
    m_new = jnp.maximum(m_prev, jnp.max(s, axis=1, keepdims=True))
    alpha = jnp.exp(m_prev - m_new)
    pr = jnp.exp(s - m_new)
    l_sc[...] = alpha * l_sc[...] + jnp.sum(pr, axis=1, keepdims=True)
    vp = vbuf[slot].reshape(n_lanes, HEAD_DIM).astype(BF16)
    acc_sc[...] = alpha * acc_sc[...] + jnp.dot(pr.astype(BF16), vp, preferred_element_type=F32)
    m_sc[...] = m_new

    @pl.when(g == ng - 1)
    def _():
        hsel_p, lane_p = _head_lane_select(page_lanes)
        f_col = jnp.sum(jnp.where(hsel_p, jnp.where(lane_p < N_HEADS, carry_sc[...], 0.0), 0.0),
                        axis=1, keepdims=True)
        kn = kn_ref[0].astype(BF16).astype(F32)
        vn = vn_ref[0].astype(BF16).astype(F32)
        s_n = jnp.sum(q.astype(F32) * kn, axis=1, keepdims=True) * scale - (f_col + lfn_ref[0])
        m_prev = m_sc[...]
        m_new = jnp.maximum(m_prev, s_n)
        alpha = jnp.exp(m_prev - m_new)
        p_n = jnp.exp(s_n - m_new)
        l = alpha * l_sc[...] + p_n
        acc = alpha * acc_sc[...] + p_n.astype(BF16).astype(F32) * vn
        o_ref[0] = acc / l


def _fox_sample(layer, pt_flat, q, k_new, v_new, logf_new, cache_k, cache_v, page_pre, page_tot, pp=16):
    Bd = q.shape[0]
    n_pool = cache_k.shape[1]
    n_pages = pt_flat.shape[0] // Bd
    pp = math.gcd(pp, n_pages)
    tok = lambda w: pl.BlockSpec((1, N_HEADS, w), lambda b, g, pt: (b, 0, 0))
    hbm = pl.BlockSpec(memory_space=pl.ANY)
    grid_spec = pltpu.PrefetchScalarGridSpec(
        num_scalar_prefetch=1, grid=(Bd, n_pages // pp),
        in_specs=[tok(HEAD_DIM), tok(HEAD_DIM), tok(HEAD_DIM), tok(1), hbm, hbm, hbm, hbm],
        out_specs=pl.BlockSpec((1, N_HEADS, HEAD_DIM), lambda b, g, pt: (b, 0, 0)),
        scratch_shapes=[pltpu.VMEM((2, pp * PAGE, N_HEADS, HEAD_DIM), F32),
                        pltpu.VMEM((2, pp * PAGE, N_HEADS, HEAD_DIM), F32),
                        pltpu.VMEM((2, pp, PAGE * N_HEADS), F32), pltpu.VMEM((2, pp, PAGE * N_HEADS), F32),
                        pltpu.SemaphoreType.DMA((4, 2)),
                        pltpu.VMEM((N_HEADS, 1), F32), pltpu.VMEM((N_HEADS, 1), F32),
                        pltpu.VMEM((N_HEADS, HEAD_DIM), F32), pltpu.VMEM((1, PAGE * N_HEADS), F32)])
    return pl.pallas_call(
        functools.partial(_fox_sample_body, layer=layer, n_pool=n_pool, n_pages=n_pages, pp=pp,
                          scale=HEAD_DIM ** -0.5),
        name="fox_sample", grid_spec=grid_spec,
        out_shape=jax.ShapeDtypeStruct((Bd, N_HEADS, HEAD_DIM), F32),
        compiler_params=_params("arbitrary", "arbitrary"),
    )(pt_flat, q, k_new, v_new, logf_new, cache_k, cache_v, page_pre, page_tot)


def _idx_scores(qi, w, ki, keys_on_lanes=False):
    contract = (((1,), (0,)), ((), ())) if keys_on_lanes else (((1,), (1,)), ((), ()))
    d = lax.dot_general(qi, ki.astype(BF16), contract, preferred_element_type=F32)
    return jnp.sum(w * (H_IDX ** -0.5 * D_IDX ** -0.5) * jnp.maximum(d, 0.0), axis=0, keepdims=True)


def _idx_scores_body(qi_ref, w_ref, ki_ref, o_ref):
    o_ref[0] = _idx_scores(qi_ref[0], w_ref[0], ki_ref[...])


def _idx_scores_paged_body(pt_ref, qi_ref, w_ref, ki_hbm, o_ref, buf, sem, *, layer, n_pages, chunk):
    b = pl.program_id(0)

    def page_copy(p):
        return pltpu.make_async_copy(ki_hbm.at[layer, pt_ref[b * n_pages + p]],
                                     buf.at[:, pl.ds(pl.multiple_of(p * PAGE, PAGE), PAGE)], sem.at[0])

    def start(p, c):
        page_copy(p).start()
        return c

    def wait(p, c):
        page_copy(p).wait()
        return c

    lax.fori_loop(0, n_pages, start, 0)
    lax.fori_loop(0, n_pages, wait, 0)
    for c in range(n_pages * PAGE // chunk):
        o_ref[0, :, c * chunk:(c + 1) * chunk] = _idx_scores(qi_ref[0], w_ref[0], buf[:, c * chunk:(c + 1) * chunk],
                                                             keys_on_lanes=True)


def _idx_scores_paged(layer, pt_flat, qi, w, cache_ki_t, chunk=2048):
    Bd = qi.shape[0]
    n_pages = pt_flat.shape[0] // Bd
    chunk = math.gcd(chunk, n_pages * PAGE)
    grid_spec = pltpu.PrefetchScalarGridSpec(
        num_scalar_prefetch=1, grid=(Bd,),
        in_specs=[pl.BlockSpec((1, H_IDX, D_IDX), lambda b, pt: (b, 0, 0)),
                  pl.BlockSpec((1, H_IDX, 1), lambda b, pt: (b, 0, 0)),
                  pl.BlockSpec(memory_space=pl.ANY)],
        out_specs=pl.BlockSpec((1, 1, n_pages * PAGE), lambda b, pt: (b, 0, 0)),
        scratch_shapes=[pltpu.VMEM((D_IDX, n_pages * PAGE), F32), pltpu.SemaphoreType.DMA((1,))])
    return pl.pallas_call(
        functools.partial(_idx_scores_paged_body, layer=layer, n_pages=n_pages, chunk=chunk),
        name="dsa_sample_scores", grid_spec=grid_spec,
        out_shape=jax.ShapeDtypeStruct((Bd, 1, n_pages * PAGE), F32),
        compiler_params=_params("arbitrary"),
    )(pt_flat, qi, w, cache_ki_t)


def _idx_scores_new(qi, w, ki_rows):
    Bd, R, _ = ki_rows.shape
    return pl.pallas_call(
        _idx_scores_body, grid=(Bd,),
        in_specs=[pl.BlockSpec((1, H_IDX, D_IDX), lambda b: (b, 0, 0)),
                  pl.BlockSpec((1, H_IDX, 1), lambda b: (b, 0, 0)),
                  pl.BlockSpec((None, R, D_IDX), lambda b: (b, 0, 0))],
        out_specs=pl.BlockSpec((1, 1, R), lambda b: (b, 0, 0)),
        out_shape=jax.ShapeDtypeStruct((Bd, 1, R), F32),
        compiler_params=_params("parallel"),
    )(qi, w, ki_rows)


def _dsa_sample_select_body(sc_ref, scn_ref, rel_ref, idx_ref, sb_ref, nb_ref, rs_sc, *, topk, chunk):
    Bd, P = sc_ref.shape
    key_p = _sortable_key(sc_ref[...])
    key_n = _sortable_key(scn_ref[:, 0:1])
    col = lax.broadcasted_iota(I32, (Bd, P), 1)

    one = lambda pred: jnp.where(pred, 1, 0)

    def count(ind_p, ind_n):
        return jnp.sum(ind_p, axis=1, keepdims=True) + ind_n

    thr = _kth_largest_key(lambda t: count(one(key_p >= t), one(key_n >= t)), (Bd, 1), topk)
    n_gt = count(one(key_p > thr), one(key_n > thr))
    n_ge = count(one(key_p >= thr), one(key_n >= thr))
    need = topk - n_gt
    nbits = (P + 1).bit_length()

    def body(b, cut):
        cand = cut | lax.shift_left(jnp.int32(1), nbits - 1 - b)
        cnt = count(jnp.where(key_p == thr, one(col < cand), 0), jnp.where(key_n == thr, one(P < cand), 0))
        return jnp.where(cnt < need, cand, cut)
    cut = lax.fori_loop(0, nbits, body, jnp.zeros((Bd, 1), I32))
    cut = jnp.where(n_ge > topk, cut, jnp.int32(2 ** 30))
    sel_p = jnp.where(key_p > thr, 1, jnp.where(key_p == thr, one(col <= cut), 0))
    sel_n = jnp.where(key_n > thr, 1, jnp.where(key_n == thr, one(P <= cut), 0))
    rank = sel_p
    sh = 1
    while sh < P:
        rank = rank + jnp.where(col >= sh, pltpu.roll(rank, sh, axis=1), 0)
        sh *= 2
    rs_sc[...] = sel_p * rank
    n_past = jnp.sum(sel_p, axis=1, keepdims=True)
    nb_ref[...] = jnp.where(sel_n > 0, rel_ref[0:1, :], NEG)

    slot = lax.broadcasted_iota(I32, (topk, 1), 0)
    ccol = lax.broadcasted_iota(I32, (topk, chunk), 1)
    for b in range(Bd):
        idx = jnp.zeros((topk, 1), I32)
        for c in range(P // chunk):
            rs = rs_sc[b:b + 1, c * chunk:(c + 1) * chunk]
            idx = idx + jnp.sum(jnp.where(rs == slot + 1, ccol + c * chunk, 0), axis=1, keepdims=True)
        idx_ref[b] = idx
        bucket = _t5_bucket(P - idx)
        bias = jnp.zeros((topk, N_HEADS), F32)
        for k in range(N_BUCKETS):
            bias = bias + jnp.where(bucket == k, rel_ref[k:k + 1, :], 0.0)
        sb_ref[b] = jnp.where(slot < n_past[b:b + 1, :], bias, NEG)


def _dsa_sample_select(sc, sc_new, rel_table, topk, chunk=2048):
    Bd, P = sc.shape
    return pl.pallas_call(
        functools.partial(_dsa_sample_select_body, topk=topk, chunk=min(chunk, P)),
        out_shape=[jax.ShapeDtypeStruct((Bd, topk, 1), I32), jax.ShapeDtypeStruct((Bd, topk, N_HEADS), F32),
                   jax.ShapeDtypeStruct((Bd, N_HEADS), F32)],
        scratch_shapes=[pltpu.VMEM((Bd, P), I32)],
        compiler_params=pltpu.CompilerParams(vmem_limit_bytes=V7X_VMEM_LIMIT_BYTES),
    )(sc, sc_new, rel_table)


def _dsa_sample_attend_body(idx_ref, pt_ref, q_ref, kn_ref, vn_ref, sb_ref, nb_ref, kc_hbm, vc_hbm, o_ref,
                            kbuf, vbuf, sems, *, layer, topk, n_pages, scale):
    b = pl.program_id(0)

    def row_copies(j):
        i = idx_ref[b * topk + j]
        pg = pt_ref[b * n_pages + lax.shift_right_logical(i, 7)]
        off = i & (PAGE - 1)
        return (pltpu.make_async_copy(kc_hbm.at[layer, pg, off], kbuf.at[j], sems.at[0]),
                pltpu.make_async_copy(vc_hbm.at[layer, pg, off], vbuf.at[j], sems.at[1]))

    def start(j, c):
        ck, cv = row_copies(j)
        ck.start()
        cv.start()
        return c

    def wait(j, c):
        ck, cv = row_copies(j)
        ck.wait()
        cv.wait()
        return c

    lax.fori_loop(0, topk, start, 0)
    lax.fori_loop(0, topk, wait, 0)

    n_lanes = topk * N_HEADS
    hsel, _ = _head_lane_select(n_lanes)
    q = q_ref[0]
    kb = kbuf[...].reshape(n_lanes, HEAD_DIM).astype(BF16)
    s = lax.dot_general(q, kb, (((1,), (1,)), ((), ())), preferred_element_type=F32) * scale
    s = jnp.where(hsel, s + sb_ref[0], NEG)
    kn = kn_ref[0].astype(BF16).astype(F32)
    vn = vn_ref[0].astype(BF16).astype(F32)
    s_n = jnp.sum(q.astype(F32) * kn, axis=1, keepdims=True) * scale + nb_ref[0]
    m = jnp.maximum(jnp.max(s, axis=1, keepdims=True), s_n)
    p = jnp.exp(s - m)
    p_n = jnp.exp(s_n - m)
    l = jnp.sum(p, axis=1, keepdims=True) + p_n
    vb = vbuf[...].reshape(n_lanes, HEAD_DIM).astype(BF16)
    acc = jnp.dot(p.astype(BF16), vb, preferred_element_type=F32) + p_n.astype(BF16).astype(F32) * vn
    o_ref[0] = acc / l


def _dsa_sample_attend(layer, idx_flat, pt_flat, q, k_new, v_new, slot_bias, new_bias, cache_k, cache_v, topk):
    Bd = q.shape[0]
    n_pages = pt_flat.shape[0] // Bd
    tok = lambda w: pl.BlockSpec((1, N_HEADS, w), lambda b, idx, pt: (b, 0, 0))
    grid_spec = pltpu.PrefetchScalarGridSpec(
        num_scalar_prefetch=2, grid=(Bd,),
        in_specs=[tok(HEAD_DIM), tok(HEAD_DIM), tok(HEAD_DIM),
                  pl.BlockSpec((1, 1, topk * N_HEADS), lambda b, idx, pt: (b, 0, 0)),
                  tok(1),
                  pl.BlockSpec(memory_space=pl.ANY), pl.BlockSpec(memory_space=pl.ANY)],
        out_specs=pl.BlockSpec((1, N_HEADS, HEAD_DIM), lambda b, idx, pt: (b, 0, 0)),
        scratch_shapes=[pltpu.VMEM((topk, N_HEADS, HEAD_DIM), F32), pltpu.VMEM((topk, N_HEADS, HEAD_DIM), F32),
                        pltpu.SemaphoreType.DMA((2,))])
    return pl.pallas_call(
        functools.partial(_dsa_sample_attend_body, layer=layer, topk=topk, n_pages=n_pages,
                          scale=HEAD_DIM ** -0.5),
        grid_spec=grid_spec,
        out_shape=jax.ShapeDtypeStruct((Bd, N_HEADS, HEAD_DIM), F32),
        compiler_params=_params("arbitrary"),
    )(idx_flat, pt_flat, q, k_new, v_new, slot_bias, new_bias, cache_k, cache_v)


W_ATT = N_HEADS * HEAD_DIM
_PROJ_NAMES = ("fq", "fk", "fv", "ff", "bq", "bk", "bv", "iq", "ik", "iw", "su", "gl")
_FF_LANES = slice(0, N_HEADS)
_IW_LANES = slice(N_HEADS, N_HEADS + H_IDX)
_IK_LANES = slice(N_HEADS + H_IDX, N_HEADS + H_IDX + D_IDX)


def _stacked_weights(p):
    D = p["w_in"].shape[1]
    w_c = p["w_glu"].shape[1]
    sizes = (W_ATT, W_ATT, W_ATT, N_HEADS, W_ATT, W_ATT, W_ATT, H_IDX * D_IDX, D_IDX, H_IDX, w_c, 3 * D)
    start, off = {}, 0
    for name, n in zip(_PROJ_NAMES, sizes):
        start[name] = (off, n)
        off += n
    w_in = p["w_in"]
    grab = lambda names: w_in[:, :, start[names[0]][0]:start[names[-1]][0] + start[names[-1]][1]]
    pad = jnp.zeros(w_in.shape[:2] + (LANES - (N_HEADS + H_IDX + D_IDX),), w_in.dtype)
    runs = (("fq", "fk", "fv"), ("bq", "bk", "bv", "iq"), ("su", "gl"))
    sw = {"proj": {}}
    for names in runs:
        arr = grab(names).astype(BF16)
        for n in names:
            sw["proj"][n] = (arr, (start[n][0] - start[names[0]][0], start[n][1]))
    small = jnp.concatenate([grab(("ff",)), grab(("iw",)), grab(("ik",)), pad], axis=2).astype(BF16)
    sw["proj"]["small"] = (small, (0, LANES))
    sw["u"] = p["peer_u"].astype(BF16)
    sw["v_t"] = jnp.swapaxes(p["peer_v"], 1, 2).astype(BF16)
    return sw


def _layer_weights(l, p, sw):
    w_c = p["w_glu"].shape[1]
    lw = {"layer": l, "proj": sw["proj"], "u": sw["u"], "v_t": sw["v_t"]}
    w_glu, w_br = p["w_glu"][l], p["w_br"][l]
    lw["glu_a"], lw["glu_b"] = w_glu[:, :w_c].astype(BF16), w_glu[:, w_c:].astype(BF16)
    lw["wa"] = w_br[:W_ATT].astype(BF16)
    lw["wb"] = w_br[W_ATT:2 * W_ATT].astype(BF16)
    lw["wc"] = w_br[2 * W_ATT:].astype(BF16)
    lw["w_out"] = p["w_out"][l].astype(BF16)
    lw["wq"] = p["peer_wq"][l].astype(BF16)
    lw["keys"] = p["peer_keys"][l].reshape(2 * P_HEADS, N_KEYS, -1).astype(BF16)
    lw["s5"] = _s5_prepare(p["s5_lam_re"][l], p["s5_lam_im"][l], p["s5_log_dt"][l], p["s5_b_re"][l],
                           p["s5_b_im"][l], p["s5_c_re"][l], p["s5_c_im"][l])
    lw["s5_d"] = p["s5_d"][l]
    lw["norm1_g"], lw["norm2_g"] = p["norm1_g"][l], p["norm2_g"][l]
    return lw


def _layer(x, mods, rows_per_group, T, lw, s5_re0, s5_im0, attend):
    M, D = x.shape
    sh1, sc1, gt1, sh2, sc2, gt2 = mods
    tm = min(1024, M)
    h = _norm_mod(x, lw["norm1_g"], sc1, sh1, rows_per_group)
    proj = lambda name, dts: _mm(h, lw["proj"][name][0], dts, tm=tm, tn=1024, order="nm", name="proj_" + name,
                                 cols=lw["proj"][name][1], layer=lw["layer"])
    (fq,) = proj("fq", [BF16])
    fk, fk16 = proj("fk", [F32, BF16])
    fv, fv16 = proj("fv", [F32, BF16])
    (bq,) = proj("bq", [BF16])
    bk, bk16 = proj("bk", [F32, BF16])
    bv, bv16 = proj("bv", [F32, BF16])
    (iq,) = proj("iq", [BF16])
    (su,) = proj("su", [F32])
    (small,) = proj("small", [F32])
    (gl,) = proj("gl", [F32])
    o_a, o_b, logf = attend(fq, fk, fk16, fv, fv16, bq, bk, bk16, bv, bv16, iq, small)
    yg, s5_re, s5_im = _s5(su, T, s5_re0, s5_im0, lw["s5"], lw["s5_d"])
    o_c = _glu(yg, lw["glu_a"], lw["glu_b"])
    merged = _merge(o_a, o_b, o_c, lw["wa"], lw["wb"], lw["wc"], gl)
    tn = 1024
    (x1,) = _mm(merged, lw["w_out"], [F32], tm=tm, tn=tn,
                epilogue=lambda acc, x_, g_: (x_ + g_ * acc,),
                extras=[(x, (tm, tn), lambda i, j: (i, j)), _mod_extra(gt1, rows_per_group, tm, tn)],
                name="out_proj")
    h2 = _norm_mod(x1, lw["norm2_g"], sc2, sh2, rows_per_group)
    if M < LANES:
        padr = lambda a: jnp.pad(a, ((0, LANES - M), (0, 0)))
        gt2p = jnp.pad(gt2, ((0, 0), (0, LANES - M), (0, 0)))
        x2 = _peer(padr(h2), padr(x1), gt2p, LANES, lw["wq"], lw["keys"], lw["u"], lw["v_t"], lw["layer"])[:M]
    else:
        x2 = _peer(h2, x1, gt2, rows_per_group, lw["wq"], lw["keys"], lw["u"], lw["v_t"], lw["layer"])
    return x2, (fk, fv, logf, bk, bv, small[:, _IK_LANES], s5_re, s5_im)


def kernel(x_prompt, x_sample, cache_fox_k, cache_fox_v, cache_fox_logf, cache_dsa_k, cache_dsa_v,
           cache_dsa_idx_k, state_s5_re, state_s5_im, page_table, c_prompt, c_sample,
           w_ada, b_ada, norm1_g, norm2_g, w_in, b_f, rel_table, s5_lam_re, s5_lam_im, s5_log_dt,
           s5_b_re, s5_b_im, s5_c_re, s5_c_im, s5_d, w_glu, w_br, w_out,
           peer_wq, peer_keys, peer_u, peer_v, final_norm_g):
    p = dict(w_in=w_in, w_glu=w_glu, w_br=w_br, w_out=w_out, peer_wq=peer_wq, peer_keys=peer_keys,
             peer_u=peer_u, peer_v=peer_v, s5_lam_re=s5_lam_re, s5_lam_im=s5_lam_im, s5_log_dt=s5_log_dt,
             s5_b_re=s5_b_re, s5_b_im=s5_b_im, s5_c_re=s5_c_re, s5_c_im=s5_c_im, s5_d=s5_d,
             norm1_g=norm1_g, norm2_g=norm2_g)
    depth = w_in.shape[0]
    B, T, D = x_prompt.shape
    Bd = x_sample.shape[0]
    assert x_sample.shape[1] == 1
    n_pool = cache_fox_k.shape[1]
    n_pages = page_table.shape[1]
    past = n_pages * PAGE
    tile = min(512, T)
    topk_p = min(TOPK_MAX, T // 4)
    topk_s = min(TOPK_MAX, (past + 1) // 4)
    pt_flat = page_table.reshape(-1).astype(I32)
    page_pre, page_tot = _page_prefix(cache_fox_logf.reshape(depth * n_pool, PAGE * N_HEADS).astype(F32))
    cache_ki_t = jnp.swapaxes(cache_dsa_idx_k, 2, 3)
    sw = _stacked_weights(p)
    rel = rel_table.astype(F32)
    tz = _t5_tiles(rel)

    n_c = B + Bd
    c_all = jnp.pad(jnp.concatenate([c_prompt, c_sample], axis=0), ((0, (-n_c) % 8), (0, 0)))
    xp = x_prompt.reshape(B * T, D)
    xs = x_sample.reshape(Bd, D)
    rows_p, rows_s = [], []
    for l in range(depth):
        lw = _layer_weights(l, p, sw)
        m = _adaln(c_all, w_ada, b_ada, l)
        mods_p = [a[:B, None, :] for a in jnp.split(m, 6, axis=1)]
        mods_s = [a[None, B:n_c, :] for a in jnp.split(m, 6, axis=1)]
        b_f_l = b_f[l].astype(F32)

        def attend_prompt(fq, fk, fk16, fv, fv16, bq, bk, bk16, bv, bv16, iq, small):
            r3 = lambda a: a.reshape(B, T, a.shape[-1])
            small3 = r3(small)
            f_t = jnp.moveaxis(small3[:, :, _FF_LANES], -1, 1).reshape(B * N_HEADS, T)
            logf_t, cum = _gate_cumsum(f_t, jnp.tile(b_f_l, B).reshape(B * N_HEADS, 1))
            logf = jnp.moveaxis(logf_t.reshape(B, N_HEADS, T), 1, -1)
            o_a = _flash("fox", r3(fq), r3(fk16), r3(fv16), cum.reshape(B, N_HEADS, T) * LOG2E, tile=tile)
            ki_t = jnp.swapaxes(small3[:, :, _IK_LANES], 1, 2).astype(BF16)
            mask = _dsa_select(r3(iq), small3, ki_t, topk_p)
            o_b = _flash("dsa", r3(bq), r3(bk16), r3(bv16), mask, tz, tile=tile)
            return o_a.reshape(B * T, W_ATT), o_b.reshape(B * T, W_ATT), logf

        def attend_sample(fq, fk, fk16, fv, fv16, bq, bk, bk16, bv, bv16, iq, small, l=l):
            h3 = lambda a: a.reshape(Bd, N_HEADS, HEAD_DIM)
            logf = _logsig(small[:, _FF_LANES], b_f_l.reshape(1, N_HEADS))
            o_a = _fox_sample(l, pt_flat, h3(fq), h3(fk), h3(fv), logf.reshape(Bd, N_HEADS, 1),
                              cache_fox_k, cache_fox_v, page_pre, page_tot)
            qi3 = iq.reshape(Bd, H_IDX, D_IDX)
            w3 = small[:, _IW_LANES].reshape(Bd, H_IDX, 1)
            sc = _idx_scores_paged(l, pt_flat, qi3, w3, cache_ki_t)
            ki_new = jnp.pad(small[:, _IK_LANES].reshape(Bd, 1, D_IDX), ((0, 0), (0, LANES - 1), (0, 0)))
            sc_new = _idx_scores_new(qi3, w3, ki_new)
            idx, sb, nb = _dsa_sample_select(sc.reshape(Bd, past), sc_new.reshape(Bd, LANES), rel, topk_s)
            o_b = _dsa_sample_attend(l, idx.reshape(-1), pt_flat, h3(bq), h3(bk), h3(bv),
                                     sb.reshape(Bd, 1, topk_s * N_HEADS), nb.reshape(Bd, N_HEADS, 1),
                                     cache_dsa_k, cache_dsa_v, topk_s)
            return (o_a.reshape(Bd, W_ATT).astype(BF16), o_b.reshape(Bd, W_ATT).astype(BF16), logf)

        zero_state = jnp.zeros((B,) + state_s5_re.shape[2:], F32)
        xp, rp = _layer(xp, mods_p, T, T, lw, zero_state, zero_state, attend_prompt)
        xs, rs = _layer(xs, mods_s, Bd, 1, lw, state_s5_re[l], state_s5_im[l], attend_sample)
        rows_p.append(rp)
        rows_s.append(rs)

    y_prompt = _norm(xp, final_norm_g).reshape(B, T, D)
    y_sample = _norm(xs, final_norm_g).reshape(Bd, 1, D)

    def leaves(rows, nb, nt):
        fk, fv, fl, bk, bv, ik, sr, si = [jnp.stack(a) for a in zip(*rows)]
        hd = (depth, nb, nt, N_HEADS, HEAD_DIM)
        return (fk.reshape(hd), fv.reshape(hd), fl.reshape(depth, nb, nt, N_HEADS), bk.reshape(hd),
                bv.reshape(hd), ik.reshape(depth, nb, nt, D_IDX), sr, si)

    return (y_prompt, y_sample) + leaves(rows_p, B, T) + leaves(rows_s, Bd, 1)
```
